```python
import math
import jax, jax.numpy as jnp
from jax import lax
import numpy as np

D_MODEL = 1024
BATCH = 16
SEQ = 2048
DEPTH = 4

GRID_W = 64
CTX_LEN = 256

D_MIX = D_MODEL
D_SSM = D_MIX // 2
D_CONV = D_MIX - D_SSM
SSM_H = 16
SSM_G = D_SSM // SSM_H
SSM_P = 64
CONV_K = 31
CONV_PAD = CONV_K // 2
N_EXPERTS = 64
TOP_K = 8
N_EXPERT_GROUPS = 8
TOP_K_GROUPS = 4
D_EXPERT = D_MODEL // 4
ROUTED_SCALE = 2.5
DEEPNORM_ALPHA = (2 * DEPTH) ** 0.25
DEEPNORM_BETA = (8 * DEPTH) ** -0.25
LN_EPS = 1e-5
DT_MIN = 1e-3
DT_MAX = 1e-1

kernel_name = 'hymba_s5_conformer_moe_deepnorm_dit'

F32 = jnp.float32


def _layernorm(x, g=None, b=None):
    xf = x.astype(F32)
    mu = jnp.mean(xf, -1, keepdims=True)
    var = jnp.mean(jnp.square(xf - mu), -1, keepdims=True)
    y = (xf - mu) * lax.rsqrt(var + LN_EPS)
    if g is not None:
        y = y * g.astype(F32) + b.astype(F32)
    return y.astype(x.dtype)


def _rmsnorm(x, g):
    xf = x.astype(F32)
    y = xf * lax.rsqrt(jnp.mean(jnp.square(xf), -1, keepdims=True) + LN_EPS)
    return (y * g.astype(F32)).astype(x.dtype)


def _modulate(x, shift, scale):
    return _layernorm(x) * (1 + scale) + shift


def _ssm_combine(left, right):
    a_l, b_l = left
    a_r, b_r = right
    return a_l * a_r, a_r * b_l + b_r


def _s5_discretise(a_re, a_im, log_dt, b_re, b_im):
    lam = lax.complex(a_re.astype(F32), a_im.astype(F32))
    dt = jnp.exp(log_dt.astype(F32))[:, None]
    lam_dt = lam * dt
    lam_bar = jnp.exp(lam_dt)
    b = lax.complex(b_re.astype(F32), b_im.astype(F32))
    b_bar = ((lam_bar - 1.0) / lam)[..., None] * b
    return lam_dt, lam_bar, b_bar


def _s5_scans(u, p, h0):
    bsz, length, _ = u.shape
    ug = u.astype(F32).reshape(bsz, length, SSM_G, SSM_H)
    states = []
    for d in range(2):
        lam_dt, lam_bar, b_bar = _s5_discretise(p['a_re'][d], p['a_im'][d], p['log_dt'][d],
                                                p['b_re'][d], p['b_im'][d])
        ud = ug if d == 0 else jnp.flip(ug, 1)
        bu = jnp.einsum('blgh,gph->blgp', ud, b_bar)
        a = jnp.broadcast_to(lam_bar, bu.shape)
        _, hs = lax.associative_scan(_ssm_combine, (a, bu), axis=1)
        if h0 is not None:
            steps = jnp.arange(1, length + 1, dtype=F32)[:, None, None]
            hs = hs + jnp.exp(steps * lam_dt)[None] * h0[d][:, None]
        states.append(hs)
    return states


def _finals(states):
    return jnp.stack([states[0][:, -1], states[1][:, -1]])


def _s5_readout(u, states, p):
    bsz, length, _ = u.shape
    ug = u.astype(F32).reshape(bsz, length, SSM_G, SSM_H)
    c_f = lax.complex(p['c_re'][0].astype(F32), p['c_im'][0].astype(F32))
    c_b = lax.complex(p['c_re'][1].astype(F32), p['c_im'][1].astype(F32))
    y_f = jnp.einsum('ghp,blgp->blgh', c_f, states[0]).real
    y_b = jnp.flip(jnp.einsum('ghp,blgp->blgh', c_b, states[1]).real, 1)
    y = y_f + y_b + p['d_skip'].astype(F32) * ug
    y = jax.nn.gelu(y.reshape(bsz, length, D_SSM)).astype(u.dtype)
    return y * jax.nn.sigmoid(y @ p['w_glu'] + p['b_glu'])


def _dwconv(z, w, bias, rows):
    bsz, length, ch = z.shape
    dn = ('NHWC', 'HWIO', 'NHWC')
    if rows is None:
        y = lax.conv_general_dilated(z[:, None], w.reshape(1, CONV_K, 1, ch), (1, 1),
                                     ((0, 0), (CONV_PAD, CONV_PAD)), dimension_numbers=dn,
                                     feature_group_count=ch)[:, 0]
    else:
        half = ch // 2
        z4 = z.reshape(bsz, rows, GRID_W, ch)
        yh = lax.conv_general_dilated(z4[..., :half], w[:, :half].reshape(1, CONV_K, 1, half), (1, 1),
                                      ((0, 0), (CONV_PAD, CONV_PAD)), dimension_numbers=dn,
                                      feature_group_count=half)
        yv = lax.conv_general_dilated(z4[..., half:], w[:, half:].reshape(CONV_K, 1, 1, ch - half), (1, 1),
                                      ((CONV_PAD, CONV_PAD), (0, 0)), dimension_numbers=dn,
                                      feature_group_count=ch - half)
        y = jnp.concatenate([yh, yv], -1).reshape(bsz, length, ch)
    return y + bias


def _conv_branch(v, g, p, rows):
    z = v * jax.nn.sigmoid(g)
    z = _dwconv(z, p['conv_w'], p['conv_b'], rows)
    z = _layernorm(z, p['conv_ln_g'], p['conv_ln_b'])
    return jax.nn.silu(z)


def _mixer(h, p, h0, rows):
    proj = h @ p['w_in']
    u, v, g = jnp.split(proj, [D_SSM, D_SSM + D_CONV], axis=-1)
    states = _s5_scans(u, p, h0)
    y_ssm = _s5_readout(u, states, p)
    y_conv = _conv_branch(v, g, p, rows)
    y = jnp.concatenate([_rmsnorm(y_ssm, p['mix_norm_g'][:D_SSM]),
                         _rmsnorm(y_conv, p['mix_norm_g'][D_SSM:])], -1)
    return y @ p['w_out'], _finals(states)


def _context_states(h, p):
    u = h @ p['w_in'][:, :D_SSM]
    return _finals(_s5_scans(u, p, None))


def _swiglu(t, w_gate, w_up, w_down):
    return (jax.nn.silu(t @ w_gate) * (t @ w_up)) @ w_down


def _route(t, w_router, router_bias):
    n_tok = t.shape[0]
    scores = jax.nn.sigmoid((t @ w_router).astype(F32))
    biased = scores + router_bias.astype(F32)
    grp = biased.reshape(n_tok, N_EXPERT_GROUPS, N_EXPERTS // N_EXPERT_GROUPS)
    grp_score = jnp.sum(lax.top_k(grp, 2)[0], -1)
    _, top_groups = lax.top_k(grp_score, TOP_K_GROUPS)
    group_mask = jnp.sum(jax.nn.one_hot(top_groups, N_EXPERT_GROUPS, dtype=F32), 1)
    expert_mask = jnp.repeat(group_mask, N_EXPERTS // N_EXPERT_GROUPS, axis=1)
    masked = jnp.where(expert_mask > 0, biased, -jnp.inf)
    _, idx = lax.top_k(masked, TOP_K)
    w = jnp.take_along_axis(scores, idx, 1)
    w = w / jnp.sum(w, -1, keepdims=True) * ROUTED_SCALE
    return jnp.sum(jax.nn.one_hot(idx, N_EXPERTS, dtype=F32) * w[..., None], 1)


def _moe(t, p):
    gates = _route(t, p['w_router'], p['router_bias']).astype(t.dtype)

    def expert_step(acc, xs):
        wg, wu, wd, gate = xs
        return acc + gate[:, None] * _swiglu(t, wg, wu, wd), None

    routed, _ = lax.scan(expert_step, jnp.zeros_like(t),
                         (p['we_gate'], p['we_up'], p['we_down'], gates.T))
    return routed + _swiglu(t, p['ws_gate'], p['ws_up'], p['ws_down'])


def setup_inputs(seed: int = 0) -> dict:
    key = jax.random.key(seed)
    keys = iter(jax.random.split(key, 48))

    def normal(shape, scale):
        return jax.random.normal(next(keys), shape, F32) * scale

    L = DEPTH
    d, e, f = D_MODEL, N_EXPERTS, D_EXPERT
    return {
        'x': normal((BATCH, SEQ, d), 1.0),
        'c': normal((BATCH, d), 1.0),
        'ctx': normal((BATCH, CTX_LEN, d), 1.0),
        'c_ctx': normal((d,), 1.0),
        'w_ada': normal((L, d, 6 * d), d ** -0.5),
        'b_ada': normal((L, 6 * d), 0.02),
        'w_in': normal((L, d, D_SSM + 2 * D_CONV), d ** -0.5),
        'ssm_a_re': -0.5 + normal((L, 2, SSM_G, SSM_P), 0.01),
        'ssm_a_im': math.pi * jnp.arange(SSM_P, dtype=F32) + normal((L, 2, SSM_G, SSM_P), 0.01),
        'ssm_log_dt': jax.random.uniform(next(keys), (L, 2, SSM_G), F32,
                                         math.log(DT_MIN), math.log(DT_MAX)),
        'ssm_b_re': normal((L, 2, SSM_G, SSM_P, SSM_H), (2 * SSM_H) ** -0.5),
        'ssm_b_im': normal((L, 2, SSM_G, SSM_P, SSM_H), (2 * SSM_H) ** -0.5),
        'ssm_c_re': normal((L, 2, SSM_G, SSM_H, SSM_P), (2 * SSM_P) ** -0.5),
        'ssm_c_im': normal((L, 2, SSM_G, SSM_H, SSM_P), (2 * SSM_P) ** -0.5),
        'ssm_d': normal((L, SSM_G, SSM_H), 0.5),
        'w_glu': normal((L, D_SSM, D_SSM), D_SSM ** -0.5),
        'b_glu': normal((L, D_SSM), 0.02),
        'conv_w': normal((L, CONV_K, D_CONV), CONV_K ** -0.5),
        'conv_b': normal((L, D_CONV), 0.02),
        'conv_ln_g': 1.0 + normal((L, D_CONV), 0.02),
        'conv_ln_b': normal((L, D_CONV), 0.02),
        'mix_norm_g': 1.0 + normal((L, D_MIX), 0.02),
        'w_out': normal((L, D_MIX, d), D_MIX ** -0.5 * DEEPNORM_BETA),
        'ln1_g': 1.0 + normal((L, d), 0.02),
        'ln1_b': normal((L, d), 0.02),
        'w_router': normal((L, d, e), d ** -0.5),
        'router_bias': normal((L, e), 0.01),
        'we_gate': normal((L, e, d, f), d ** -0.5),
        'we_up': normal((L, e, d, f), d ** -0.5),
        'we_down': normal((L, e, f, d), f ** -0.5 * DEEPNORM_BETA),
        'ws_gate': normal((L, d, f), d ** -0.5),
        'ws_up': normal((L, d, f), d ** -0.5),
        'ws_down': normal((L, f, d), f ** -0.5 * DEEPNORM_BETA),
        'ln2_g': 1.0 + normal((L, d), 0.02),
        'ln2_b': normal((L, d), 0.02),
    }


def reference(x, c, ctx, c_ctx, w_ada, b_ada, w_in, ssm_a_re, ssm_a_im, ssm_log_dt,
              ssm_b_re, ssm_b_im, ssm_c_re, ssm_c_im, ssm_d, w_glu, b_glu, conv_w, conv_b,
              conv_ln_g, conv_ln_b, mix_norm_g, w_out, ln1_g, ln1_b, w_router, router_bias,
              we_gate, we_up, we_down, ws_gate, ws_up, ws_down, ln2_g, ln2_b):
    rows = x.shape[1] // GRID_W
    xl, xc = x, ctx
    for l in range(DEPTH):
        last = l == DEPTH - 1
        p = dict(w_in=w_in[l], a_re=ssm_a_re[l], a_im=ssm_a_im[l], log_dt=ssm_log_dt[l],
                 b_re=ssm_b_re[l], b_im=ssm_b_im[l], c_re=ssm_c_re[l], c_im=ssm_c_im[l],
                 d_skip=ssm_d[l], w_glu=w_glu[l], b_glu=b_glu[l], conv_w=conv_w[l],
                 conv_b=conv_b[l], conv_ln_g=conv_ln_g[l], conv_ln_b=conv_ln_b[l],
                 mix_norm_g=mix_norm_g[l], w_out=w_out[l], w_router=w_router[l],
                 router_bias=router_bias[l], we_gate=we_gate[l], we_up=we_up[l],
                 we_down=we_down[l], ws_gate=ws_gate[l], ws_up=ws_up[l], ws_down=ws_down[l])
        mod_l = (jax.nn.silu(c) @ w_ada[l] + b_ada[l])[:, None, :]
        sh1, sc1, g1, sh2, sc2, g2 = jnp.split(mod_l, 6, axis=-1)
        mod_c = jax.nn.silu(c_ctx) @ w_ada[l] + b_ada[l]
        csh1, csc1, cg1, csh2, csc2, cg2 = jnp.split(mod_c, 6, axis=-1)

        hc = _modulate(xc, csh1, csc1)
        if last:
            h_ctx = _context_states(hc, p)
        else:
            yc, h_ctx = _mixer(hc, p, None, None)
            xc = _layernorm(DEEPNORM_ALPHA * xc + cg1 * yc, ln1_g[l], ln1_b[l])
        yl, _ = _mixer(_modulate(xl, sh1, sc1), p, h_ctx, rows)
        xl = _layernorm(DEEPNORM_ALPHA * xl + g1 * yl, ln1_g[l], ln1_b[l])

        hl2 = _modulate(xl, sh2, sc2).reshape(-1, D_MODEL)
        if last:
            fl = _moe(hl2, p).reshape(xl.shape)
        else:
            hc2 = _modulate(xc, csh2, csc2).reshape(-1, D_MODEL)
            n_ctx = hc2.shape[0]
            fo = _moe(jnp.concatenate([hc2, hl2], 0), p)
            fc = fo[:n_ctx].reshape(xc.shape)
            fl = fo[n_ctx:].reshape(xl.shape)
            xc = _layernorm(DEEPNORM_ALPHA * xc + cg2 * fc, ln2_g[l], ln2_b[l])
        xl = _layernorm(DEEPNORM_ALPHA * xl + g2 * fl, ln2_g[l], ln2_b[l])
    return xl
```

```python
import functools
import math

import jax
import jax.numpy as jnp
from jax import lax
from jax.experimental import pallas as pl
from jax.experimental.pallas import tpu as pltpu

F32 = jnp.float32
BF16 = jnp.bfloat16

D_MODEL = 1024
DEPTH = 4
GRID_W = 64
D_SSM = 512
D_CONV = 512
SSM_H = 16
SSM_G = 32
SSM_P = 64
CONV_K = 31
CONV_PAD = 15
CONV_WIN = GRID_W + 32
N_EXPERTS = 64
TOP_K = 8
N_GROUPS = 8
TOP_K_GROUPS = 4
GROUP_SIZE = N_EXPERTS // N_GROUPS
D_EXPERT = 256
ROUTED_SCALE = 2.5
ALPHA = (2 * DEPTH) ** 0.25
LN_EPS = 1e-5

CHUNK = 16
CW = CHUNK * SSM_H
MOD_ROWS = 32
TM = 256
TME = 256
VMEM_LIMIT = 48 * 1024 * 1024


def _cp(*sem):
    return pltpu.CompilerParams(dimension_semantics=sem, vmem_limit_bytes=VMEM_LIMIT)


def _ln(x):
    mu = jnp.mean(x, axis=-1, keepdims=True)
    xc = x - mu
    var = jnp.mean(xc * xc, axis=-1, keepdims=True)
    return xc * lax.rsqrt(var + LN_EPS)


def _silu(x):
    return x * jax.nn.sigmoid(x)


def _ada_kernel(c_ref, w_ref, b_ref, o_ref):
    a = _silu(c_ref[...]).astype(BF16)
    o_ref[...] = jnp.dot(a, w_ref[...].astype(BF16), preferred_element_type=F32) + b_ref[...]


def _ada(cc, w_ada, b_ada):
    depth, d, n = w_ada.shape
    tn = 1536
    return pl.pallas_call(
        _ada_kernel,
        grid=(depth, n // tn),
        in_specs=[pl.BlockSpec((MOD_ROWS, d), lambda l, j: (0, 0)),
                  pl.BlockSpec((None, d, tn), lambda l, j: (l, 0, j)),
                  pl.BlockSpec((None, 1, tn), lambda l, j: (l, 0, j))],
        out_specs=pl.BlockSpec((None, MOD_ROWS, tn), lambda l, j: (l, 0, j)),
        out_shape=jax.ShapeDtypeStruct((depth, MOD_ROWS, n), F32),
        compiler_params=_cp("arbitrary", "arbitrary"),
        name="ada",
    )(cc, w_ada, b_ada.reshape(depth, 1, n))


def _inproj_kernel(x_ref, mod_ref, w_ref, u_ref, z_ref):
    m = mod_ref[...]
    h = _ln(x_ref[...]) * (1.0 + m[1:2]) + m[0:1]
    p = jnp.dot(h.astype(BF16), w_ref[...], preferred_element_type=F32)
    u_ref[...] = p[:, :D_SSM].astype(BF16)
    v = p[:, D_SSM:D_SSM + D_CONV]
    g = p[:, D_SSM + D_CONV:]
    z_ref[...] = (v * jax.nn.sigmoid(g)).astype(BF16)


def _mod_row(i, t_ctx, seq):
    start = i * TM
    return jnp.where(start < t_ctx, 0, 1 + (start - t_ctx) // seq)


def _inproj(xs, mod, w, t_ctx, seq):
    t, d = xs.shape
    return pl.pallas_call(
        _inproj_kernel,
        grid=(t // TM,),
        in_specs=[pl.BlockSpec((TM, d), lambda i: (i, 0)),
                  pl.BlockSpec((None, 6, d), lambda i: (_mod_row(i, t_ctx, seq), 0, 0)),
                  pl.BlockSpec(w.shape, lambda i: (0, 0))],
        out_specs=[pl.BlockSpec((TM, D_SSM), lambda i: (i, 0)),
                   pl.BlockSpec((TM, D_CONV), lambda i: (i, 0))],
        out_shape=[jax.ShapeDtypeStruct((t, D_SSM), BF16),
                   jax.ShapeDtypeStruct((t, D_CONV), BF16)],
        compiler_params=_cp("arbitrary"),
        name="inproj",
    )(xs, mod, w)


def _s5_mats(a_re, a_im, log_dt, b_re, b_im, c_re, c_im, d_skip):
    hp = lax.Precision.HIGHEST
    q = CHUNK
    dt = jnp.exp(log_dt)[..., None]
    ldr, ldi = a_re * dt, a_im * dt
    ks = jnp.arange(q + 1, dtype=F32)[:, None, None, None]
    mag = jnp.exp(ks * ldr)
    pwr, pwi = mag * jnp.cos(ks * ldi), mag * jnp.sin(ks * ldi)
    nr, ni = pwr[1] - 1.0, pwi[1]
    den = a_re * a_re + a_im * a_im
    qr, qi = (nr * a_re + ni * a_im) / den, (ni * a_re - nr * a_im) / den
    bbr = qr[..., None] * b_re - qi[..., None] * b_im
    bbi = qr[..., None] * b_im + qi[..., None] * b_re
    mr = c_re[None] * pwr[:, :, :, None, :] - c_im[None] * pwi[:, :, :, None, :]
    mi = c_re[None] * pwi[:, :, :, None, :] + c_im[None] * pwr[:, :, :, None, :]
    kern = (jnp.einsum('kdgop,dgph->kdgoh', mr, bbr, precision=hp)
            - jnp.einsum('kdgop,dgph->kdgoh', mi, bbi, precision=hp))
    s_i = jnp.arange(q)[:, None]
    t_i = jnp.arange(q)[None, :]
    lag = t_i - s_i
    tf = jnp.where((lag >= 0)[:, :, None, None, None], kern[:q, 0][jnp.clip(lag, 0, q - 1)], 0.0)
    tb = jnp.where((lag <= 0)[:, :, None, None, None], kern[:q, 1][jnp.clip(-lag, 0, q - 1)], 0.0)
    tm = (tf + tb).transpose(2, 0, 4, 1, 3).reshape(SSM_G, CW, CW)

    def state_in(pr, pi, d):
        re = pr[..., None] * bbr[d][None] - pi[..., None] * bbi[d][None]
        im = pr[..., None] * bbi[d][None] + pi[..., None] * bbr[d][None]
        fix = lambda a: a.transpose(1, 0, 3, 2).reshape(SSM_G, CW, SSM_P)
        return fix(re), fix(im)

    f_re, f_im = state_in(pwr[:q, 0][::-1], pwi[:q, 0][::-1], 0)
    g_re, g_im = state_in(pwr[:q, 1], pwi[:q, 1], 1)
    rm = jnp.concatenate([f_re, g_re, f_im, g_im], axis=-1)

    fix_o = lambda a: a.transpose(1, 3, 0, 2).reshape(SSM_G, SSM_P, CW)
    om = jnp.concatenate([fix_o(mr[1:, 0]), fix_o(mr[1:, 1][::-1]),
                          fix_o(-mi[1:, 0]), fix_o(-mi[1:, 1][::-1])], axis=1)
    lq = jnp.stack([jnp.concatenate([pwr[q, 0], pwr[q, 1]], -1),
                    jnp.concatenate([pwi[q, 0], pwi[q, 1]], -1)], axis=1)
    dvec = jnp.tile(d_skip[:, None, :], (1, q, 1)).reshape(SSM_G, 1, CW)
    return tm.astype(BF16), rm.astype(BF16), om.astype(BF16), lq, dvec


def _scan_kernel(x_ref, t_ref, r_ref, o_ref, lq_ref, d_ref, h0_ref, y_ref, fin_ref,
                 rr_ref, st_ref, *, n_chunks, bsz):
    p2 = 2 * SSM_P
    x = x_ref[...]
    rr_ref[...] = jnp.dot(x, r_ref[...], preferred_element_type=F32)
    lq = lq_ref[...]
    lre, lim = lq[0:1], lq[1:2]
    is_f = lax.broadcasted_iota(jnp.int32, (bsz, p2), 1) < SSM_P
    h0 = h0_ref[...]

    def step(j, carry):
        sre, sim = carry
        rf = pl.multiple_of(j * bsz, bsz)
        rb = pl.multiple_of((n_chunks - 1 - j) * bsz, bsz)
        st_ref[pl.ds(rf, bsz), 0:SSM_P] = sre[:, 0:SSM_P]
        st_ref[pl.ds(rb, bsz), SSM_P:p2] = sre[:, SSM_P:p2]
        st_ref[pl.ds(rf, bsz), p2:p2 + SSM_P] = sim[:, 0:SSM_P]
        st_ref[pl.ds(rb, bsz), p2 + SSM_P:2 * p2] = sim[:, SSM_P:p2]
        r_re = jnp.where(is_f, rr_ref[pl.ds(rf, bsz), 0:p2], rr_ref[pl.ds(rb, bsz), 0:p2])
        r_im = jnp.where(is_f, rr_ref[pl.ds(rf, bsz), p2:2 * p2], rr_ref[pl.ds(rb, bsz), p2:2 * p2])
        return lre * sre - lim * sim + r_re, lre * sim + lim * sre + r_im

    sre, sim = lax.fori_loop(0, n_chunks, step, (h0[:, 0:p2], h0[:, p2:2 * p2]))
    fin_ref[...] = jnp.concatenate([sre, sim], axis=1)
    y = jnp.dot(x, t_ref[...], preferred_element_type=F32)
    y = y + jnp.dot(st_ref[...].astype(BF16), o_ref[...], preferred_element_type=F32)
    y = y + d_ref[...] * x.astype(F32)
    y_ref[...] = jax.nn.gelu(y, approximate=True).astype(BF16)


def _scan(xg, tm, rm, om, lq, dvec, h0, bsz):
    g, n, _ = xg.shape
    n_chunks = n // bsz
    mat = pl.BlockSpec((None, CW, CW), lambda i: (i, 0, 0))
    return pl.pallas_call(
        functools.partial(_scan_kernel, n_chunks=n_chunks, bsz=bsz),
        grid=(g,),
        in_specs=[pl.BlockSpec((None, n, CW), lambda i: (i, 0, 0)), mat, mat, mat,
                  pl.BlockSpec((None, 2, 2 * SSM_P), lambda i: (i, 0, 0)),
                  pl.BlockSpec((None, 1, CW), lambda i: (i, 0, 0)),
                  pl.BlockSpec((None, bsz, 4 * SSM_P), lambda i: (i, 0, 0))],
        out_specs=[pl.BlockSpec((None, n, CW), lambda i: (i, 0, 0)),
                   pl.BlockSpec((None, bsz, 4 * SSM_P), lambda i: (i, 0, 0))],
        out_shape=[jax.ShapeDtypeStruct((g, n, CW), BF16),
                   jax.ShapeDtypeStruct((g, bsz, 4 * SSM_P), F32)],
        scratch_shapes=[pltpu.VMEM((n, 4 * SSM_P), F32), pltpu.VMEM((n, 4 * SSM_P), F32)],
        compiler_params=_cp("arbitrary"),
        name="s5_scan",
    )(xg, tm, rm, om, lq, dvec, h0)


def _to_chunks(u, bsz, length):
    nc = length // CHUNK
    return (u.reshape(bsz, nc, CHUNK, SSM_G, SSM_H).transpose(3, 1, 0, 2, 4)
            .reshape(SSM_G, nc * bsz, CW))


def _from_chunks(y, bsz, length):
    nc = length // CHUNK
    return (y.reshape(SSM_G, nc, bsz, CHUNK, SSM_H).transpose(2, 1, 3, 0, 4)
            .reshape(bsz * length, D_SSM))


def _conv_post(cv, p_ref):
    p = p_ref[...]
    cv = cv + p[0:1]
    y = _silu(_ln(cv) * p[1:2] + p[2:3])
    return y * lax.rsqrt(jnp.mean(y * y, axis=-1, keepdims=True) + LN_EPS) * p[3:4]


def _conv_seq(buf_ref, base, w_ref, c0):
    half = D_CONV // 2
    win = buf_ref[pl.ds(base, CONV_WIN), c0:c0 + half]
    acc = jnp.zeros((GRID_W, half), F32)
    for ph in range(8):
        wb = win if ph == 0 else pltpu.roll(win, CONV_WIN - ph, axis=0)
        for a in range(4):
            j = 8 * a + ph - 1
            if 0 <= j < CONV_K:
                acc = acc + w_ref[j:j + 1, c0:c0 + half] * wb[8 * a:8 * a + GRID_W]
    return acc


def _conv_lat_kernel(z_ref, w_ref, p_ref, o_ref, hb_ref, vb_ref, *, rows):
    half = D_CONV // 2
    hstride = GRID_W + 16
    hb_ref[...] = jnp.zeros(hb_ref.shape, F32)
    vb_ref[pl.ds(0, CONV_PAD * GRID_W), :] = jnp.zeros((CONV_PAD * GRID_W, half), F32)
    vb_ref[pl.ds((CONV_PAD + rows) * GRID_W, CONV_PAD * GRID_W), :] = jnp.zeros((CONV_PAD * GRID_W, half), F32)
    for r in range(rows):
        hb_ref[pl.ds(16 + r * hstride, GRID_W), :] = z_ref[pl.ds(r * GRID_W, GRID_W), 0:half].astype(F32)
    vb_ref[pl.ds(CONV_PAD * GRID_W, rows * GRID_W), :] = z_ref[:, half:D_CONV].astype(F32)

    def row(r, carry):
        hbase = pl.multiple_of(r * hstride, 16)
        vbase = pl.multiple_of(r * GRID_W, GRID_W)
        acc_h = _conv_seq(hb_ref, hbase, w_ref, 0)
        acc_v = jnp.zeros((GRID_W, half), F32)
        for j in range(CONV_K):
            acc_v = acc_v + w_ref[j:j + 1, half:D_CONV] * vb_ref[pl.ds(vbase + j * GRID_W, GRID_W), :]
        out = _conv_post(jnp.concatenate([acc_h, acc_v], axis=1), p_ref)
        o_ref[pl.ds(vbase, GRID_W), :] = out.astype(BF16)
        return carry

    lax.fori_loop(0, rows, row, 0)


def _conv_lat(z, w, p, bsz, seq, blk0):
    rows = seq // GRID_W
    half = D_CONV // 2
    return pl.pallas_call(
        functools.partial(_conv_lat_kernel, rows=rows),
        grid=(bsz,),
        in_specs=[pl.BlockSpec((seq, D_CONV), lambda b: (blk0 + b, 0)),
                  pl.BlockSpec(w.shape, lambda b: (0, 0)),
                  pl.BlockSpec(p.shape, lambda b: (0, 0))],
        out_specs=pl.BlockSpec((seq, D_CONV), lambda b: (b, 0)),
        out_shape=jax.ShapeDtypeStruct((bsz * seq, D_CONV), BF16),
        scratch_shapes=[pltpu.VMEM((rows * (GRID_W + 16) + 16, half), F32),
                        pltpu.VMEM(((rows + 2 * CONV_PAD) * GRID_W, half), F32)],
        compiler_params=_cp("arbitrary"),
        name="conv_latent",
    )(z, w, p)


def _conv_ctx_kernel(z_ref, w_ref, p_ref, o_ref, cb_ref, *, clen):
    cb_ref[pl.ds(0, 16), :] = jnp.zeros((16, D_CONV), F32)
    cb_ref[pl.ds(16 + clen, 16), :] = jnp.zeros((16, D_CONV), F32)
    cb_ref[pl.ds(16, clen), :] = z_ref[...].astype(F32)
    for blk in range(clen // GRID_W):
        base = blk * GRID_W
        acc = jnp.concatenate([_conv_seq(cb_ref, base, w_ref, 0),
                               _conv_seq(cb_ref, base, w_ref, D_CONV // 2)], axis=1)
        o_ref[pl.ds(base, GRID_W), :] = _conv_post(acc, p_ref).astype(BF16)


def _conv_ctx(z, w, p, bsz, clen):
    return pl.pallas_call(
        functools.partial(_conv_ctx_kernel, clen=clen),
        grid=(bsz,),
        in_specs=[pl.BlockSpec((clen, D_CONV), lambda b: (b, 0)),
                  pl.BlockSpec(w.shape, lambda b: (0, 0)),
                  pl.BlockSpec(p.shape, lambda b: (0, 0))],
        out_specs=pl.BlockSpec((clen, D_CONV), lambda b: (b, 0)),
        out_shape=jax.ShapeDtypeStruct((bsz * clen, D_CONV), BF16),
        scratch_shapes=[pltpu.VMEM((clen + 32, D_CONV), F32)],
        compiler_params=_cp("arbitrary"),
        name="conv_context",
    )(z, w, p)


def _outproj_kernel(ys_ref, yc_ref, x_ref, mod_ref, wglu_ref, pv_ref, wo_ref, ln_ref, o_ref):
    ys = ys_ref[...]
    pv = pv_ref[...]
    gl = jnp.dot(ys, wglu_ref[...], preferred_element_type=F32) + pv[0:1]
    yg = ys.astype(F32) * jax.nn.sigmoid(gl)
    yn = yg * lax.rsqrt(jnp.mean(yg * yg, axis=-1, keepdims=True) + LN_EPS) * pv[1:2]
    y = jnp.dot(yn.astype(BF16), wo_ref[0:D_SSM, :], preferred_element_type=F32)
    y = y + jnp.dot(yc_ref[...], wo_ref[D_SSM:, :], preferred_element_type=F32)
    m = mod_ref[...]
    ln = ln_ref[...]
    o_ref[...] = _ln(ALPHA * x_ref[...] + m[2:3] * y) * ln[0:1] + ln[1:2]


def _outproj(ys, yc, xs, mod, wglu, pv, wo, ln, t_ctx, seq):
    t, d = xs.shape
    full = lambda a: pl.BlockSpec(a.shape, lambda i: (0, 0))
    return pl.pallas_call(
        _outproj_kernel,
        grid=(t // TM,),
        in_specs=[pl.BlockSpec((TM, D_SSM), lambda i: (i, 0)),
                  pl.BlockSpec((TM, D_CONV), lambda i: (i, 0)),
                  pl.BlockSpec((TM, d), lambda i: (i, 0)),
                  pl.BlockSpec((None, 6, d), lambda i: (_mod_row(i, t_ctx, seq), 0, 0)),
                  full(wglu), full(pv), full(wo), full(ln)],
        out_specs=pl.BlockSpec((TM, d), lambda i: (i, 0)),
        out_shape=jax.ShapeDtypeStruct((t, d), F32),
        compiler_params=_cp("arbitrary"),
        name="outproj",
    )(ys, yc, xs, mod, wglu, pv, wo, ln)


def _router_kernel(x_ref, mod_ref, wr_ref, rb_ref, tri_ref, h_ref, idx_ref, wt_ref, rank_ref,
                   cnt_ref, carry_ref):
    @pl.when(pl.program_id(0) == 0)
    def _():
        carry_ref[...] = jnp.zeros(carry_ref.shape, F32)

    m = mod_ref[...]
    h = _ln(x_ref[...]) * (1.0 + m[4:5]) + m[3:4]
    h_ref[...] = h.astype(BF16)
    logits = lax.dot_general(wr_ref[...], h, (((1,), (1,)), ((), ())),
                             precision=lax.Precision.HIGHEST, preferred_element_type=F32)
    s = jax.nn.sigmoid(logits)
    biased = s + rb_ref[...]
    ninf = -jnp.inf
    tm = s.shape[1]

    row8 = lax.broadcasted_iota(jnp.int32, (GROUP_SIZE, tm), 0).astype(F32)
    gs = []
    for q in range(N_GROUPS):
        v = biased[q * GROUP_SIZE:(q + 1) * GROUP_SIZE]
        m1 = jnp.max(v, axis=0, keepdims=True)
        i1 = jnp.min(jnp.where(v == m1, row8, float(GROUP_SIZE)), axis=0, keepdims=True)
        m2 = jnp.max(jnp.where(row8 == i1, ninf, v), axis=0, keepdims=True)
        gs.append(m1 + m2)
    gsc = jnp.concatenate(gs, axis=0)
    rowg = lax.broadcasted_iota(jnp.int32, (N_GROUPS, tm), 0).astype(F32)
    gsel = jnp.zeros((N_GROUPS, tm), F32)
    for _ in range(TOP_K_GROUPS):
        mx = jnp.max(gsc, axis=0, keepdims=True)
        ii = jnp.min(jnp.where(gsc == mx, rowg, float(N_GROUPS)), axis=0, keepdims=True)
        hit = rowg == ii
        gsel = jnp.where(hit, 1.0, gsel)
        gsc = jnp.where(hit, ninf, gsc)
    emask = jnp.concatenate([jnp.broadcast_to(gsel[q:q + 1], (GROUP_SIZE, tm))
                             for q in range(N_GROUPS)], axis=0)
    masked = jnp.where(emask > 0.0, biased, ninf)

    rowe = lax.broadcasted_iota(jnp.int32, (N_EXPERTS, tm), 0).astype(F32)
    selm = jnp.zeros((N_EXPERTS, tm), F32)
    idxs, ws = [], []
    for _ in range(TOP_K):
        mx = jnp.max(masked, axis=0, keepdims=True)
        ii = jnp.min(jnp.where(masked == mx, rowe, float(N_EXPERTS)), axis=0, keepdims=True)
        hit = rowe == ii
        idxs.append(ii)
        ws.append(jnp.sum(jnp.where(hit, s, 0.0), axis=0, keepdims=True))
        selm = jnp.where(hit, 1.0, selm)
        masked = jnp.where(hit, ninf, masked)
    wsum = ws[0]
    for k in range(1, TOP_K):
        wsum = wsum + ws[k]
    idx_ref[...] = jnp.concatenate(idxs, axis=0).astype(jnp.int32)
    wt_ref[...] = jnp.concatenate([w / wsum * ROUTED_SCALE for w in ws], axis=0)

    cnt = jnp.dot(selm.astype(BF16), tri_ref[...], preferred_element_type=F32)
    carry = carry_ref[...]
    rank_excl = cnt - selm + carry
    rank_ref[...] = jnp.concatenate(
        [jnp.sum(jnp.where(rowe == idxs[k], rank_excl, 0.0), axis=0, keepdims=True)
         for k in range(TOP_K)], axis=0).astype(jnp.int32)
    new_carry = carry + cnt[:, tm - 1:tm]
    carry_ref[...] = new_carry
    cnt_ref[...] = new_carry


def _router(xs, mod, wr_t, rb, t_ctx, seq):
    t, d = xs.shape
    tri = (jnp.arange(TM)[:, None] <= jnp.arange(TM)[None, :]).astype(BF16)
    full = lambda a: pl.BlockSpec(a.shape, lambda i: (0, 0))
    return pl.pallas_call(
        _router_kernel,
        grid=(t // TM,),
        in_specs=[pl.BlockSpec((TM, d), lambda i: (i, 0)),
                  pl.BlockSpec((None, 6, d), lambda i: (_mod_row(i, t_ctx, seq), 0, 0)),
                  full(wr_t), full(rb), full(tri)],
        out_specs=[pl.BlockSpec((TM, d), lambda i: (i, 0)),
                   pl.BlockSpec((TOP_K, TM), lambda i: (0, i)),
                   pl.BlockSpec((TOP_K, TM), lambda i: (0, i)),
                   pl.BlockSpec((TOP_K, TM), lambda i: (0, i)),
                   pl.BlockSpec((N_EXPERTS, 1), lambda i: (0, 0))],
        out_shape=[jax.ShapeDtypeStruct((t, d), BF16),
                   jax.ShapeDtypeStruct((TOP_K, t), jnp.int32),
                   jax.ShapeDtypeStruct((TOP_K, t), F32),
                   jax.ShapeDtypeStruct((TOP_K, t), jnp.int32),
                   jax.ShapeDtypeStruct((N_EXPERTS, 1), F32)],
        scratch_shapes=[pltpu.VMEM((N_EXPERTS, 1), F32)],
        compiler_params=_cp("arbitrary"),
        name="router",
    )(xs, mod, wr_t, rb, tri)


def _experts_kernel(te_ref, tv_ref, x_ref, wg_ref, wu_ref, wd_ref, o_ref):
    i = pl.program_id(0)

    @pl.when(tv_ref[i] > 0)
    def _():
        x = x_ref[...]
        a = jnp.dot(x, wg_ref[...], preferred_element_type=F32)
        b = jnp.dot(x, wu_ref[...], preferred_element_type=F32)
        hid = (_silu(a) * b).astype(BF16)
        o_ref[...] = jnp.dot(hid, wd_ref[...], preferred_element_type=F32).astype(BF16)

    @pl.when(tv_ref[i] == 0)
    def _():
        o_ref[...] = jnp.zeros(o_ref.shape, BF16)


def _experts(tile_expert, tile_valid, xs, wg, wu, wd):
    n, d = xs.shape
    f = wg.shape[-1]
    grid_spec = pltpu.PrefetchScalarGridSpec(
        num_scalar_prefetch=2,
        grid=(n // TME,),
        in_specs=[pl.BlockSpec((TME, d), lambda i, te, tv: (i, 0)),
                  pl.BlockSpec((None, d, f), lambda i, te, tv: (te[i], 0, 0)),
                  pl.BlockSpec((None, d, f), lambda i, te, tv: (te[i], 0, 0)),
                  pl.BlockSpec((None, f, d), lambda i, te, tv: (te[i], 0, 0))],
        out_specs=pl.BlockSpec((TME, d), lambda i, te, tv: (i, 0)),
    )
    return pl.pallas_call(
        _experts_kernel,
        grid_spec=grid_spec,
        out_shape=jax.ShapeDtypeStruct((n, d), BF16),
        compiler_params=_cp("arbitrary"),
        name="experts",
    )(tile_expert, tile_valid, xs, wg, wu, wd)


def _combine_kernel(yg_ref, wt_ref, h_ref, x_ref, mod_ref, wsg_ref, wsu_ref, wsd_ref, ln_ref, o_ref):
    d = x_ref.shape[1]
    wt = wt_ref[...]
    acc = wt[:, 0:1] * yg_ref[:, 0:d].astype(F32)
    for k in range(1, TOP_K):
        acc = acc + wt[:, k:k + 1] * yg_ref[:, k * d:(k + 1) * d].astype(F32)
    h = h_ref[...]
    a = jnp.dot(h, wsg_ref[...], preferred_element_type=F32)
    b = jnp.dot(h, wsu_ref[...], preferred_element_type=F32)
    acc = acc + jnp.dot((_silu(a) * b).astype(BF16), wsd_ref[...], preferred_element_type=F32)
    m = mod_ref[...]
    ln = ln_ref[...]
    o_ref[...] = _ln(ALPHA * x_ref[...] + m[5:6] * acc) * ln[0:1] + ln[1:2]


def _combine(yg, wt, h2, xs, mod, wsg, wsu, wsd, ln, t_ctx, seq):
    t, d = xs.shape
    full = lambda a: pl.BlockSpec(a.shape, lambda i: (0, 0))
    return pl.pallas_call(
        _combine_kernel,
        grid=(t // TM,),
        in_specs=[pl.BlockSpec((TM, TOP_K * d), lambda i: (i, 0)),
                  pl.BlockSpec((TM, TOP_K), lambda i: (i, 0)),
                  pl.BlockSpec((TM, d), lambda i: (i, 0)),
                  pl.BlockSpec((TM, d), lambda i: (i, 0)),
                  pl.BlockSpec((None, 6, d), lambda i: (_mod_row(i, t_ctx, seq), 0, 0)),
                  full(wsg), full(wsu), full(wsd), full(ln)],
        out_specs=pl.BlockSpec((TM, d), lambda i: (i, 0)),
        out_shape=jax.ShapeDtypeStruct((t, d), F32),
        compiler_params=_cp("arbitrary"),
        name="combine",
    )(yg, wt, h2, xs, mod, wsg, wsu, wsd, ln)


def _moe(xs, mod, wr_t, rb, wg, wu, wd, wsg, wsu, wsd, ln, t_ctx, seq):
    t, d = xs.shape
    h2, idx, wt, rank, cnt = _router(xs, mod, wr_t, rb, t_ctx, seq)
    n_pad = t * TOP_K + N_EXPERTS * TME
    n_tiles = n_pad // TME
    counts = cnt[:, 0].astype(jnp.int32)
    padded = ((counts + TME - 1) // TME) * TME
    ends = jnp.cumsum(padded)
    starts = ends - padded
    dest = starts[idx] + rank
    tile_start = jnp.arange(n_tiles, dtype=jnp.int32) * TME
    tile_expert = jnp.minimum(jnp.searchsorted(ends, tile_start, side='right'),
                              N_EXPERTS - 1).astype(jnp.int32)
    tile_valid = (tile_start < ends[-1]).astype(jnp.int32)
    tok = jnp.broadcast_to(jnp.arange(t, dtype=jnp.int32)[None, :], (TOP_K, t))
    src = jnp.zeros((n_pad,), jnp.int32).at[dest.reshape(-1)].set(tok.reshape(-1))
    xe = jnp.take(h2, src, axis=0)
    ye = _experts(tile_expert, tile_valid, xe, wg, wu, wd)
    yg = jnp.take(ye, dest.T, axis=0).reshape(t, TOP_K * d)
    return _combine(yg, wt.T, h2, xs, mod, wsg, wsu, wsd, ln, t_ctx, seq)


def kernel(x, c, ctx, c_ctx, w_ada, b_ada, w_in, ssm_a_re, ssm_a_im, ssm_log_dt, ssm_b_re, ssm_b_im,
           ssm_c_re, ssm_c_im, ssm_d, w_glu, b_glu, conv_w, conv_b, conv_ln_g, conv_ln_b, mix_norm_g,
           w_out, ln1_g, ln1_b, w_router, router_bias, we_gate, we_up, we_down, ws_gate, ws_up,
           ws_down, ln2_g, ln2_b):
    bsz, seq, d = x.shape
    clen = ctx.shape[1]
    depth = w_in.shape[0]
    t_ctx, t_lat = bsz * clen, bsz * seq
    assert d == D_MODEL and bsz + 1 <= MOD_ROWS
    assert seq % GRID_W == 0 and clen % GRID_W == 0 and seq % TM == 0 and t_ctx % seq == 0

    xs = jnp.concatenate([ctx.reshape(t_ctx, d), x.reshape(t_lat, d)], axis=0)
    cc = jnp.zeros((MOD_ROWS, d), F32).at[0].set(c_ctx).at[1:bsz + 1].set(c)
    mod_all = _ada(cc, w_ada, b_ada).reshape(depth, MOD_ROWS, 6, d)

    for l in range(depth):
        mod = mod_all[l]
        u, z = _inproj(xs, mod, w_in[l].astype(BF16), t_ctx, seq)

        tm, rm, om, lq, dvec = _s5_mats(ssm_a_re[l], ssm_a_im[l], ssm_log_dt[l], ssm_b_re[l],
                                         ssm_b_im[l], ssm_c_re[l], ssm_c_im[l], ssm_d[l])
        h0 = jnp.zeros((SSM_G, bsz, 4 * SSM_P), F32)
        yc, fin = _scan(_to_chunks(u[:t_ctx], bsz, clen), tm, rm, om, lq, dvec, h0, bsz)
        yl, _ = _scan(_to_chunks(u[t_ctx:], bsz, seq), tm, rm, om, lq, dvec, fin, bsz)
        ys = jnp.concatenate([_from_chunks(yc, bsz, clen), _from_chunks(yl, bsz, seq)], axis=0)

        cp = jnp.stack([conv_b[l], conv_ln_g[l], conv_ln_b[l], mix_norm_g[l, D_SSM:]])
        zc = jnp.concatenate([_conv_ctx(z, conv_w[l], cp, bsz, clen),
                              _conv_lat(z, conv_w[l], cp, bsz, seq, t_ctx // seq)], axis=0)

        pv = jnp.stack([b_glu[l], mix_norm_g[l, :D_SSM]])
        xs = _outproj(ys, zc, xs, mod, w_glu[l].astype(BF16), pv, w_out[l].astype(BF16),
                      jnp.stack([ln1_g[l], ln1_b[l]]), t_ctx, seq)

        xs = _moe(xs, mod, w_router[l].T, router_bias[l][:, None],
                  we_gate[l].astype(BF16), we_up[l].astype(BF16), we_down[l].astype(BF16),
                  ws_gate[l].astype(BF16), ws_up[l].astype(BF16), ws_down[l].astype(BF16),
                  jnp.stack([ln2_g[l], ln2_b[l]]), t_ctx, seq)

    return xs[t_ctx:].reshape(bsz, seq, d)
```

```python
import functools
import math

import jax
import jax.numpy as jnp
from jax import lax
from jax.experimental import pallas as pl
from jax.experimental.pallas import tpu as pltpu

F32 = jnp.float32
BF16 = jnp.bfloat16

D_MODEL = 1024
DEPTH = 4
GRID_W = 64
D_SSM = 512
D_CONV = 512
SSM_H = 16
SSM_G = 32
SSM_P = 64
CONV_K = 31
CONV_PAD = 15
CONV_WIN = GRID_W + 32
N_EXPERTS = 64
TOP_K = 8
N_GROUPS = 8
TOP_K_GROUPS = 4
GROUP_SIZE = N_EXPERTS // N_GROUPS
D_EXPERT = 256
ROUTED_SCALE = 2.5
ALPHA = (2 * DEPTH) ** 0.25
LN_EPS = 1e-5

CHUNK = 16
CW = CHUNK * SSM_H
MOD_ROWS = 32
TM = 256
TME = 256
KEY_SHIFT = 19
VMEM_LIMIT = 48 * 1024 * 1024


def _cp(*sem):
    return pltpu.CompilerParams(dimension_semantics=sem, vmem_limit_bytes=VMEM_LIMIT)


def _ln(x):
    mu = jnp.mean(x, axis=-1, keepdims=True)
    xc = x - mu
    var = jnp.mean(xc * xc, axis=-1, keepdims=True)
    return xc * lax.rsqrt(var + LN_EPS)


def _silu(x):
    return x * jax.nn.sigmoid(x)


def _ada_kernel(c_ref, w_ref, b_ref, o_ref):
    a = _silu(c_ref[...]).astype(BF16)
    o_ref[...] = jnp.dot(a, w_ref[...].astype(BF16), preferred_element_type=F32) + b_ref[...]


def _ada(cc, w_ada, b_ada):
    depth, d, n = w_ada.shape
    tn = 1536
    return pl.pallas_call(
        _ada_kernel,
        grid=(depth, n // tn),
        in_specs=[pl.BlockSpec((MOD_ROWS, d), lambda l, j: (0, 0)),
                  pl.BlockSpec((None, d, tn), lambda l, j: (l, 0, j)),
                  pl.BlockSpec((None, 1, tn), lambda l, j: (l, 0, j))],
        out_specs=pl.BlockSpec((None, MOD_ROWS, tn), lambda l, j: (l, 0, j)),
        out_shape=jax.ShapeDtypeStruct((depth, MOD_ROWS, n), F32),
        compiler_params=_cp("arbitrary", "arbitrary"),
        name="ada",
    )(cc, w_ada, b_ada.reshape(depth, 1, n))


def _inproj_kernel(x_ref, mod_ref, w_ref, u_ref, z_ref):
    m = mod_ref[...]
    h = _ln(x_ref[...]) * (1.0 + m[1:2]) + m[0:1]
    p = jnp.dot(h.astype(BF16), w_ref[...], preferred_element_type=F32)
    u_ref[...] = p[:, :D_SSM].astype(BF16)
    v = p[:, D_SSM:D_SSM + D_CONV]
    g = p[:, D_SSM + D_CONV:]
    z_ref[...] = (v * jax.nn.sigmoid(g)).astype(BF16)


def _mod_row(i, t_ctx, seq):
    start = i * TM
    return jnp.where(start < t_ctx, 0, 1 + (start - t_ctx) // seq)


def _inproj(xs, mod, w, t_ctx, seq):
    t, d = xs.shape
    return pl.pallas_call(
        _inproj_kernel,
        grid=(t // TM,),
        in_specs=[pl.BlockSpec((TM, d), lambda i: (i, 0)),
                  pl.BlockSpec((None, 6, d), lambda i: (_mod_row(i, t_ctx, seq), 0, 0)),
                  pl.BlockSpec(w.shape, lambda i: (0, 0))],
        out_specs=[pl.BlockSpec((TM, D_SSM), lambda i: (i, 0)),
                   pl.BlockSpec((TM, D_CONV), lambda i: (i, 0))],
        out_shape=[jax.ShapeDtypeStruct((t, D_SSM), BF16),
                   jax.ShapeDtypeStruct((t, D_CONV), BF16)],
        compiler_params=_cp("arbitrary"),
        name="inproj",
    )(xs, mod, w)


def _s5_mats(a_re, a_im, log_dt, b_re, b_im, c_re, c_im, d_skip):
    hp = lax.Precision.HIGHEST
    q = CHUNK
    dt = jnp.exp(log_dt)[..., None]
    ldr, ldi = a_re * dt, a_im * dt
    ks = jnp.arange(q + 1, dtype=F32)[:, None, None, None]
    mag = jnp.exp(ks * ldr)
    pwr, pwi = mag * jnp.cos(ks * ldi), mag * jnp.sin(ks * ldi)
    nr, ni = pwr[1] - 1.0, pwi[1]
    den = a_re * a_re + a_im * a_im
    qr, qi = (nr * a_re + ni * a_im) / den, (ni * a_re - nr * a_im) / den
    bbr = qr[..., None] * b_re - qi[..., None] * b_im
    bbi = qr[..., None] * b_im + qi[..., None] * b_re
    mr = c_re[None] * pwr[:, :, :, None, :] - c_im[None] * pwi[:, :, :, None, :]
    mi = c_re[None] * pwi[:, :, :, None, :] + c_im[None] * pwr[:, :, :, None, :]
    kern = (jnp.einsum('kdgop,dgph->kdgoh', mr, bbr, precision=hp)
            - jnp.einsum('kdgop,dgph->kdgoh', mi, bbi, precision=hp))
    s_i = jnp.arange(q)[:, None]
    t_i = jnp.arange(q)[None, :]
    lag = t_i - s_i
    tf = jnp.where((lag >= 0)[:, :, None, None, None], kern[:q, 0][jnp.clip(lag, 0, q - 1)], 0.0)
    tb = jnp.where((lag <= 0)[:, :, None, None, None], kern[:q, 1][jnp.clip(-lag, 0, q - 1)], 0.0)
    tm = (tf + tb).transpose(2, 0, 4, 1, 3).reshape(SSM_G, CW, CW)

    def state_in(pr, pi, d):
        re = pr[..., None] * bbr[d][None] - pi[..., None] * bbi[d][None]
        im = pr[..., None] * bbi[d][None] + pi[..., None] * bbr[d][None]
        fix = lambda a: a.transpose(1, 0, 3, 2).reshape(SSM_G, CW, SSM_P)
        return fix(re), fix(im)

    f_re, f_im = state_in(pwr[:q, 0][::-1], pwi[:q, 0][::-1], 0)
    g_re, g_im = state_in(pwr[:q, 1], pwi[:q, 1], 1)
    rm = jnp.concatenate([f_re, g_re, f_im, g_im], axis=-1)

    fix_o = lambda a: a.transpose(1, 3, 0, 2).reshape(SSM_G, SSM_P, CW)
    om = jnp.concatenate([fix_o(mr[1:, 0]), fix_o(mr[1:, 1][::-1]),
                          fix_o(-mi[1:, 0]), fix_o(-mi[1:, 1][::-1])], axis=1)
    lq = jnp.stack([jnp.concatenate([pwr[q, 0], pwr[q, 1]], -1),
                    jnp.concatenate([pwi[q, 0], pwi[q, 1]], -1)], axis=1)
    dvec = jnp.tile(d_skip[:, None, :], (1, q, 1)).reshape(SSM_G, 1, CW)
    return tm.astype(BF16), rm.astype(BF16), om.astype(BF16), lq, dvec


def _scan_kernel(x_ref, t_ref, r_ref, o_ref, lq_ref, d_ref, h0_ref, y_ref, fin_ref,
                 rr_ref, st_ref, *, n_chunks, bsz):
    p2 = 2 * SSM_P
    x = x_ref[...]
    rr_ref[...] = jnp.dot(x, r_ref[...], preferred_element_type=F32)
    lq = lq_ref[...]
    lre, lim = lq[0:1], lq[1:2]
    is_f = lax.broadcasted_iota(jnp.int32, (bsz, p2), 1) < SSM_P
    h0 = h0_ref[...]

    def step(j, carry):
        sre, sim = carry
        rf = pl.multiple_of(j * bsz, bsz)
        rb = pl.multiple_of((n_chunks - 1 - j) * bsz, bsz)
        st_ref[pl.ds(rf, bsz), 0:SSM_P] = sre[:, 0:SSM_P]
        st_ref[pl.ds(rb, bsz), SSM_P:p2] = sre[:, SSM_P:p2]
        st_ref[pl.ds(rf, bsz), p2:p2 + SSM_P] = sim[:, 0:SSM_P]
        st_ref[pl.ds(rb, bsz), p2 + SSM_P:2 * p2] = sim[:, SSM_P:p2]
        r_re = jnp.where(is_f, rr_ref[pl.ds(rf, bsz), 0:p2], rr_ref[pl.ds(rb, bsz), 0:p2])
        r_im = jnp.where(is_f, rr_ref[pl.ds(rf, bsz), p2:2 * p2], rr_ref[pl.ds(rb, bsz), p2:2 * p2])
        return lre * sre - lim * sim + r_re, lre * sim + lim * sre + r_im

    sre, sim = lax.fori_loop(0, n_chunks, step, (h0[:, 0:p2], h0[:, p2:2 * p2]))
    fin_ref[...] = jnp.concatenate([sre, sim], axis=1)
    y = jnp.dot(x, t_ref[...], preferred_element_type=F32)
    y = y + jnp.dot(st_ref[...].astype(BF16), o_ref[...], preferred_element_type=F32)
    y = y + d_ref[...] * x.astype(F32)
    y_ref[...] = jax.nn.gelu(y, approximate=True).astype(BF16)


def _scan(xg, tm, rm, om, lq, dvec, h0, bsz):
    g, n, _ = xg.shape
    n_chunks = n // bsz
    mat = pl.BlockSpec((None, CW, CW), lambda i: (i, 0, 0))
    return pl.pallas_call(
        functools.partial(_scan_kernel, n_chunks=n_chunks, bsz=bsz),
        grid=(g,),
        in_specs=[pl.BlockSpec((None, n, CW), lambda i: (i, 0, 0)), mat, mat, mat,
                  pl.BlockSpec((None, 2, 2 * SSM_P), lambda i: (i, 0, 0)),
                  pl.BlockSpec((None, 1, CW), lambda i: (i, 0, 0)),
                  pl.BlockSpec((None, bsz, 4 * SSM_P), lambda i: (i, 0, 0))],
        out_specs=[pl.BlockSpec((None, n, CW), lambda i: (i, 0, 0)),
                   pl.BlockSpec((None, bsz, 4 * SSM_P), lambda i: (i, 0, 0))],
        out_shape=[jax.ShapeDtypeStruct((g, n, CW), BF16),
                   jax.ShapeDtypeStruct((g, bsz, 4 * SSM_P), F32)],
        scratch_shapes=[pltpu.VMEM((n, 4 * SSM_P), F32), pltpu.VMEM((n, 4 * SSM_P), F32)],
        compiler_params=_cp("arbitrary"),
        name="s5_scan",
    )(xg, tm, rm, om, lq, dvec, h0)


def _to_chunks(u, bsz, length):
    nc = length // CHUNK
    return (u.reshape(bsz, nc, CHUNK, SSM_G, SSM_H).transpose(3, 1, 0, 2, 4)
            .reshape(SSM_G, nc * bsz, CW))


def _from_chunks(y, bsz, length):
    nc = length // CHUNK
    return (y.reshape(SSM_G, nc, bsz, CHUNK, SSM_H).transpose(2, 1, 3, 0, 4)
            .reshape(bsz * length, D_SSM))


def _conv_post(cv, p_ref):
    p = p_ref[...]
    cv = cv + p[0:1]
    y = _silu(_ln(cv) * p[1:2] + p[2:3])
    return y * lax.rsqrt(jnp.mean(y * y, axis=-1, keepdims=True) + LN_EPS) * p[3:4]


def _conv_seq(buf_ref, base, w_ref, c0):
    half = D_CONV // 2
    win = buf_ref[pl.ds(base, CONV_WIN), c0:c0 + half]
    acc = jnp.zeros((GRID_W, half), F32)
    for ph in range(8):
        wb = win if ph == 0 else pltpu.roll(win, CONV_WIN - ph, axis=0)
        for a in range(4):
            j = 8 * a + ph - 1
            if 0 <= j < CONV_K:
                acc = acc + w_ref[j:j + 1, c0:c0 + half] * wb[8 * a:8 * a + GRID_W]
    return acc


def _conv_lat_kernel(z_ref, w_ref, p_ref, o_ref, hb_ref, vb_ref, *, rows):
    half = D_CONV // 2
    hstride = GRID_W + 16
    hb_ref[...] = jnp.zeros(hb_ref.shape, F32)
    vb_ref[pl.ds(0, CONV_PAD * GRID_W), :] = jnp.zeros((CONV_PAD * GRID_W, half), F32)
    vb_ref[pl.ds((CONV_PAD + rows) * GRID_W, CONV_PAD * GRID_W), :] = jnp.zeros((CONV_PAD * GRID_W, half), F32)
    for r in range(rows):
        hb_ref[pl.ds(16 + r * hstride, GRID_W), :] = z_ref[pl.ds(r * GRID_W, GRID_W), 0:half].astype(F32)
    vb_ref[pl.ds(CONV_PAD * GRID_W, rows * GRID_W), :] = z_ref[:, half:D_CONV].astype(F32)

    def row(r, carry):
        hbase = pl.multiple_of(r * hstride, 16)
        vbase = pl.multiple_of(r * GRID_W, GRID_W)
        acc_h = _conv_seq(hb_ref, hbase, w_ref, 0)
        acc_v = jnp.zeros((GRID_W, half), F32)
        for j in range(CONV_K):
            acc_v = acc_v + w_ref[j:j + 1, half:D_CONV] * vb_ref[pl.ds(vbase + j * GRID_W, GRID_W), :]
        out = _conv_post(jnp.concatenate([acc_h, acc_v], axis=1), p_ref)
        o_ref[pl.ds(vbase, GRID_W), :] = out.astype(BF16)
        return carry

    lax.fori_loop(0, rows, row, 0)


def _conv_lat(z, w, p, bsz, seq, blk0):
    rows = seq // GRID_W
    half = D_CONV // 2
    return pl.pallas_call(
        functools.partial(_conv_lat_kernel, rows=rows),
        grid=(bsz,),
        in_specs=[pl.BlockSpec((seq, D_CONV), lambda b: (blk0 + b, 0)),
                  pl.BlockSpec(w.shape, lambda b: (0, 0)),
                  pl.BlockSpec(p.shape, lambda b: (0, 0))],
        out_specs=pl.BlockSpec((seq, D_CONV), lambda b: (b, 0)),
        out_shape=jax.ShapeDtypeStruct((bsz * seq, D_CONV), BF16),
        scratch_shapes=[pltpu.VMEM((rows * (GRID_W + 16) + 16, half), F32),
                        pltpu.VMEM(((rows + 2 * CONV_PAD) * GRID_W, half), F32)],
        compiler_params=_cp("arbitrary"),
        name="conv_latent",
    )(z, w, p)


def _conv_ctx_kernel(z_ref, w_ref, p_ref, o_ref, cb_ref, *, clen):
    cb_ref[pl.ds(0, 16), :] = jnp.zeros((16, D_CONV), F32)
    cb_ref[pl.ds(16 + clen, 16), :] = jnp.zeros((16, D_CONV), F32)
    cb_ref[pl.ds(16, clen), :] = z_ref[...].astype(F32)
    for blk in range(clen // GRID_W):
        base = blk * GRID_W
        acc = jnp.concatenate([_conv_seq(cb_ref, base, w_ref, 0),
                               _conv_seq(cb_ref, base, w_ref, D_CONV // 2)], axis=1)
        o_ref[pl.ds(base, GRID_W), :] = _conv_post(acc, p_ref).astype(BF16)


def _conv_ctx(z, w, p, bsz, clen):
    return pl.pallas_call(
        functools.partial(_conv_ctx_kernel, clen=clen),
        grid=(bsz,),
        in_specs=[pl.BlockSpec((clen, D_CONV), lambda b: (b, 0)),
                  pl.BlockSpec(w.shape, lambda b: (0, 0)),
                  pl.BlockSpec(p.shape, lambda b: (0, 0))],
        out_specs=pl.BlockSpec((clen, D_CONV), lambda b: (b, 0)),
        out_shape=jax.ShapeDtypeStruct((bsz * clen, D_CONV), BF16),
        scratch_shapes=[pltpu.VMEM((clen + 32, D_CONV), F32)],
        compiler_params=_cp("arbitrary"),
        name="conv_context",
    )(z, w, p)


def _outproj_kernel(ys_ref, yc_ref, x_ref, mod_ref, wglu_ref, pv_ref, wo_ref, ln_ref, o_ref):
    ys = ys_ref[...]
    pv = pv_ref[...]
    gl = jnp.dot(ys, wglu_ref[...], preferred_element_type=F32) + pv[0:1]
    yg = ys.astype(F32) * jax.nn.sigmoid(gl)
    yn = yg * lax.rsqrt(jnp.mean(yg * yg, axis=-1, keepdims=True) + LN_EPS) * pv[1:2]
    y = jnp.dot(yn.astype(BF16), wo_ref[0:D_SSM, :], preferred_element_type=F32)
    y = y + jnp.dot(yc_ref[...], wo_ref[D_SSM:, :], preferred_element_type=F32)
    m = mod_ref[...]
    ln = ln_ref[...]
    o_ref[...] = _ln(ALPHA * x_ref[...] + m[2:3] * y) * ln[0:1] + ln[1:2]


def _outproj(ys, yc, xs, mod, wglu, pv, wo, ln, t_ctx, seq):
    t, d = xs.shape
    full = lambda a: pl.BlockSpec(a.shape, lambda i: (0, 0))
    return pl.pallas_call(
        _outproj_kernel,
        grid=(t // TM,),
        in_specs=[pl.BlockSpec((TM, D_SSM), lambda i: (i, 0)),
                  pl.BlockSpec((TM, D_CONV), lambda i: (i, 0)),
                  pl.BlockSpec((TM, d), lambda i: (i, 0)),
                  pl.BlockSpec((None, 6, d), lambda i: (_mod_row(i, t_ctx, seq), 0, 0)),
                  full(wglu), full(pv), full(wo), full(ln)],
        out_specs=pl.BlockSpec((TM, d), lambda i: (i, 0)),
        out_shape=jax.ShapeDtypeStruct((t, d), F32),
        compiler_params=_cp("arbitrary"),
        name="outproj",
    )(ys, yc, xs, mod, wglu, pv, wo, ln)


ROW_TILE = 8
LANES = 128


def _rows_to_tiles(ref, lead, val):
    n = val.shape[0]
    for c in range(ROW_TILE):
        ref[lead + (pl.ds(c, n, stride=ROW_TILE), slice(None))] = val[:, c * LANES:(c + 1) * LANES]


def _tiles_to_rows(ref, lead, n):
    return jnp.concatenate([ref[lead + (pl.ds(c, n, stride=ROW_TILE), slice(None))]
                            for c in range(ROW_TILE)], axis=1)


def _router_kernel(x_ref, mod_ref, wr_ref, rb_ref, h_ref, idx_ref, wt_ref, cnt_ref, carry_ref):
    @pl.when(pl.program_id(0) == 0)
    def _():
        carry_ref[...] = jnp.zeros(carry_ref.shape, F32)

    m = mod_ref[...]
    h = _ln(x_ref[...]) * (1.0 + m[4:5]) + m[3:4]
    _rows_to_tiles(h_ref, (), h)
    logits = lax.dot_general(wr_ref[...], h, (((1,), (1,)), ((), ())),
                             precision=lax.Precision.HIGHEST, preferred_element_type=F32)
    s = jax.nn.sigmoid(logits)
    biased = s + rb_ref[...]
    ninf = -jnp.inf
    tm = s.shape[1]

    row8 = lax.broadcasted_iota(jnp.int32, (GROUP_SIZE, tm), 0).astype(F32)
    gs = []
    for q in range(N_GROUPS):
        v = biased[q * GROUP_SIZE:(q + 1) * GROUP_SIZE]
        m1 = jnp.max(v, axis=0, keepdims=True)
        i1 = jnp.min(jnp.where(v == m1, row8, float(GROUP_SIZE)), axis=0, keepdims=True)
        m2 = jnp.max(jnp.where(row8 == i1, ninf, v), axis=0, keepdims=True)
        gs.append(m1 + m2)
    gsc = jnp.concatenate(gs, axis=0)
    rowg = lax.broadcasted_iota(jnp.int32, (N_GROUPS, tm), 0).astype(F32)
    gsel = jnp.zeros((N_GROUPS, tm), F32)
    for _ in range(TOP_K_GROUPS):
        mx = jnp.max(gsc, axis=0, keepdims=True)
        ii = jnp.min(jnp.where(gsc == mx, rowg, float(N_GROUPS)), axis=0, keepdims=True)
        hit = rowg == ii
        gsel = jnp.where(hit, 1.0, gsel)
        gsc = jnp.where(hit, ninf, gsc)
    emask = jnp.concatenate([jnp.broadcast_to(gsel[q:q + 1], (GROUP_SIZE, tm))
                             for q in range(N_GROUPS)], axis=0)
    masked = jnp.where(emask > 0.0, biased, ninf)

    rowe = lax.broadcasted_iota(jnp.int32, (N_EXPERTS, tm), 0).astype(F32)
    selm = jnp.zeros((N_EXPERTS, tm), F32)
    idxs, ws = [], []
    for _ in range(TOP_K):
        mx = jnp.max(masked, axis=0, keepdims=True)
        ii = jnp.min(jnp.where(masked == mx, rowe, float(N_EXPERTS)), axis=0, keepdims=True)
        hit = rowe == ii
        idxs.append(ii)
        ws.append(jnp.sum(jnp.where(hit, s, 0.0), axis=0, keepdims=True))
        selm = jnp.where(hit, 1.0, selm)
        masked = jnp.where(hit, ninf, masked)
    wsum = ws[0]
    for k in range(1, TOP_K):
        wsum = wsum + ws[k]
    idx_ref[...] = jnp.concatenate(idxs, axis=0).astype(jnp.int32)
    wt_ref[...] = jnp.concatenate([w / wsum * ROUTED_SCALE for w in ws], axis=0)

    new_carry = carry_ref[...] + jnp.sum(selm, axis=1, keepdims=True)
    carry_ref[...] = new_carry
    cnt_ref[...] = new_carry


def _router(xs, mod, wr_t, rb, t_ctx, seq):
    t, d = xs.shape
    full = lambda a: pl.BlockSpec(a.shape, lambda i: (0, 0))
    return pl.pallas_call(
        _router_kernel,
        grid=(t // TM,),
        in_specs=[pl.BlockSpec((TM, d), lambda i: (i, 0)),
                  pl.BlockSpec((None, 6, d), lambda i: (_mod_row(i, t_ctx, seq), 0, 0)),
                  full(wr_t), full(rb)],
        out_specs=[pl.BlockSpec((TM * ROW_TILE, LANES), lambda i: (i, 0)),
                   pl.BlockSpec((TOP_K, TM), lambda i: (0, i)),
                   pl.BlockSpec((TOP_K, TM), lambda i: (0, i)),
                   pl.BlockSpec((N_EXPERTS, 1), lambda i: (0, 0))],
        out_shape=[jax.ShapeDtypeStruct((t * ROW_TILE, LANES), F32),
                   jax.ShapeDtypeStruct((TOP_K, t), jnp.int32),
                   jax.ShapeDtypeStruct((TOP_K, t), F32),
                   jax.ShapeDtypeStruct((N_EXPERTS, 1), F32)],
        scratch_shapes=[pltpu.VMEM((N_EXPERTS, 1), F32)],
        compiler_params=_cp("arbitrary"),
        name="router",
    )(xs, mod, wr_t, rb)


def _experts_kernel(te_ref, tv_ref, src0_ref, srcn_ref, dst_ref, h_ref, wg_ref, wu_ref, wd_ref,
                    y_ref, xbuf, ybuf, gsem, ssem):
    i = pl.program_id(0)
    valid = tv_ref[i] > 0
    prev_valid = tv_ref[jnp.maximum(i - 1, 0)] > 0
    rows = TME * ROW_TILE

    def gather_start(idx_ref, s):
        for r in range(TME):
            pltpu.make_async_copy(
                h_ref.at[pl.ds(pl.multiple_of(idx_ref[0, r], ROW_TILE), ROW_TILE), :],
                xbuf.at[s, pl.ds(r * ROW_TILE, ROW_TILE), :], gsem.at[s]).start()

    def scatter_start(s):
        for r in range(TME):
            pltpu.make_async_copy(
                ybuf.at[s, pl.ds(r * ROW_TILE, ROW_TILE), :],
                y_ref.at[pl.ds(pl.multiple_of(dst_ref[0, r], ROW_TILE), ROW_TILE), :],
                ssem.at[s]).start()

    def gather_wait(s):
        pltpu.make_async_copy(h_ref.at[pl.ds(0, rows), :], xbuf.at[s], gsem.at[s]).wait()

    def scatter_wait(s):
        pltpu.make_async_copy(ybuf.at[s], y_ref.at[pl.ds(0, rows), :], ssem.at[s]).wait()

    @pl.when(i == 0)
    def _():
        gather_start(src0_ref, 0)

    def step(s):
        @pl.when(valid)
        def _():
            @pl.when(i >= 2)
            def _():
                scatter_wait(s)

            gather_wait(s)
            gather_start(srcn_ref, 1 - s)
            x = _tiles_to_rows(xbuf, (s,), TME).astype(BF16)
            a = jnp.dot(x, wg_ref[...], preferred_element_type=F32)
            b = jnp.dot(x, wu_ref[...], preferred_element_type=F32)
            hid = (_silu(a) * b).astype(BF16)
            _rows_to_tiles(ybuf, (s,), jnp.dot(hid, wd_ref[...], preferred_element_type=F32))
            scatter_start(s)

        @pl.when(jnp.logical_not(valid) & prev_valid & (i >= 1))
        def _():
            gather_wait(s)
            scatter_wait(1 - s)

            @pl.when(i >= 2)
            def _():
                scatter_wait(s)

            dump = pltpu.make_async_copy(ybuf.at[s], y_ref.at[pl.ds(y_ref.shape[0] - rows, rows), :],
                                         ssem.at[s])
            dump.start()
            dump.wait()

    for s in range(2):
        pl.when(i % 2 == s)(functools.partial(step, s))


def _experts(tile_expert, tile_valid, src, dst, h2, wg, wu, wd, n_out):
    n_tiles = src.shape[0]
    d, f = wg.shape[1:]
    smem = lambda fn: pl.BlockSpec((None, 1, TME), fn, memory_space=pltpu.SMEM)
    grid_spec = pltpu.PrefetchScalarGridSpec(
        num_scalar_prefetch=2,
        grid=(n_tiles,),
        in_specs=[smem(lambda i, te, tv: (0, 0, 0)),
                  smem(lambda i, te, tv: (jnp.minimum(i + 1, n_tiles - 1), 0, 0)),
                  smem(lambda i, te, tv: (i, 0, 0)),
                  pl.BlockSpec(memory_space=pl.ANY),
                  pl.BlockSpec((None, d, f), lambda i, te, tv: (te[i], 0, 0)),
                  pl.BlockSpec((None, d, f), lambda i, te, tv: (te[i], 0, 0)),
                  pl.BlockSpec((None, f, d), lambda i, te, tv: (te[i], 0, 0))],
        out_specs=pl.BlockSpec(memory_space=pl.ANY),
        scratch_shapes=[pltpu.VMEM((2, TME * ROW_TILE, LANES), F32),
                        pltpu.VMEM((2, TME * ROW_TILE, LANES), F32),
                        pltpu.SemaphoreType.DMA((2,)), pltpu.SemaphoreType.DMA((2,))],
    )
    return pl.pallas_call(
        _experts_kernel,
        grid_spec=grid_spec,
        out_shape=jax.ShapeDtypeStruct((n_out * ROW_TILE, LANES), F32),
        compiler_params=_cp("arbitrary"),
        name="experts",
    )(tile_expert, tile_valid, src, src, dst, h2, wg, wu, wd)


def _combine_kernel(*refs):
    yg_refs = refs[:TOP_K]
    wt_ref, h_ref, x_ref, mod_ref, wsg_ref, wsu_ref, wsd_ref, ln_ref, o_ref = refs[TOP_K:]
    wt = wt_ref[...]
    n = x_ref.shape[0]
    acc = wt[:, 0:1] * _tiles_to_rows(yg_refs[0], (), n)
    for k in range(1, TOP_K):
        acc = acc + wt[:, k:k + 1] * _tiles_to_rows(yg_refs[k], (), n)
    h = _tiles_to_rows(h_ref, (), n).astype(BF16)
    a = jnp.dot(h, wsg_ref[...], preferred_element_type=F32)
    b = jnp.dot(h, wsu_ref[...], preferred_element_type=F32)
    acc = acc + jnp.dot((_silu(a) * b).astype(BF16), wsd_ref[...], preferred_element_type=F32)
    m = mod_ref[...]
    ln = ln_ref[...]
    o_ref[...] = _ln(ALPHA * x_ref[...] + m[5:6] * acc) * ln[0:1] + ln[1:2]


def _combine(yg, wt, h2, xs, mod, wsg, wsu, wsd, ln, t_ctx, seq):
    t, d = xs.shape
    full = lambda a: pl.BlockSpec(a.shape, lambda i: (0, 0))
    n_blocks = t // TM
    slot_specs = [pl.BlockSpec((TM * ROW_TILE, LANES), functools.partial(lambda i, k: (k * n_blocks + i, 0), k=k))
                  for k in range(TOP_K)]
    return pl.pallas_call(
        _combine_kernel,
        grid=(n_blocks,),
        in_specs=slot_specs + [
                  pl.BlockSpec((TM, TOP_K), lambda i: (i, 0)),
                  pl.BlockSpec((TM * ROW_TILE, LANES), lambda i: (i, 0)),
                  pl.BlockSpec((TM, d), lambda i: (i, 0)),
                  pl.BlockSpec((None, 6, d), lambda i: (_mod_row(i, t_ctx, seq), 0, 0)),
                  full(wsg), full(wsu), full(wsd), full(ln)],
        out_specs=pl.BlockSpec((TM, d), lambda i: (i, 0)),
        out_shape=jax.ShapeDtypeStruct((t, d), F32),
        compiler_params=_cp("arbitrary"),
        name="combine",
    )(*([yg] * TOP_K), wt, h2, xs, mod, wsg, wsu, wsd, ln)


def _moe(xs, mod, wr_t, rb, wg, wu, wd, wsg, wsu, wsd, ln, t_ctx, seq):
    t, d = xs.shape
    h2, idx, wt, cnt = _router(xs, mod, wr_t, rb, t_ctx, seq)
    n_pairs = t * TOP_K
    n_rows = n_pairs + N_EXPERTS * TME
    n_tiles = n_rows // TME
    assert n_rows < (1 << KEY_SHIFT) and n_pairs % TME == 0
    counts = cnt[:, 0].astype(jnp.int32)
    n_padding = (-counts) % TME
    slot = jnp.arange(TME, dtype=jnp.int32)[None, :]
    pad_key = jnp.where(slot < n_padding[:, None], jnp.arange(N_EXPERTS, dtype=jnp.int32)[:, None],
                        N_EXPERTS).reshape(-1)
    keys = jnp.concatenate([idx.reshape(-1), pad_key]) * (1 << KEY_SHIFT) + jnp.arange(n_rows, dtype=jnp.int32)
    order = lax.sort(keys)
    pair = order & ((1 << KEY_SHIFT) - 1)
    src = ((pair % t) * ROW_TILE).reshape(n_tiles, 1, TME)
    dst = (jnp.where(pair < n_pairs, pair, n_pairs + pair % TME) * ROW_TILE).reshape(n_tiles, 1, TME)
    first = order[::TME] >> KEY_SHIFT
    tile_expert = jnp.minimum(first, N_EXPERTS - 1)
    tile_valid = (first < N_EXPERTS).astype(jnp.int32)
    ye = _experts(tile_expert, tile_valid, src, dst, h2, wg, wu, wd, n_pairs + TME)
    return _combine(ye, wt.T, h2, xs, mod, wsg, wsu, wsd, ln, t_ctx, seq)


def kernel(x, c, ctx, c_ctx, w_ada, b_ada, w_in, ssm_a_re, ssm_a_im, ssm_log_dt, ssm_b_re, ssm_b_im,
           ssm_c_re, ssm_c_im, ssm_d, w_glu, b_glu, conv_w, conv_b, conv_ln_g, conv_ln_b, mix_norm_g,
           w_out, ln1_g, ln1_b, w_router, router_bias, we_gate, we_up, we_down, ws_gate, ws_up,
           ws_down, ln2_g, ln2_b):
    bsz, seq, d = x.shape
    clen = ctx.shape[1]
    depth = w_in.shape[0]
    t_ctx, t_lat = bsz * clen, bsz * seq
    assert d == D_MODEL and bsz + 1 <= MOD_ROWS
    assert seq % GRID_W == 0 and clen % GRID_W == 0 and seq % TM == 0 and t_ctx % seq == 0

    xs = jnp.concatenate([ctx.reshape(t_ctx, d), x.reshape(t_lat, d)], axis=0)
    cc = jnp.zeros((MOD_ROWS, d), F32).at[0].set(c_ctx).at[1:bsz + 1].set(c)
    mod_all = _ada(cc, w_ada, b_ada).reshape(depth, MOD_ROWS, 6, d)

    for l in range(depth):
        mod = mod_all[l]
        u, z = _inproj(xs, mod, w_in[l].astype(BF16), t_ctx, seq)

        tm, rm, om, lq, dvec = _s5_mats(ssm_a_re[l], ssm_a_im[l], ssm_log_dt[l], ssm_b_re[l],
                                         ssm_b_im[l], ssm_c_re[l], ssm_c_im[l], ssm_d[l])
        h0 = jnp.zeros((SSM_G, bsz, 4 * SSM_P), F32)
        yc, fin = _scan(_to_chunks(u[:t_ctx], bsz, clen), tm, rm, om, lq, dvec, h0, bsz)
        yl, _ = _scan(_to_chunks(u[t_ctx:], bsz, seq), tm, rm, om, lq, dvec, fin, bsz)
        ys = jnp.concatenate([_from_chunks(yc, bsz, clen), _from_chunks(yl, bsz, seq)], axis=0)

        cp = jnp.stack([conv_b[l], conv_ln_g[l], conv_ln_b[l], mix_norm_g[l, D_SSM:]])
        zc = jnp.concatenate([_conv_ctx(z, conv_w[l], cp, bsz, clen),
                              _conv_lat(z, conv_w[l], cp, bsz, seq, t_ctx // seq)], axis=0)

        pv = jnp.stack([b_glu[l], mix_norm_g[l, :D_SSM]])
        xs = _outproj(ys, zc, xs, mod, w_glu[l].astype(BF16), pv, w_out[l].astype(BF16),
                      jnp.stack([ln1_g[l], ln1_b[l]]), t_ctx, seq)

        xs = _moe(xs, mod, w_router[l].T, router_bias[l][:, None],
                  we_gate[l].astype(BF16), we_up[l].astype(BF16), we_down[l].astype(BF16),
                  ws_gate[l].astype(BF16), ws_up[l].astype(BF16), ws_down[l].astype(BF16),
                  jnp.stack([ln2_g[l], ln2_b[l]]), t_ctx, seq)

    return xs[t_ctx:].reshape(bsz, seq, d)
```

```python
import functools
import math

import jax
import jax.numpy as jnp
from jax import lax
from jax.experimental import pallas as pl
from jax.experimental.pallas import tpu as pltpu

F32 = jnp.float32
BF16 = jnp.bfloat16

D_MODEL = 1024
DEPTH = 4
GRID_W = 64
D_SSM = 512
D_CONV = 512
SSM_H = 16
SSM_G = 32
SSM_P = 64
CONV_K = 31
CONV_PAD = 15
CONV_WIN = GRID_W + 32
N_EXPERTS = 64
TOP_K = 8
N_GROUPS = 8
TOP_K_GROUPS = 4
GROUP_SIZE = N_EXPERTS // N_GROUPS
D_EXPERT = 256
ROUTED_SCALE = 2.5
ALPHA = (2 * DEPTH) ** 0.25
LN_EPS = 1e-5

CHUNK = 16
CW = CHUNK * SSM_H
MOD_ROWS = 32
TM = 256
TME = 256
KEY_SHIFT = 19
TMC = 128
COMBINE_ROWS = 32
VMEM_LIMIT = 48 * 1024 * 1024


def _cp(*sem):
    return pltpu.CompilerParams(dimension_semantics=sem, vmem_limit_bytes=VMEM_LIMIT)


def _ln(x):
    mu = jnp.mean(x, axis=-1, keepdims=True)
    xc = x - mu
    var = jnp.mean(xc * xc, axis=-1, keepdims=True)
    return xc * lax.rsqrt(var + LN_EPS)


def _silu(x):
    return x * jax.nn.sigmoid(x)


def _ada_kernel(c_ref, w_ref, b_ref, o_ref):
    a = _silu(c_ref[...]).astype(BF16)
    o_ref[...] = jnp.dot(a, w_ref[...].astype(BF16), preferred_element_type=F32) + b_ref[...]


def _ada(cc, w_ada, b_ada):
    depth, d, n = w_ada.shape
    tn = 1536
    return pl.pallas_call(
        _ada_kernel,
        grid=(depth, n // tn),
        in_specs=[pl.BlockSpec((MOD_ROWS, d), lambda l, j: (0, 0)),
                  pl.BlockSpec((None, d, tn), lambda l, j: (l, 0, j)),
                  pl.BlockSpec((None, 1, tn), lambda l, j: (l, 0, j))],
        out_specs=pl.BlockSpec((None, MOD_ROWS, tn), lambda l, j: (l, 0, j)),
        out_shape=jax.ShapeDtypeStruct((depth, MOD_ROWS, n), F32),
        compiler_params=_cp("arbitrary", "arbitrary"),
        name="ada",
    )(cc, w_ada, b_ada.reshape(depth, 1, n))


def _inproj_kernel(x_ref, mod_ref, w_ref, u_ref, z_ref):
    m = mod_ref[...]
    h = _ln(x_ref[...]) * (1.0 + m[1:2]) + m[0:1]
    p = jnp.dot(h.astype(BF16), w_ref[...], preferred_element_type=F32)
    u_ref[...] = p[:, :D_SSM].astype(BF16)
    v = p[:, D_SSM:D_SSM + D_CONV]
    g = p[:, D_SSM + D_CONV:]
    z_ref[...] = (v * jax.nn.sigmoid(g)).astype(BF16)


def _mod_row(i, t_ctx, seq, tile=TM):
    start = i * tile
    return jnp.where(start < t_ctx, 0, 1 + (start - t_ctx) // seq)


def _inproj(xs, mod, w, t_ctx, seq):
    t, d = xs.shape
    return pl.pallas_call(
        _inproj_kernel,
        grid=(t // TM,),
        in_specs=[pl.BlockSpec((TM, d), lambda i: (i, 0)),
                  pl.BlockSpec((None, 6, d), lambda i: (_mod_row(i, t_ctx, seq), 0, 0)),
                  pl.BlockSpec(w.shape, lambda i: (0, 0))],
        out_specs=[pl.BlockSpec((TM, D_SSM), lambda i: (i, 0)),
                   pl.BlockSpec((TM, D_CONV), lambda i: (i, 0))],
        out_shape=[jax.ShapeDtypeStruct((t, D_SSM), BF16),
                   jax.ShapeDtypeStruct((t, D_CONV), BF16)],
        compiler_params=_cp("arbitrary"),
        name="inproj",
    )(xs, mod, w)


def _s5_mats(a_re, a_im, log_dt, b_re, b_im, c_re, c_im, d_skip):
    hp = lax.Precision.HIGHEST
    q = CHUNK
    dt = jnp.exp(log_dt)[..., None]
    ldr, ldi = a_re * dt, a_im * dt
    ks = jnp.arange(q + 1, dtype=F32)[:, None, None, None]
    mag = jnp.exp(ks * ldr)
    pwr, pwi = mag * jnp.cos(ks * ldi), mag * jnp.sin(ks * ldi)
    nr, ni = pwr[1] - 1.0, pwi[1]
    den = a_re * a_re + a_im * a_im
    qr, qi = (nr * a_re + ni * a_im) / den, (ni * a_re - nr * a_im) / den
    bbr = qr[..., None] * b_re - qi[..., None] * b_im
    bbi = qr[..., None] * b_im + qi[..., None] * b_re
    mr = c_re[None] * pwr[:, :, :, None, :] - c_im[None] * pwi[:, :, :, None, :]
    mi = c_re[None] * pwi[:, :, :, None, :] + c_im[None] * pwr[:, :, :, None, :]
    kern = (jnp.einsum('kdgop,dgph->kdgoh', mr, bbr, precision=hp)
            - jnp.einsum('kdgop,dgph->kdgoh', mi, bbi, precision=hp))
    s_i = jnp.arange(q)[:, None]
    t_i = jnp.arange(q)[None, :]
    lag = t_i - s_i
    tf = jnp.where((lag >= 0)[:, :, None, None, None], kern[:q, 0][jnp.clip(lag, 0, q - 1)], 0.0)
    tb = jnp.where((lag <= 0)[:, :, None, None, None], kern[:q, 1][jnp.clip(-lag, 0, q - 1)], 0.0)
    tm = (tf + tb).transpose(2, 0, 4, 1, 3).reshape(SSM_G, CW, CW)

    def state_in(pr, pi, d):
        re = pr[..., None] * bbr[d][None] - pi[..., None] * bbi[d][None]
        im = pr[..., None] * bbi[d][None] + pi[..., None] * bbr[d][None]
        fix = lambda a: a.transpose(1, 0, 3, 2).reshape(SSM_G, CW, SSM_P)
        return fix(re), fix(im)

    f_re, f_im = state_in(pwr[:q, 0][::-1], pwi[:q, 0][::-1], 0)
    g_re, g_im = state_in(pwr[:q, 1], pwi[:q, 1], 1)
    rm = jnp.concatenate([f_re, g_re, f_im, g_im], axis=-1)

    fix_o = lambda a: a.transpose(1, 3, 0, 2).reshape(SSM_G, SSM_P, CW)
    om = jnp.concatenate([fix_o(mr[1:, 0]), fix_o(mr[1:, 1][::-1]),
                          fix_o(-mi[1:, 0]), fix_o(-mi[1:, 1][::-1])], axis=1)
    lq = jnp.stack([jnp.concatenate([pwr[q, 0], pwr[q, 1]], -1),
                    jnp.concatenate([pwi[q, 0], pwi[q, 1]], -1)], axis=1)
    dvec = jnp.tile(d_skip[:, None, :], (1, q, 1)).reshape(SSM_G, 1, CW)
    return tm.astype(BF16), rm.astype(BF16), om.astype(BF16), lq, dvec


def _scan_kernel(x_ref, t_ref, r_ref, o_ref, lq_ref, d_ref, h0_ref, y_ref, fin_ref,
                 rr_ref, st_ref, *, n_chunks, bsz):
    p2 = 2 * SSM_P
    x = x_ref[...]
    rr_ref[...] = jnp.dot(x, r_ref[...], preferred_element_type=F32)
    lq = lq_ref[...]
    lre, lim = lq[0:1], lq[1:2]
    is_f = lax.broadcasted_iota(jnp.int32, (bsz, p2), 1) < SSM_P
    h0 = h0_ref[...]

    def step(j, carry):
        sre, sim = carry
        rf = pl.multiple_of(j * bsz, bsz)
        rb = pl.multiple_of((n_chunks - 1 - j) * bsz, bsz)
        st_ref[pl.ds(rf, bsz), 0:SSM_P] = sre[:, 0:SSM_P]
        st_ref[pl.ds(rb, bsz), SSM_P:p2] = sre[:, SSM_P:p2]
        st_ref[pl.ds(rf, bsz), p2:p2 + SSM_P] = sim[:, 0:SSM_P]
        st_ref[pl.ds(rb, bsz), p2 + SSM_P:2 * p2] = sim[:, SSM_P:p2]
        r_re = jnp.where(is_f, rr_ref[pl.ds(rf, bsz), 0:p2], rr_ref[pl.ds(rb, bsz), 0:p2])
        r_im = jnp.where(is_f, rr_ref[pl.ds(rf, bsz), p2:2 * p2], rr_ref[pl.ds(rb, bsz), p2:2 * p2])
        return lre * sre - lim * sim + r_re, lre * sim + lim * sre + r_im

    sre, sim = lax.fori_loop(0, n_chunks, step, (h0[:, 0:p2], h0[:, p2:2 * p2]))
    fin_ref[...] = jnp.concatenate([sre, sim], axis=1)
    y = jnp.dot(x, t_ref[...], preferred_element_type=F32)
    y = y + jnp.dot(st_ref[...].astype(BF16), o_ref[...], preferred_element_type=F32)
    y = y + d_ref[...] * x.astype(F32)
    y_ref[...] = jax.nn.gelu(y, approximate=True).astype(BF16)


def _scan(xg, tm, rm, om, lq, dvec, h0, bsz):
    g, n, _ = xg.shape
    n_chunks = n // bsz
    mat = pl.BlockSpec((None, CW, CW), lambda i: (i, 0, 0))
    return pl.pallas_call(
        functools.partial(_scan_kernel, n_chunks=n_chunks, bsz=bsz),
        grid=(g,),
        in_specs=[pl.BlockSpec((None, n, CW), lambda i: (i, 0, 0)), mat, mat, mat,
                  pl.BlockSpec((None, 2, 2 * SSM_P), lambda i: (i, 0, 0)),
                  pl.BlockSpec((None, 1, CW), lambda i: (i, 0, 0)),
                  pl.BlockSpec((None, bsz, 4 * SSM_P), lambda i: (i, 0, 0))],
        out_specs=[pl.BlockSpec((None, n, CW), lambda i: (i, 0, 0)),
                   pl.BlockSpec((None, bsz, 4 * SSM_P), lambda i: (i, 0, 0))],
        out_shape=[jax.ShapeDtypeStruct((g, n, CW), BF16),
                   jax.ShapeDtypeStruct((g, bsz, 4 * SSM_P), F32)],
        scratch_shapes=[pltpu.VMEM((n, 4 * SSM_P), F32), pltpu.VMEM((n, 4 * SSM_P), F32)],
        compiler_params=_cp("arbitrary"),
        name="s5_scan",
    )(xg, tm, rm, om, lq, dvec, h0)


def _to_chunks(u, bsz, length):
    nc = length // CHUNK
    return (u.reshape(bsz, nc, CHUNK, SSM_G, SSM_H).transpose(3, 1, 0, 2, 4)
            .reshape(SSM_G, nc * bsz, CW))


def _from_chunks(y, bsz, length):
    nc = length // CHUNK
    return (y.reshape(SSM_G, nc, bsz, CHUNK, SSM_H).transpose(2, 1, 3, 0, 4)
            .reshape(bsz * length, D_SSM))


def _conv_post(cv, p_ref):
    p = p_ref[...]
    cv = cv + p[0:1]
    y = _silu(_ln(cv) * p[1:2] + p[2:3])
    return y * lax.rsqrt(jnp.mean(y * y, axis=-1, keepdims=True) + LN_EPS) * p[3:4]


def _conv_seq(buf_ref, base, w_ref, c0):
    half = D_CONV // 2
    win = buf_ref[pl.ds(base, CONV_WIN), c0:c0 + half]
    acc = jnp.zeros((GRID_W, half), F32)
    for ph in range(8):
        wb = win if ph == 0 else pltpu.roll(win, CONV_WIN - ph, axis=0)
        for a in range(4):
            j = 8 * a + ph - 1
            if 0 <= j < CONV_K:
                acc = acc + w_ref[j:j + 1, c0:c0 + half] * wb[8 * a:8 * a + GRID_W]
    return acc


def _conv_lat_kernel(z_ref, w_ref, p_ref, o_ref, hb_ref, vb_ref, *, rows):
    half = D_CONV // 2
    hstride = GRID_W + 16
    hb_ref[...] = jnp.zeros(hb_ref.shape, F32)
    vb_ref[pl.ds(0, CONV_PAD * GRID_W), :] = jnp.zeros((CONV_PAD * GRID_W, half), F32)
    vb_ref[pl.ds((CONV_PAD + rows) * GRID_W, CONV_PAD * GRID_W), :] = jnp.zeros((CONV_PAD * GRID_W, half), F32)
    for r in range(rows):
        hb_ref[pl.ds(16 + r * hstride, GRID_W), :] = z_ref[pl.ds(r * GRID_W, GRID_W), 0:half].astype(F32)
    vb_ref[pl.ds(CONV_PAD * GRID_W, rows * GRID_W), :] = z_ref[:, half:D_CONV].astype(F32)

    def row(r, carry):
        hbase = pl.multiple_of(r * hstride, 16)
        vbase = pl.multiple_of(r * GRID_W, GRID_W)
        acc_h = _conv_seq(hb_ref, hbase, w_ref, 0)
        acc_v = jnp.zeros((GRID_W, half), F32)
        for j in range(CONV_K):
            acc_v = acc_v + w_ref[j:j + 1, half:D_CONV] * vb_ref[pl.ds(vbase + j * GRID_W, GRID_W), :]
        out = _conv_post(jnp.concatenate([acc_h, acc_v], axis=1), p_ref)
        o_ref[pl.ds(vbase, GRID_W), :] = out.astype(BF16)
        return carry

    lax.fori_loop(0, rows, row, 0)


def _conv_lat(z, w, p, bsz, seq, blk0):
    rows = seq // GRID_W
    half = D_CONV // 2
    return pl.pallas_call(
        functools.partial(_conv_lat_kernel, rows=rows),
        grid=(bsz,),
        in_specs=[pl.BlockSpec((seq, D_CONV), lambda b: (blk0 + b, 0)),
                  pl.BlockSpec(w.shape, lambda b: (0, 0)),
                  pl.BlockSpec(p.shape, lambda b: (0, 0))],
        out_specs=pl.BlockSpec((seq, D_CONV), lambda b: (b, 0)),
        out_shape=jax.ShapeDtypeStruct((bsz * seq, D_CONV), BF16),
        scratch_shapes=[pltpu.VMEM((rows * (GRID_W + 16) + 16, half), F32),
                        pltpu.VMEM(((rows + 2 * CONV_PAD) * GRID_W, half), F32)],
        compiler_params=_cp("arbitrary"),
        name="conv_latent",
    )(z, w, p)


def _conv_ctx_kernel(z_ref, w_ref, p_ref, o_ref, cb_ref, *, clen):
    cb_ref[pl.ds(0, 16), :] = jnp.zeros((16, D_CONV), F32)
    cb_ref[pl.ds(16 + clen, 16), :] = jnp.zeros((16, D_CONV), F32)
    cb_ref[pl.ds(16, clen), :] = z_ref[...].astype(F32)
    for blk in range(clen // GRID_W):
        base = blk * GRID_W
        acc = jnp.concatenate([_conv_seq(cb_ref, base, w_ref, 0),
                               _conv_seq(cb_ref, base, w_ref, D_CONV // 2)], axis=1)
        o_ref[pl.ds(base, GRID_W), :] = _conv_post(acc, p_ref).astype(BF16)


def _conv_ctx(z, w, p, bsz, clen):
    return pl.pallas_call(
        functools.partial(_conv_ctx_kernel, clen=clen),
        grid=(bsz,),
        in_specs=[pl.BlockSpec((clen, D_CONV), lambda b: (b, 0)),
                  pl.BlockSpec(w.shape, lambda b: (0, 0)),
                  pl.BlockSpec(p.shape, lambda b: (0, 0))],
        out_specs=pl.BlockSpec((clen, D_CONV), lambda b: (b, 0)),
        out_shape=jax.ShapeDtypeStruct((bsz * clen, D_CONV), BF16),
        scratch_shapes=[pltpu.VMEM((clen + 32, D_CONV), F32)],
        compiler_params=_cp("arbitrary"),
        name="conv_context",
    )(z, w, p)


def _outproj_kernel(ys_ref, yc_ref, x_ref, mod_ref, wglu_ref, pv_ref, wo_ref, ln_ref, o_ref):
    ys = ys_ref[...]
    pv = pv_ref[...]
    gl = jnp.dot(ys, wglu_ref[...], preferred_element_type=F32) + pv[0:1]
    yg = ys.astype(F32) * jax.nn.sigmoid(gl)
    yn = yg * lax.rsqrt(jnp.mean(yg * yg, axis=-1, keepdims=True) + LN_EPS) * pv[1:2]
    y = jnp.dot(yn.astype(BF16), wo_ref[0:D_SSM, :], preferred_element_type=F32)
    y = y + jnp.dot(yc_ref[...], wo_ref[D_SSM:, :], preferred_element_type=F32)
    m = mod_ref[...]
    ln = ln_ref[...]
    o_ref[...] = _ln(ALPHA * x_ref[...] + m[2:3] * y) * ln[0:1] + ln[1:2]


def _outproj(ys, yc, xs, mod, wglu, pv, wo, ln, t_ctx, seq):
    t, d = xs.shape
    full = lambda a: pl.BlockSpec(a.shape, lambda i: (0, 0))
    return pl.pallas_call(
        _outproj_kernel,
        grid=(t // TM,),
        in_specs=[pl.BlockSpec((TM, D_SSM), lambda i: (i, 0)),
                  pl.BlockSpec((TM, D_CONV), lambda i: (i, 0)),
                  pl.BlockSpec((TM, d), lambda i: (i, 0)),
                  pl.BlockSpec((None, 6, d), lambda i: (_mod_row(i, t_ctx, seq), 0, 0)),
                  full(wglu), full(pv), full(wo), full(ln)],
        out_specs=pl.BlockSpec((TM, d), lambda i: (i, 0)),
        out_shape=jax.ShapeDtypeStruct((t, d), F32),
        compiler_params=_cp("arbitrary"),
        name="outproj",
    )(ys, yc, xs, mod, wglu, pv, wo, ln)


ROW_TILE = 8
LANES = 128


def _rows_to_tiles(ref, lead, val):
    n = val.shape[0]
    for c in range(ROW_TILE):
        ref[lead + (pl.ds(c, n, stride=ROW_TILE), slice(None))] = val[:, c * LANES:(c + 1) * LANES]


def _tiles_to_rows(ref, lead, n, base=0):
    return jnp.concatenate([ref[lead + (pl.ds(base + c, n, stride=ROW_TILE), slice(None))]
                            for c in range(ROW_TILE)], axis=1)


def _router_kernel(x_ref, mod_ref, wr_ref, rb_ref, tri_ref, h_ref, idx_ref, wt_ref, rank_ref, cnt_ref,
                   carry_ref):
    @pl.when(pl.program_id(0) == 0)
    def _():
        carry_ref[...] = jnp.zeros(carry_ref.shape, F32)

    m = mod_ref[...]
    h = _ln(x_ref[...]) * (1.0 + m[4:5]) + m[3:4]
    _rows_to_tiles(h_ref, (), h)
    logits = lax.dot_general(wr_ref[...], h, (((1,), (1,)), ((), ())),
                             precision=lax.Precision.HIGHEST, preferred_element_type=F32)
    s = jax.nn.sigmoid(logits)
    biased = s + rb_ref[...]
    ninf = -jnp.inf
    tm = s.shape[1]

    row8 = lax.broadcasted_iota(jnp.int32, (GROUP_SIZE, tm), 0).astype(F32)
    gs = []
    for q in range(N_GROUPS):
        v = biased[q * GROUP_SIZE:(q + 1) * GROUP_SIZE]
        m1 = jnp.max(v, axis=0, keepdims=True)
        i1 = jnp.min(jnp.where(v == m1, row8, float(GROUP_SIZE)), axis=0, keepdims=True)
        m2 = jnp.max(jnp.where(row8 == i1, ninf, v), axis=0, keepdims=True)
        gs.append(m1 + m2)
    gsc = jnp.concatenate(gs, axis=0)
    rowg = lax.broadcasted_iota(jnp.int32, (N_GROUPS, tm), 0).astype(F32)
    gsel = jnp.zeros((N_GROUPS, tm), F32)
    for _ in range(TOP_K_GROUPS):
        mx = jnp.max(gsc, axis=0, keepdims=True)
        ii = jnp.min(jnp.where(gsc == mx, rowg, float(N_GROUPS)), axis=0, keepdims=True)
        hit = rowg == ii
        gsel = jnp.where(hit, 1.0, gsel)
        gsc = jnp.where(hit, ninf, gsc)
    emask = jnp.concatenate([jnp.broadcast_to(gsel[q:q + 1], (GROUP_SIZE, tm))
                             for q in range(N_GROUPS)], axis=0)
    masked = jnp.where(emask > 0.0, biased, ninf)

    rowe = lax.broadcasted_iota(jnp.int32, (N_EXPERTS, tm), 0).astype(F32)
    selm = jnp.zeros((N_EXPERTS, tm), F32)
    idxs, ws = [], []
    for _ in range(TOP_K):
        mx = jnp.max(masked, axis=0, keepdims=True)
        ii = jnp.min(jnp.where(masked == mx, rowe, float(N_EXPERTS)), axis=0, keepdims=True)
        hit = rowe == ii
        idxs.append(ii)
        ws.append(jnp.sum(jnp.where(hit, s, 0.0), axis=0, keepdims=True))
        selm = jnp.where(hit, 1.0, selm)
        masked = jnp.where(hit, ninf, masked)
    wsum = ws[0]
    for k in range(1, TOP_K):
        wsum = wsum + ws[k]
    idx_ref[...] = jnp.concatenate(idxs, axis=0).astype(jnp.int32)
    wt_ref[...] = jnp.concatenate([w / wsum * ROUTED_SCALE for w in ws], axis=0)

    cnt = jnp.dot(selm.astype(BF16), tri_ref[...], preferred_element_type=F32)
    carry = carry_ref[...]
    rank_excl = cnt - selm + carry
    rank_ref[...] = jnp.concatenate(
        [jnp.sum(jnp.where(rowe == idxs[k], rank_excl, 0.0), axis=0, keepdims=True)
         for k in range(TOP_K)], axis=0).astype(jnp.int32)
    new_carry = carry + cnt[:, tm - 1:tm]
    carry_ref[...] = new_carry
    cnt_ref[...] = new_carry


def _router(xs, mod, wr_t, rb, t_ctx, seq):
    t, d = xs.shape
    full = lambda a: pl.BlockSpec(a.shape, lambda i: (0, 0))
    tri = (jnp.arange(TM)[:, None] <= jnp.arange(TM)[None, :]).astype(BF16)
    return pl.pallas_call(
        _router_kernel,
        grid=(t // TM,),
        in_specs=[pl.BlockSpec((TM, d), lambda i: (i, 0)),
                  pl.BlockSpec((None, 6, d), lambda i: (_mod_row(i, t_ctx, seq), 0, 0)),
                  full(wr_t), full(rb), full(tri)],
        out_specs=[pl.BlockSpec((TM * ROW_TILE, LANES), lambda i: (i, 0)),
                   pl.BlockSpec((TOP_K, TM), lambda i: (0, i)),
                   pl.BlockSpec((TOP_K, TM), lambda i: (0, i)),
                   pl.BlockSpec((TOP_K, TM), lambda i: (0, i)),
                   pl.BlockSpec((N_EXPERTS, 1), lambda i: (0, 0))],
        out_shape=[jax.ShapeDtypeStruct((t * ROW_TILE, LANES), F32),
                   jax.ShapeDtypeStruct((TOP_K, t), jnp.int32),
                   jax.ShapeDtypeStruct((TOP_K, t), F32),
                   jax.ShapeDtypeStruct((TOP_K, t), jnp.int32),
                   jax.ShapeDtypeStruct((N_EXPERTS, 1), F32)],
        scratch_shapes=[pltpu.VMEM((N_EXPERTS, 1), F32)],
        compiler_params=_cp("arbitrary"),
        name="router",
    )(xs, mod, wr_t, rb, tri)


def _experts_kernel(te_ref, tv_ref, src0_ref, srcn_ref, h_ref, wg_ref, wu_ref, wd_ref, y_ref,
                    xbuf0, xbuf1, gsem):
    i = pl.program_id(0)
    valid = tv_ref[i] > 0
    prev_valid = tv_ref[jnp.maximum(i - 1, 0)] > 0
    rows = TME * ROW_TILE
    xbuf = (xbuf0, xbuf1)

    def gather_start(idx_ref, s):
        for r in range(TME):
            pltpu.make_async_copy(
                h_ref.at[pl.ds(pl.multiple_of(idx_ref[0, r], ROW_TILE), ROW_TILE), :],
                xbuf[s].at[pl.ds(r * ROW_TILE, ROW_TILE), :], gsem.at[s]).start()

    def gather_wait(s):
        pltpu.make_async_copy(h_ref.at[pl.ds(0, rows), :], xbuf[s], gsem.at[s]).wait()

    @pl.when(i == 0)
    def _():
        gather_start(src0_ref, 0)

    def step(s):
        @pl.when(valid)
        def _():
            gather_wait(s)
            gather_start(srcn_ref, 1 - s)
            x = _tiles_to_rows(xbuf[s], (), TME).astype(BF16)
            a = jnp.dot(x, wg_ref[...], preferred_element_type=F32)
            b = jnp.dot(x, wu_ref[...], preferred_element_type=F32)
            hid = (_silu(a) * b).astype(BF16)
            _rows_to_tiles(y_ref, (), jnp.dot(hid, wd_ref[...], preferred_element_type=F32))

        @pl.when(jnp.logical_not(valid))
        def _():
            y_ref[...] = jnp.zeros(y_ref.shape, F32)

            @pl.when(prev_valid & (i >= 1))
            def _():
                gather_wait(s)

    for s in range(2):
        pl.when(i % 2 == s)(functools.partial(step, s))


def _experts(tile_expert, tile_valid, src, h2, wg, wu, wd):
    n_tiles = src.shape[0]
    d, f = wg.shape[1:]
    smem = lambda fn: pl.BlockSpec((None, 1, TME), fn, memory_space=pltpu.SMEM)
    grid_spec = pltpu.PrefetchScalarGridSpec(
        num_scalar_prefetch=2,
        grid=(n_tiles,),
        in_specs=[smem(lambda i, te, tv: (0, 0, 0)),
                  smem(lambda i, te, tv: (jnp.minimum(i + 1, n_tiles - 1), 0, 0)),
                  pl.BlockSpec(memory_space=pl.ANY),
                  pl.BlockSpec((None, d, f), lambda i, te, tv: (te[i], 0, 0)),
                  pl.BlockSpec((None, d, f), lambda i, te, tv: (te[i], 0, 0)),
                  pl.BlockSpec((None, f, d), lambda i, te, tv: (te[i], 0, 0))],
        out_specs=pl.BlockSpec((TME * ROW_TILE, LANES), lambda i, te, tv: (i, 0)),
        scratch_shapes=[pltpu.VMEM((TME * ROW_TILE, LANES), F32), pltpu.VMEM((TME * ROW_TILE, LANES), F32),
                        pltpu.SemaphoreType.DMA((2,))],
    )
    return pl.pallas_call(
        _experts_kernel,
        grid_spec=grid_spec,
        out_shape=jax.ShapeDtypeStruct((n_tiles * TME * ROW_TILE, LANES), F32),
        compiler_params=_cp("arbitrary"),
        name="experts",
    )(tile_expert, tile_valid, src, src, h2, wg, wu, wd)


def _combine_kernel(dn_ref, ye_ref, wt_ref, h_ref, x_ref, mod_ref, wsg_ref, wsu_ref, wsd_ref, ln_ref,
                    o_ref, ybuf0, ybuf1, acc_ref, sem):
    j = pl.program_id(0)
    n_blocks = pl.num_programs(0) - 1
    n = x_ref.shape[0]
    rows = n * ROW_TILE
    ybuf = (ybuf0, ybuf1)

    def gather_start(s):
        for k in range(TOP_K):
            for r in range(n):
                pltpu.make_async_copy(
                    ye_ref.at[pl.ds(pl.multiple_of(dn_ref[k, r], ROW_TILE), ROW_TILE), :],
                    ybuf[s].at[pl.ds(k * rows + r * ROW_TILE, ROW_TILE), :], sem.at[s]).start()

    def gather_wait(s):
        pltpu.make_async_copy(ye_ref.at[pl.ds(0, TOP_K * rows), :], ybuf[s], sem.at[s]).wait()

    def finish(s):
        for c0 in range(0, n, COMBINE_ROWS):
            wt = wt_ref[pl.ds(c0, COMBINE_ROWS), :]
            part = None
            for k in range(TOP_K):
                term = wt[:, k:k + 1] * _tiles_to_rows(ybuf[s], (), COMBINE_ROWS,
                                                       k * rows + c0 * ROW_TILE)
                part = term if part is None else part + term
            acc_ref[pl.ds(c0, COMBINE_ROWS), :] = part
        h = _tiles_to_rows(h_ref, (), n).astype(BF16)
        a = jnp.dot(h, wsg_ref[...], preferred_element_type=F32)
        b = jnp.dot(h, wsu_ref[...], preferred_element_type=F32)
        acc = acc_ref[...] + jnp.dot((_silu(a) * b).astype(BF16), wsd_ref[...],
                                     preferred_element_type=F32)
        m = mod_ref[...]
        ln = ln_ref[...]
        o_ref[...] = _ln(ALPHA * x_ref[...] + m[5:6] * acc) * ln[0:1] + ln[1:2]

    def step(s):
        @pl.when((j >= 1) & (j < n_blocks))
        def _():
            gather_wait(1 - s)
            gather_start(s)
            finish(1 - s)

        @pl.when(j == 0)
        def _():
            gather_start(s)

        @pl.when(j == n_blocks)
        def _():
            gather_wait(1 - s)
            finish(1 - s)

    for s in range(2):
        pl.when(j % 2 == s)(functools.partial(step, s))


def _combine(ye, dest, wt, h2, xs, mod, wsg, wsu, wsd, ln, t_ctx, seq):
    t, d = xs.shape
    full = lambda a: pl.BlockSpec(a.shape, lambda j: (0, 0))
    n_blocks = t // TMC
    prev = lambda j: jnp.maximum(j - 1, 0)
    return pl.pallas_call(
        _combine_kernel,
        grid=(n_blocks + 1,),
        in_specs=[pl.BlockSpec((None, TOP_K, TMC), lambda j: (jnp.minimum(j, n_blocks - 1), 0, 0),
                               memory_space=pltpu.SMEM),
                  pl.BlockSpec(memory_space=pl.ANY),
                  pl.BlockSpec((TMC, TOP_K), lambda j: (prev(j), 0)),
                  pl.BlockSpec((TMC * ROW_TILE, LANES), lambda j: (prev(j), 0)),
                  pl.BlockSpec((TMC, d), lambda j: (prev(j), 0)),
                  pl.BlockSpec((None, 6, d), lambda j: (_mod_row(prev(j), t_ctx, seq, TMC), 0, 0)),
                  full(wsg), full(wsu), full(wsd), full(ln)],
        out_specs=pl.BlockSpec((TMC, d), lambda j: (prev(j), 0)),
        out_shape=jax.ShapeDtypeStruct((t, d), F32),
        scratch_shapes=[pltpu.VMEM((TOP_K * TMC * ROW_TILE, LANES), F32),
                        pltpu.VMEM((TOP_K * TMC * ROW_TILE, LANES), F32),
                        pltpu.VMEM((TMC, d), F32), pltpu.SemaphoreType.DMA((2,))],
        compiler_params=_cp("arbitrary"),
        name="combine",
    )(dest, ye, wt, h2, xs, mod, wsg, wsu, wsd, ln)


def _moe(xs, mod, wr_t, rb, wg, wu, wd, wsg, wsu, wsd, ln, t_ctx, seq):
    t, d = xs.shape
    h2, idx, wt, rank, cnt = _router(xs, mod, wr_t, rb, t_ctx, seq)
    n_pairs = t * TOP_K
    n_rows = n_pairs + N_EXPERTS * TME
    n_tiles = n_rows // TME
    assert n_rows < (1 << KEY_SHIFT) and n_pairs % TME == 0
    counts = cnt[:, 0].astype(jnp.int32)
    n_padding = (-counts) % TME
    ends = jnp.cumsum(counts + n_padding)
    starts = ends - counts - n_padding
    experts = jnp.arange(N_EXPERTS, dtype=jnp.int32)
    slot = jnp.arange(TME, dtype=jnp.int32)[None, :]
    pad_key = jnp.where(slot < n_padding[:, None], experts[:, None], N_EXPERTS).reshape(-1)
    keys = (jnp.concatenate([idx.T.reshape(-1), pad_key]) * (1 << KEY_SHIFT)
            + jnp.arange(n_rows, dtype=jnp.int32))
    order = lax.sort(keys)
    pair = order & ((1 << KEY_SHIFT) - 1)
    src = (jnp.minimum(pair // TOP_K, t - 1) * ROW_TILE).reshape(n_tiles, 1, TME)
    first = order[::TME] >> KEY_SHIFT
    tile_expert = jnp.minimum(first, N_EXPERTS - 1)
    tile_valid = (first < N_EXPERTS).astype(jnp.int32)
    ye = _experts(tile_expert, tile_valid, src, h2, wg, wu, wd)
    row_start = jnp.sum(jnp.where(idx[:, :, None] == experts, starts, 0), axis=-1)
    dest = ((row_start + rank) * ROW_TILE).reshape(TOP_K, t // TMC, TMC).transpose(1, 0, 2)
    return _combine(ye, dest, wt.T, h2, xs, mod, wsg, wsu, wsd, ln, t_ctx, seq)


def kernel(x, c, ctx, c_ctx, w_ada, b_ada, w_in, ssm_a_re, ssm_a_im, ssm_log_dt, ssm_b_re, ssm_b_im,
           ssm_c_re, ssm_c_im, ssm_d, w_glu, b_glu, conv_w, conv_b, conv_ln_g, conv_ln_b, mix_norm_g,
           w_out, ln1_g, ln1_b, w_router, router_bias, we_gate, we_up, we_down, ws_gate, ws_up,
           ws_down, ln2_g, ln2_b):
    bsz, seq, d = x.shape
    clen = ctx.shape[1]
    depth = w_in.shape[0]
    t_ctx, t_lat = bsz * clen, bsz * seq
    assert d == D_MODEL and bsz + 1 <= MOD_ROWS
    assert seq % GRID_W == 0 and clen % GRID_W == 0 and seq % TM == 0 and t_ctx % seq == 0

    xs = jnp.concatenate([ctx.reshape(t_ctx, d), x.reshape(t_lat, d)], axis=0)
    cc = jnp.zeros((MOD_ROWS, d), F32).at[0].set(c_ctx).at[1:bsz + 1].set(c)
    mod_all = _ada(cc, w_ada, b_ada).reshape(depth, MOD_ROWS, 6, d)

    for l in range(depth):
        mod = mod_all[l]
        u, z = _inproj(xs, mod, w_in[l].astype(BF16), t_ctx, seq)

        tm, rm, om, lq, dvec = _s5_mats(ssm_a_re[l], ssm_a_im[l], ssm_log_dt[l], ssm_b_re[l],
                                         ssm_b_im[l], ssm_c_re[l], ssm_c_im[l], ssm_d[l])
        h0 = jnp.zeros((SSM_G, bsz, 4 * SSM_P), F32)
        yc, fin = _scan(_to_chunks(u[:t_ctx], bsz, clen), tm, rm, om, lq, dvec, h0, bsz)
        yl, _ = _scan(_to_chunks(u[t_ctx:], bsz, seq), tm, rm, om, lq, dvec, fin, bsz)
        ys = jnp.concatenate([_from_chunks(yc, bsz, clen), _from_chunks(yl, bsz, seq)], axis=0)

        cp = jnp.stack([conv_b[l], conv_ln_g[l], conv_ln_b[l], mix_norm_g[l, D_SSM:]])
        zc = jnp.concatenate([_conv_ctx(z, conv_w[l], cp, bsz, clen),
                              _conv_lat(z, conv_w[l], cp, bsz, seq, t_ctx // seq)], axis=0)

        pv = jnp.stack([b_glu[l], mix_norm_g[l, :D_SSM]])
        xs = _outproj(ys, zc, xs, mod, w_glu[l].astype(BF16), pv, w_out[l].astype(BF16),
                      jnp.stack([ln1_g[l], ln1_b[l]]), t_ctx, seq)

        xs = _moe(xs, mod, w_router[l].T, router_bias[l][:, None],
                  we_gate[l].astype(BF16), we_up[l].astype(BF16), we_down[l].astype(BF16),
                  ws_gate[l].astype(BF16), ws_up[l].astype(BF16), ws_down[l].astype(BF16),
                  jnp.stack([ln2_g[l], ln2_b[l]]), t_ctx, seq)

    return xs[t_ctx:].reshape(bsz, seq, d)
```

```python
import functools
import math

import jax
import jax.numpy as jnp
from jax import lax
from jax.experimental import pallas as pl
from jax.experimental.pallas import tpu as pltpu

F32 = jnp.float32
BF16 = jnp.bfloat16

D_MODEL = 1024
DEPTH = 4
GRID_W = 64
D_SSM = 512
D_CONV = 512
SSM_H = 16
SSM_G = 32
SSM_P = 64
CONV_K = 31
CONV_PAD = 15
CONV_WIN = GRID_W + 32
N_EXPERTS = 64
TOP_K = 8
N_GROUPS = 8
TOP_K_GROUPS = 4
GROUP_SIZE = N_EXPERTS // N_GROUPS
D_EXPERT = 256
ROUTED_SCALE = 2.5
ALPHA = (2 * DEPTH) ** 0.25
LN_EPS = 1e-5

LANES = 128
CHUNK = 16
CW = CHUNK * SSM_H
LANE_TILES = D_SSM // LANES
GROUPS_PER_TILE = LANES // SSM_H
MOD_ROWS = 32
TM = 256
TMB = 512
TME = 512
KEY_SHIFT = 19
VMEM_LIMIT = 48 * 1024 * 1024


def _cp(*sem):
    return pltpu.CompilerParams(dimension_semantics=sem, vmem_limit_bytes=VMEM_LIMIT)


def _ln(x):
    mu = jnp.mean(x, axis=-1, keepdims=True)
    xc = x - mu
    var = jnp.mean(xc * xc, axis=-1, keepdims=True)
    return xc * lax.rsqrt(var + LN_EPS)


def _silu(x):
    return x * jax.nn.sigmoid(x)


def _ada_kernel(c_ref, w_ref, b_ref, o_ref):
    a = _silu(c_ref[...]).astype(BF16)
    o_ref[...] = jnp.dot(a, w_ref[...].astype(BF16), preferred_element_type=F32) + b_ref[...]


def _ada(cc, w_ada, b_ada):
    depth, d, n = w_ada.shape
    tn = 1536
    return pl.pallas_call(
        _ada_kernel,
        grid=(depth, n // tn),
        in_specs=[pl.BlockSpec((MOD_ROWS, d), lambda l, j: (0, 0)),
                  pl.BlockSpec((None, d, tn), lambda l, j: (l, 0, j)),
                  pl.BlockSpec((None, 1, tn), lambda l, j: (l, 0, j))],
        out_specs=pl.BlockSpec((None, MOD_ROWS, tn), lambda l, j: (l, 0, j)),
        out_shape=jax.ShapeDtypeStruct((depth, MOD_ROWS, n), F32),
        compiler_params=_cp("arbitrary", "arbitrary"),
        name="ada",
    )(cc, w_ada, b_ada.reshape(depth, 1, n))


def _inproj_kernel(x_ref, mod_ref, w_ref, u_ref, z_ref):
    m = mod_ref[...]
    h = _ln(x_ref[...]) * (1.0 + m[1:2]) + m[0:1]
    p = jnp.dot(h.astype(BF16), w_ref[...], preferred_element_type=F32)
    u_ref[...] = p[:, :D_SSM].astype(BF16)
    v = p[:, D_SSM:D_SSM + D_CONV]
    g = p[:, D_SSM + D_CONV:]
    z_ref[...] = (v * jax.nn.sigmoid(g)).astype(BF16)


def _mod_row(i, t_ctx, seq, tile):
    start = i * tile
    return jnp.where(start < t_ctx, 0, 1 + (start - t_ctx) // seq)


def _inproj(xs, mod, w, t_ctx, seq):
    t, d = xs.shape
    return pl.pallas_call(
        _inproj_kernel,
        grid=(t // TMB,),
        in_specs=[pl.BlockSpec((TMB, d), lambda i: (i, 0)),
                  pl.BlockSpec((None, 6, d), lambda i: (_mod_row(i, t_ctx, seq, TMB), 0, 0)),
                  pl.BlockSpec(w.shape, lambda i: (0, 0))],
        out_specs=[pl.BlockSpec((TMB, D_SSM), lambda i: (i, 0)),
                   pl.BlockSpec((TMB, D_CONV), lambda i: (i, 0))],
        out_shape=[jax.ShapeDtypeStruct((t, D_SSM), BF16),
                   jax.ShapeDtypeStruct((t, D_CONV), BF16)],
        compiler_params=_cp("arbitrary"),
        name="inproj",
    )(xs, mod, w)


def _s5_mats(a_re, a_im, log_dt, b_re, b_im, c_re, c_im, d_skip):
    hp = lax.Precision.HIGHEST
    q = CHUNK
    dt = jnp.exp(log_dt)[..., None]
    ldr, ldi = a_re * dt, a_im * dt
    ks = jnp.arange(q + 1, dtype=F32)[:, None, None, None]
    mag = jnp.exp(ks * ldr)
    pwr, pwi = mag * jnp.cos(ks * ldi), mag * jnp.sin(ks * ldi)
    nr, ni = pwr[1] - 1.0, pwi[1]
    den = a_re * a_re + a_im * a_im
    qr, qi = (nr * a_re + ni * a_im) / den, (ni * a_re - nr * a_im) / den
    bbr = qr[..., None] * b_re - qi[..., None] * b_im
    bbi = qr[..., None] * b_im + qi[..., None] * b_re
    mr = c_re[None] * pwr[:, :, :, None, :] - c_im[None] * pwi[:, :, :, None, :]
    mi = c_re[None] * pwi[:, :, :, None, :] + c_im[None] * pwr[:, :, :, None, :]
    kern = (jnp.einsum('kdgop,dgph->kdgoh', mr, bbr, precision=hp)
            - jnp.einsum('kdgop,dgph->kdgoh', mi, bbi, precision=hp))
    s_i = jnp.arange(q)[:, None]
    t_i = jnp.arange(q)[None, :]
    lag = t_i - s_i
    tf = jnp.where((lag >= 0)[:, :, None, None, None], kern[:q, 0][jnp.clip(lag, 0, q - 1)], 0.0)
    tb = jnp.where((lag <= 0)[:, :, None, None, None], kern[:q, 1][jnp.clip(-lag, 0, q - 1)], 0.0)
    tm = (tf + tb).transpose(2, 0, 4, 1, 3).reshape(SSM_G, CW, CW)

    def state_in(pr, pi, d):
        re = pr[..., None] * bbr[d][None] - pi[..., None] * bbi[d][None]
        im = pr[..., None] * bbi[d][None] + pi[..., None] * bbr[d][None]
        fix = lambda a: a.transpose(1, 0, 3, 2).reshape(SSM_G, CW, SSM_P)
        return fix(re), fix(im)

    f_re, f_im = state_in(pwr[:q, 0][::-1], pwi[:q, 0][::-1], 0)
    g_re, g_im = state_in(pwr[:q, 1], pwi[:q, 1], 1)
    rm = jnp.concatenate([f_re, g_re, f_im, g_im], axis=-1)

    fix_o = lambda a: a.transpose(1, 3, 0, 2).reshape(SSM_G, SSM_P, CW)
    om = jnp.concatenate([fix_o(mr[1:, 0]), fix_o(mr[1:, 1][::-1]),
                          fix_o(-mi[1:, 0]), fix_o(-mi[1:, 1][::-1])], axis=1)
    lq = jnp.stack([jnp.concatenate([pwr[q, 0], pwr[q, 1]], -1),
                    jnp.concatenate([pwi[q, 0], pwi[q, 1]], -1)], axis=1)
    dvec = jnp.tile(d_skip[:, None, :], (1, q, 1)).reshape(SSM_G, 1, CW)
    return tm.astype(BF16), rm.astype(BF16), om.astype(BF16), lq, dvec


def _scan_kernel(x_ref, t_ref, r_ref, o_ref, lq_ref, d_ref, h0_ref, y_ref, fin_ref,
                 rr_ref, st_ref, *, n_chunks, bsz):
    p2 = 2 * SSM_P
    x = x_ref[...]
    rr_ref[...] = jnp.dot(x, r_ref[...], preferred_element_type=F32)
    lq = lq_ref[...]
    lre, lim = lq[0:1], lq[1:2]
    is_f = lax.broadcasted_iota(jnp.int32, (bsz, p2), 1) < SSM_P
    h0 = h0_ref[...]

    def step(j, carry):
        sre, sim = carry
        rf = pl.multiple_of(j * bsz, bsz)
        rb = pl.multiple_of((n_chunks - 1 - j) * bsz, bsz)
        st_ref[pl.ds(rf, bsz), 0:SSM_P] = sre[:, 0:SSM_P]
        st_ref[pl.ds(rb, bsz), SSM_P:p2] = sre[:, SSM_P:p2]
        st_ref[pl.ds(rf, bsz), p2:p2 + SSM_P] = sim[:, 0:SSM_P]
        st_ref[pl.ds(rb, bsz), p2 + SSM_P:2 * p2] = sim[:, SSM_P:p2]
        r_re = jnp.where(is_f, rr_ref[pl.ds(rf, bsz), 0:p2], rr_ref[pl.ds(rb, bsz), 0:p2])
        r_im = jnp.where(is_f, rr_ref[pl.ds(rf, bsz), p2:2 * p2], rr_ref[pl.ds(rb, bsz), p2:2 * p2])
        return lre * sre - lim * sim + r_re, lre * sim + lim * sre + r_im

    sre, sim = lax.fori_loop(0, n_chunks, step, (h0[:, 0:p2], h0[:, p2:2 * p2]))
    fin_ref[...] = jnp.concatenate([sre, sim], axis=1)
    y = jnp.dot(x, t_ref[...], preferred_element_type=F32)
    y = y + jnp.dot(st_ref[...].astype(BF16), o_ref[...], preferred_element_type=F32)
    y = y + d_ref[...] * x.astype(F32)
    y_ref[...] = jax.nn.gelu(y, approximate=True).astype(BF16)


def _scan(xg, tm, rm, om, lq, dvec, h0, bsz):
    g, n, _ = xg.shape
    n_chunks = n // bsz
    mat = pl.BlockSpec((None, CW, CW), lambda i: (i, 0, 0))
    return pl.pallas_call(
        functools.partial(_scan_kernel, n_chunks=n_chunks, bsz=bsz),
        grid=(g,),
        in_specs=[pl.BlockSpec((None, n, CW), lambda i: (i, 0, 0)), mat, mat, mat,
                  pl.BlockSpec((None, 2, 2 * SSM_P), lambda i: (i, 0, 0)),
                  pl.BlockSpec((None, 1, CW), lambda i: (i, 0, 0)),
                  pl.BlockSpec((None, bsz, 4 * SSM_P), lambda i: (i, 0, 0))],
        out_specs=[pl.BlockSpec((None, n, CW), lambda i: (i, 0, 0)),
                   pl.BlockSpec((None, bsz, 4 * SSM_P), lambda i: (i, 0, 0))],
        out_shape=[jax.ShapeDtypeStruct((g, n, CW), BF16),
                   jax.ShapeDtypeStruct((g, bsz, 4 * SSM_P), F32)],
        scratch_shapes=[pltpu.VMEM((n, 4 * SSM_P), F32), pltpu.VMEM((n, 4 * SSM_P), F32)],
        compiler_params=_cp("arbitrary"),
        name="s5_scan",
    )(xg, tm, rm, om, lq, dvec, h0)


def _chunk_perm():
    n = CHUNK * LANES
    src = (jnp.arange(n, dtype=jnp.int32).reshape(CHUNK, GROUPS_PER_TILE, SSM_H)
           .transpose(1, 0, 2).reshape(-1))
    return (jnp.arange(n, dtype=jnp.int32)[:, None] == src[None, :]).astype(BF16)


def _to_chunks(u, bsz, length, perm):
    nc = length // CHUNK
    z = (u.reshape(bsz, nc, CHUNK, LANE_TILES, LANES).transpose(3, 1, 0, 2, 4)
         .reshape(LANE_TILES, nc * bsz, CHUNK * LANES))
    w = jnp.einsum('jnk,kc->jnc', z, perm, preferred_element_type=BF16)
    return (w.reshape(LANE_TILES, nc * bsz, GROUPS_PER_TILE, CW).transpose(0, 2, 1, 3)
            .reshape(SSM_G, nc * bsz, CW))


def _from_chunks(y, bsz, length, perm):
    nc = length // CHUNK
    w = (y.reshape(LANE_TILES, GROUPS_PER_TILE, nc * bsz, CW).transpose(0, 2, 1, 3)
         .reshape(LANE_TILES, nc * bsz, CHUNK * LANES))
    z = jnp.einsum('jnc,kc->jnk', w, perm, preferred_element_type=BF16)
    return (z.reshape(LANE_TILES, nc, bsz, CHUNK, LANES).transpose(2, 1, 3, 0, 4)
            .reshape(bsz * length, D_SSM))


def _conv_post(cv, p_ref):
    p = p_ref[...]
    cv = cv + p[0:1]
    y = _silu(_ln(cv) * p[1:2] + p[2:3])
    return y * lax.rsqrt(jnp.mean(y * y, axis=-1, keepdims=True) + LN_EPS) * p[3:4]


def _conv_seq(buf_ref, base, w_ref, c0):
    half = D_CONV // 2
    win = buf_ref[pl.ds(base, CONV_WIN), c0:c0 + half]
    acc = jnp.zeros((GRID_W, half), F32)
    for ph in range(8):
        wb = win if ph == 0 else pltpu.roll(win, CONV_WIN - ph, axis=0)
        for a in range(4):
            j = 8 * a + ph - 1
            if 0 <= j < CONV_K:
                acc = acc + w_ref[j:j + 1, c0:c0 + half] * wb[8 * a:8 * a + GRID_W]
    return acc


def _conv_lat_kernel(z_ref, w_ref, p_ref, o_ref, hb_ref, vb_ref, *, rows):
    half = D_CONV // 2
    hstride = GRID_W + 16
    hb_ref[...] = jnp.zeros(hb_ref.shape, F32)
    vb_ref[pl.ds(0, CONV_PAD * GRID_W), :] = jnp.zeros((CONV_PAD * GRID_W, half), F32)
    vb_ref[pl.ds((CONV_PAD + rows) * GRID_W, CONV_PAD * GRID_W), :] = jnp.zeros((CONV_PAD * GRID_W, half), F32)
    for r in range(rows):
        hb_ref[pl.ds(16 + r * hstride, GRID_W), :] = z_ref[pl.ds(r * GRID_W, GRID_W), 0:half].astype(F32)
    vb_ref[pl.ds(CONV_PAD * GRID_W, rows * GRID_W), :] = z_ref[:, half:D_CONV].astype(F32)

    def row(r, carry):
        hbase = pl.multiple_of(r * hstride, 16)
        vbase = pl.multiple_of(r * GRID_W, GRID_W)
        acc_h = _conv_seq(hb_ref, hbase, w_ref, 0)
        acc_v = jnp.zeros((GRID_W, half), F32)
        for j in range(CONV_K):
            acc_v = acc_v + w_ref[j:j + 1, half:D_CONV] * vb_ref[pl.ds(vbase + j * GRID_W, GRID_W), :]
        out = _conv_post(jnp.concatenate([acc_h, acc_v], axis=1), p_ref)
        o_ref[pl.ds(vbase, GRID_W), :] = out.astype(BF16)
        return carry

    lax.fori_loop(0, rows, row, 0)


def _conv_lat(z, w, p, bsz, seq, blk0):
    rows = seq // GRID_W
    half = D_CONV // 2
    return pl.pallas_call(
        functools.partial(_conv_lat_kernel, rows=rows),
        grid=(bsz,),
        in_specs=[pl.BlockSpec((seq, D_CONV), lambda b: (blk0 + b, 0)),
                  pl.BlockSpec(w.shape, lambda b: (0, 0)),
                  pl.BlockSpec(p.shape, lambda b: (0, 0))],
        out_specs=pl.BlockSpec((seq, D_CONV), lambda b: (b, 0)),
        out_shape=jax.ShapeDtypeStruct((bsz * seq, D_CONV), BF16),
        scratch_shapes=[pltpu.VMEM((rows * (GRID_W + 16) + 16, half), F32),
                        pltpu.VMEM(((rows + 2 * CONV_PAD) * GRID_W, half), F32)],
        compiler_params=_cp("arbitrary"),
        name="conv_latent",
    )(z, w, p)


def _conv_ctx_kernel(z_ref, w_ref, p_ref, o_ref, cb_ref, *, clen):
    cb_ref[pl.ds(0, 16), :] = jnp.zeros((16, D_CONV), F32)
    cb_ref[pl.ds(16 + clen, 16), :] = jnp.zeros((16, D_CONV), F32)
    cb_ref[pl.ds(16, clen), :] = z_ref[...].astype(F32)
    for blk in range(clen // GRID_W):
        base = blk * GRID_W
        acc = jnp.concatenate([_conv_seq(cb_ref, base, w_ref, 0),
                               _conv_seq(cb_ref, base, w_ref, D_CONV // 2)], axis=1)
        o_ref[pl.ds(base, GRID_W), :] = _conv_post(acc, p_ref).astype(BF16)


def _conv_ctx(z, w, p, bsz, clen):
    return pl.pallas_call(
        functools.partial(_conv_ctx_kernel, clen=clen),
        grid=(bsz,),
        in_specs=[pl.BlockSpec((clen, D_CONV), lambda b: (b, 0)),
                  pl.BlockSpec(w.shape, lambda b: (0, 0)),
                  pl.BlockSpec(p.shape, lambda b: (0, 0))],
        out_specs=pl.BlockSpec((clen, D_CONV), lambda b: (b, 0)),
        out_shape=jax.ShapeDtypeStruct((bsz * clen, D_CONV), BF16),
        scratch_shapes=[pltpu.VMEM((clen + 32, D_CONV), F32)],
        compiler_params=_cp("arbitrary"),
        name="conv_context",
    )(z, w, p)


def _outproj_kernel(ys_ref, yc_ref, x_ref, mod_ref, wglu_ref, pv_ref, wo_ref, ln_ref, o_ref):
    ys = ys_ref[...]
    pv = pv_ref[...]
    gl = jnp.dot(ys, wglu_ref[...], preferred_element_type=F32) + pv[0:1]
    yg = ys.astype(F32) * jax.nn.sigmoid(gl)
    yn = yg * lax.rsqrt(jnp.mean(yg * yg, axis=-1, keepdims=True) + LN_EPS) * pv[1:2]
    y = jnp.dot(yn.astype(BF16), wo_ref[0:D_SSM, :], preferred_element_type=F32)
    y = y + jnp.dot(yc_ref[...], wo_ref[D_SSM:, :], preferred_element_type=F32)
    m = mod_ref[...]
    ln = ln_ref[...]
    o_ref[...] = _ln(ALPHA * x_ref[...] + m[2:3] * y) * ln[0:1] + ln[1:2]


def _outproj(ys, yc, xs, mod, wglu, pv, wo, ln, t_ctx, seq):
    t, d = xs.shape
    full = lambda a: pl.BlockSpec(a.shape, lambda i: (0, 0))
    return pl.pallas_call(
        _outproj_kernel,
        grid=(t // TMB,),
        in_specs=[pl.BlockSpec((TMB, D_SSM), lambda i: (i, 0)),
                  pl.BlockSpec((TMB, D_CONV), lambda i: (i, 0)),
                  pl.BlockSpec((TMB, d), lambda i: (i, 0)),
                  pl.BlockSpec((None, 6, d), lambda i: (_mod_row(i, t_ctx, seq, TMB), 0, 0)),
                  full(wglu), full(pv), full(wo), full(ln)],
        out_specs=pl.BlockSpec((TMB, d), lambda i: (i, 0)),
        out_shape=jax.ShapeDtypeStruct((t, d), F32),
        compiler_params=_cp("arbitrary"),
        name="outproj",
    )(ys, yc, xs, mod, wglu, pv, wo, ln)


ROW_TILE = 8


def _rows_to_tiles(ref, lead, val):
    n = val.shape[0]
    for c in range(ROW_TILE):
        ref[lead + (pl.ds(c, n, stride=ROW_TILE), slice(None))] = val[:, c * LANES:(c + 1) * LANES]


def _tiles_to_rows(ref, lead, n):
    return jnp.concatenate([ref[lead + (pl.ds(c, n, stride=ROW_TILE), slice(None))]
                            for c in range(ROW_TILE)], axis=1)


def _router_kernel(x_ref, mod_ref, wr_ref, rb_ref, h_ref, idx_ref, wt_ref, cnt_ref, carry_ref):
    @pl.when(pl.program_id(0) == 0)
    def _():
        carry_ref[...] = jnp.zeros(carry_ref.shape, F32)

    m = mod_ref[...]
    h = _ln(x_ref[...]) * (1.0 + m[4:5]) + m[3:4]
    _rows_to_tiles(h_ref, (), h)
    logits = lax.dot_general(wr_ref[...], h, (((1,), (1,)), ((), ())),
                             precision=lax.Precision.HIGHEST, preferred_element_type=F32)
    s = jax.nn.sigmoid(logits)
    biased = s + rb_ref[...]
    ninf = -jnp.inf
    tm = s.shape[1]

    row8 = lax.broadcasted_iota(jnp.int32, (GROUP_SIZE, tm), 0).astype(F32)
    gs = []
    for q in range(N_GROUPS):
        v = biased[q * GROUP_SIZE:(q + 1) * GROUP_SIZE]
        m1 = jnp.max(v, axis=0, keepdims=True)
        i1 = jnp.min(jnp.where(v == m1, row8, float(GROUP_SIZE)), axis=0, keepdims=True)
        m2 = jnp.max(jnp.where(row8 == i1, ninf, v), axis=0, keepdims=True)
        gs.append(m1 + m2)
    gsc = jnp.concatenate(gs, axis=0)
    rowg = lax.broadcasted_iota(jnp.int32, (N_GROUPS, tm), 0).astype(F32)
    gsel = jnp.zeros((N_GROUPS, tm), F32)
    for _ in range(TOP_K_GROUPS):
        mx = jnp.max(gsc, axis=0, keepdims=True)
        ii = jnp.min(jnp.where(gsc == mx, rowg, float(N_GROUPS)), axis=0, keepdims=True)
        hit = rowg == ii
        gsel = jnp.where(hit, 1.0, gsel)
        gsc = jnp.where(hit, ninf, gsc)
    emask = jnp.concatenate([jnp.broadcast_to(gsel[q:q + 1], (GROUP_SIZE, tm))
                             for q in range(N_GROUPS)], axis=0)
    masked = jnp.where(emask > 0.0, biased, ninf)

    rowe = lax.broadcasted_iota(jnp.int32, (N_EXPERTS, tm), 0).astype(F32)
    selm = jnp.zeros((N_EXPERTS, tm), F32)
    idxs, ws = [], []
    for _ in range(TOP_K):
        mx = jnp.max(masked, axis=0, keepdims=True)
        ii = jnp.min(jnp.where(masked == mx, rowe, float(N_EXPERTS)), axis=0, keepdims=True)
        hit = rowe == ii
        idxs.append(ii)
        ws.append(jnp.sum(jnp.where(hit, s, 0.0), axis=0, keepdims=True))
        selm = jnp.where(hit, 1.0, selm)
        masked = jnp.where(hit, ninf, masked)
    wsum = ws[0]
    for k in range(1, TOP_K):
        wsum = wsum + ws[k]
    idx_ref[...] = jnp.concatenate(idxs, axis=0).astype(jnp.int32)
    wt_ref[...] = jnp.concatenate([w / wsum * ROUTED_SCALE for w in ws], axis=0)

    new_carry = carry_ref[...] + jnp.sum(selm, axis=1, keepdims=True)
    carry_ref[...] = new_carry
    cnt_ref[...] = new_carry


def _router(xs, mod, wr_t, rb, t_ctx, seq):
    t, d = xs.shape
    full = lambda a: pl.BlockSpec(a.shape, lambda i: (0, 0))
    return pl.pallas_call(
        _router_kernel,
        grid=(t // TMB,),
        in_specs=[pl.BlockSpec((TMB, d), lambda i: (i, 0)),
                  pl.BlockSpec((None, 6, d), lambda i: (_mod_row(i, t_ctx, seq, TMB), 0, 0)),
                  full(wr_t), full(rb)],
        out_specs=[pl.BlockSpec((TMB * ROW_TILE, LANES), lambda i: (i, 0)),
                   pl.BlockSpec((TOP_K, TMB), lambda i: (0, i)),
                   pl.BlockSpec((TOP_K, TMB), lambda i: (0, i)),
                   pl.BlockSpec((N_EXPERTS, 1), lambda i: (0, 0))],
        out_shape=[jax.ShapeDtypeStruct((t * ROW_TILE, LANES), F32),
                   jax.ShapeDtypeStruct((TOP_K, t), jnp.int32),
                   jax.ShapeDtypeStruct((TOP_K, t), F32),
                   jax.ShapeDtypeStruct((N_EXPERTS, 1), F32)],
        scratch_shapes=[pltpu.VMEM((N_EXPERTS, 1), F32)],
        compiler_params=_cp("arbitrary"),
        name="router",
    )(xs, mod, wr_t, rb)


def _experts_kernel(te_ref, tv_ref, src0_ref, srcn_ref, dst_ref, h_ref, wg_ref, wu_ref, wd_ref,
                    y_ref, xbuf, ybuf, gsem, ssem):
    i = pl.program_id(0)
    valid = tv_ref[i] > 0
    prev_valid = tv_ref[jnp.maximum(i - 1, 0)] > 0
    rows = TME * ROW_TILE

    def gather_start(idx_ref, s):
        for r in range(TME):
            pltpu.make_async_copy(
                h_ref.at[pl.ds(pl.multiple_of(idx_ref[0, r], ROW_TILE), ROW_TILE), :],
                xbuf.at[s, pl.ds(r * ROW_TILE, ROW_TILE), :], gsem.at[s]).start()

    def scatter_start(s):
        for r in range(TME):
            pltpu.make_async_copy(
                ybuf.at[s, pl.ds(r * ROW_TILE, ROW_TILE), :],
                y_ref.at[pl.ds(pl.multiple_of(dst_ref[0, r], ROW_TILE), ROW_TILE), :],
                ssem.at[s]).start()

    def gather_wait(s):
        pltpu.make_async_copy(h_ref.at[pl.ds(0, rows), :], xbuf.at[s], gsem.at[s]).wait()

    def scatter_wait(s):
        pltpu.make_async_copy(ybuf.at[s], y_ref.at[pl.ds(0, rows), :], ssem.at[s]).wait()

    @pl.when(i == 0)
    def _():
        gather_start(src0_ref, 0)

    def step(s):
        @pl.when(valid)
        def _():
            @pl.when(i >= 2)
            def _():
                scatter_wait(s)

            gather_wait(s)
            gather_start(srcn_ref, 1 - s)
            x = _tiles_to_rows(xbuf, (s,), TME).astype(BF16)
            a = jnp.dot(x, wg_ref[...].astype(BF16), preferred_element_type=F32)
            b = jnp.dot(x, wu_ref[...].astype(BF16), preferred_element_type=F32)
            hid = (_silu(a) * b).astype(BF16)
            _rows_to_tiles(ybuf, (s,), jnp.dot(hid, wd_ref[...].astype(BF16), preferred_element_type=F32))
            scatter_start(s)

        @pl.when(jnp.logical_not(valid) & prev_valid & (i >= 1))
        def _():
            gather_wait(s)
            scatter_wait(1 - s)

            @pl.when(i >= 2)
            def _():
                scatter_wait(s)

            dump = pltpu.make_async_copy(ybuf.at[s], y_ref.at[pl.ds(y_ref.shape[0] - rows, rows), :],
                                         ssem.at[s])
            dump.start()
            dump.wait()

    for s in range(2):
        pl.when(i % 2 == s)(functools.partial(step, s))


def _experts(tile_expert, tile_valid, src, dst, h2, wg, wu, wd, n_out):
    n_tiles = src.shape[0]
    d, f = wg.shape[1:]
    smem = lambda fn: pl.BlockSpec((None, 1, TME), fn, memory_space=pltpu.SMEM)
    grid_spec = pltpu.PrefetchScalarGridSpec(
        num_scalar_prefetch=2,
        grid=(n_tiles,),
        in_specs=[smem(lambda i, te, tv: (0, 0, 0)),
                  smem(lambda i, te, tv: (jnp.minimum(i + 1, n_tiles - 1), 0, 0)),
                  smem(lambda i, te, tv: (i, 0, 0)),
                  pl.BlockSpec(memory_space=pl.ANY),
                  pl.BlockSpec((None, d, f), lambda i, te, tv: (te[i], 0, 0)),
                  pl.BlockSpec((None, d, f), lambda i, te, tv: (te[i], 0, 0)),
                  pl.BlockSpec((None, f, d), lambda i, te, tv: (te[i], 0, 0))],
        out_specs=pl.BlockSpec(memory_space=pl.ANY),
        scratch_shapes=[pltpu.VMEM((2, TME * ROW_TILE, LANES), F32),
                        pltpu.VMEM((2, TME * ROW_TILE, LANES), F32),
                        pltpu.SemaphoreType.DMA((2,)), pltpu.SemaphoreType.DMA((2,))],
    )
    return pl.pallas_call(
        _experts_kernel,
        grid_spec=grid_spec,
        out_shape=jax.ShapeDtypeStruct((n_out * ROW_TILE, LANES), F32),
        compiler_params=_cp("arbitrary"),
        name="experts",
    )(tile_expert, tile_valid, src, src, dst, h2, wg, wu, wd)


def _combine_kernel(*refs):
    yg_refs = refs[:TOP_K]
    wt_ref, h_ref, x_ref, mod_ref, wsg_ref, wsu_ref, wsd_ref, ln_ref, o_ref = refs[TOP_K:]
    wt = wt_ref[...]
    n = x_ref.shape[0]
    acc = wt[:, 0:1] * _tiles_to_rows(yg_refs[0], (), n)
    for k in range(1, TOP_K):
        acc = acc + wt[:, k:k + 1] * _tiles_to_rows(yg_refs[k], (), n)
    h = _tiles_to_rows(h_ref, (), n).astype(BF16)
    a = jnp.dot(h, wsg_ref[...], preferred_element_type=F32)
    b = jnp.dot(h, wsu_ref[...], preferred_element_type=F32)
    acc = acc + jnp.dot((_silu(a) * b).astype(BF16), wsd_ref[...], preferred_element_type=F32)
    m = mod_ref[...]
    ln = ln_ref[...]
    o_ref[...] = _ln(ALPHA * x_ref[...] + m[5:6] * acc) * ln[0:1] + ln[1:2]


def _combine(yg, wt, h2, xs, mod, wsg, wsu, wsd, ln, t_ctx, seq):
    t, d = xs.shape
    full = lambda a: pl.BlockSpec(a.shape, lambda i: (0, 0))
    n_blocks = t // TM
    slot_specs = [pl.BlockSpec((TM * ROW_TILE, LANES), functools.partial(lambda i, k: (k * n_blocks + i, 0), k=k))
                  for k in range(TOP_K)]
    return pl.pallas_call(
        _combine_kernel,
        grid=(n_blocks,),
        in_specs=slot_specs + [
                  pl.BlockSpec((TM, TOP_K), lambda i: (i, 0)),
                  pl.BlockSpec((TM * ROW_TILE, LANES), lambda i: (i, 0)),
                  pl.BlockSpec((TM, d), lambda i: (i, 0)),
                  pl.BlockSpec((None, 6, d), lambda i: (_mod_row(i, t_ctx, seq, TM), 0, 0)),
                  full(wsg), full(wsu), full(wsd), full(ln)],
        out_specs=pl.BlockSpec((TM, d), lambda i: (i, 0)),
        out_shape=jax.ShapeDtypeStruct((t, d), F32),
        compiler_params=_cp("arbitrary"),
        name="combine",
    )(*([yg] * TOP_K), wt, h2, xs, mod, wsg, wsu, wsd, ln)


def _moe(xs, mod, wr_t, rb, wg, wu, wd, wsg, wsu, wsd, ln, t_ctx, seq):
    t, d = xs.shape
    h2, idx, wt, cnt = _router(xs, mod, wr_t, rb, t_ctx, seq)
    n_pairs = t * TOP_K
    n_rows = n_pairs + N_EXPERTS * TME
    n_tiles = n_rows // TME
    assert n_rows < (1 << KEY_SHIFT) and n_pairs % TME == 0
    counts = cnt[:, 0].astype(jnp.int32)
    n_padding = (-counts) % TME
    slot = jnp.arange(TME, dtype=jnp.int32)[None, :]
    pad_key = jnp.where(slot < n_padding[:, None], jnp.arange(N_EXPERTS, dtype=jnp.int32)[:, None],
                        N_EXPERTS).reshape(-1)
    keys = jnp.concatenate([idx.reshape(-1), pad_key]) * (1 << KEY_SHIFT) + jnp.arange(n_rows, dtype=jnp.int32)
    order = lax.sort(keys)
    pair = order & ((1 << KEY_SHIFT) - 1)
    src = ((pair % t) * ROW_TILE).reshape(n_tiles, 1, TME)
    dst = (jnp.where(pair < n_pairs, pair, n_pairs + pair % TME) * ROW_TILE).reshape(n_tiles, 1, TME)
    first = order[::TME] >> KEY_SHIFT
    tile_expert = jnp.minimum(first, N_EXPERTS - 1)
    tile_valid = (first < N_EXPERTS).astype(jnp.int32)
    ye = _experts(tile_expert, tile_valid, src, dst, h2, wg, wu, wd, n_pairs + TME)
    return _combine(ye, wt.T, h2, xs, mod, wsg, wsu, wsd, ln, t_ctx, seq)


def kernel(x, c, ctx, c_ctx, w_ada, b_ada, w_in, ssm_a_re, ssm_a_im, ssm_log_dt, ssm_b_re, ssm_b_im,
           ssm_c_re, ssm_c_im, ssm_d, w_glu, b_glu, conv_w, conv_b, conv_ln_g, conv_ln_b, mix_norm_g,
           w_out, ln1_g, ln1_b, w_router, router_bias, we_gate, we_up, we_down, ws_gate, ws_up,
           ws_down, ln2_g, ln2_b):
    bsz, seq, d = x.shape
    clen = ctx.shape[1]
    depth = w_in.shape[0]
    t_ctx, t_lat = bsz * clen, bsz * seq
    assert d == D_MODEL and bsz + 1 <= MOD_ROWS
    assert seq % GRID_W == 0 and clen % GRID_W == 0 and t_ctx % seq == 0
    assert seq % TMB == 0 and t_ctx % TMB == 0 and seq % TM == 0 and t_ctx % TM == 0

    xs = jnp.concatenate([ctx.reshape(t_ctx, d), x.reshape(t_lat, d)], axis=0)
    cc = jnp.zeros((MOD_ROWS, d), F32).at[0].set(c_ctx).at[1:bsz + 1].set(c)
    mod_all = _ada(cc, w_ada, b_ada).reshape(depth, MOD_ROWS, 6, d)
    perm = _chunk_perm()

    for l in range(depth):
        mod = mod_all[l]
        u, z = _inproj(xs, mod, w_in[l].astype(BF16), t_ctx, seq)

        tm, rm, om, lq, dvec = _s5_mats(ssm_a_re[l], ssm_a_im[l], ssm_log_dt[l], ssm_b_re[l],
                                         ssm_b_im[l], ssm_c_re[l], ssm_c_im[l], ssm_d[l])
        h0 = jnp.zeros((SSM_G, bsz, 4 * SSM_P), F32)
        yc, fin = _scan(_to_chunks(u[:t_ctx], bsz, clen, perm), tm, rm, om, lq, dvec, h0, bsz)
        yl, _ = _scan(_to_chunks(u[t_ctx:], bsz, seq, perm), tm, rm, om, lq, dvec, fin, bsz)
        ys = jnp.concatenate([_from_chunks(yc, bsz, clen, perm), _from_chunks(yl, bsz, seq, perm)], axis=0)

        cp = jnp.stack([conv_b[l], conv_ln_g[l], conv_ln_b[l], mix_norm_g[l, D_SSM:]])
        zc = jnp.concatenate([_conv_ctx(z, conv_w[l], cp, bsz, clen),
                              _conv_lat(z, conv_w[l], cp, bsz, seq, t_ctx // seq)], axis=0)

        pv = jnp.stack([b_glu[l], mix_norm_g[l, :D_SSM]])
        xs = _outproj(ys, zc, xs, mod, w_glu[l].astype(BF16), pv, w_out[l].astype(BF16),
                      jnp.stack([ln1_g[l], ln1_b[l]]), t_ctx, seq)

        xs = _moe(xs, mod, w_router[l].T, router_bias[l][:, None], we_gate[l], we_up[l], we_down[l],
                  ws_gate[l].astype(BF16), ws_up[l].astype(BF16), ws_down[l].astype(BF16),
                  jnp.stack([ln2_g[l], ln2_b[l]]), t_ctx, seq)

    return xs[t_ctx:].reshape(bsz, seq, d)
```

```python
import functools
import math

import jax
import jax.numpy as jnp
from jax import lax
from jax.experimental import pallas as pl
from jax.experimental.pallas import tpu as pltpu

F32 = jnp.float32
BF16 = jnp.bfloat16

D_MODEL = 1024
DEPTH = 4
GRID_W = 64
D_SSM = 512
D_CONV = 512
SSM_H = 16
SSM_G = 32
SSM_P = 64
CONV_K = 31
CONV_PAD = 15
CONV_WIN = GRID_W + 32
N_EXPERTS = 64
TOP_K = 8
N_GROUPS = 8
TOP_K_GROUPS = 4
GROUP_SIZE = N_EXPERTS // N_GROUPS
D_EXPERT = 256
ROUTED_SCALE = 2.5
ALPHA = (2 * DEPTH) ** 0.25
LN_EPS = 1e-5

LANES = 128
CHUNK = 16
CW = CHUNK * SSM_H
LANE_TILES = D_SSM // LANES
GROUPS_PER_TILE = LANES // SSM_H
MOD_ROWS = 32
TM = 256
TMB = 512
TME = 512
KEY_SHIFT = 19
VMEM_LIMIT = 48 * 1024 * 1024


def _cp(*sem):
    return pltpu.CompilerParams(dimension_semantics=sem, vmem_limit_bytes=VMEM_LIMIT)


def _ln(x):
    mu = jnp.mean(x, axis=-1, keepdims=True)
    xc = x - mu
    var = jnp.mean(xc * xc, axis=-1, keepdims=True)
    return xc * lax.rsqrt(var + LN_EPS)


def _silu(x):
    return x * jax.nn.sigmoid(x)


def _ada_kernel(c_ref, w_ref, b_ref, o_ref):
    a = _silu(c_ref[...]).astype(BF16)
    o_ref[...] = jnp.dot(a, w_ref[...].astype(BF16), preferred_element_type=F32) + b_ref[...]


def _ada(cc, w_ada, b_ada):
    depth, d, n = w_ada.shape
    tn = 1536
    return pl.pallas_call(
        _ada_kernel,
        grid=(depth, n // tn),
        in_specs=[pl.BlockSpec((MOD_ROWS, d), lambda l, j: (0, 0)),
                  pl.BlockSpec((None, d, tn), lambda l, j: (l, 0, j)),
                  pl.BlockSpec((None, 1, tn), lambda l, j: (l, 0, j))],
        out_specs=pl.BlockSpec((None, MOD_ROWS, tn), lambda l, j: (l, 0, j)),
        out_shape=jax.ShapeDtypeStruct((depth, MOD_ROWS, n), F32),
        compiler_params=_cp("arbitrary", "arbitrary"),
        name="ada",
    )(cc, w_ada, b_ada.reshape(depth, 1, n))


def _inproj_kernel(x_ref, mod_ref, w_ref, u_ref, z_ref):
    m = mod_ref[...]
    h = _ln(x_ref[...]) * (1.0 + m[1:2]) + m[0:1]
    p = jnp.dot(h.astype(BF16), w_ref[...], preferred_element_type=F32)
    u_ref[...] = p[:, :D_SSM].astype(BF16)
    v = p[:, D_SSM:D_SSM + D_CONV]
    g = p[:, D_SSM + D_CONV:]
    z_ref[...] = (v * jax.nn.sigmoid(g)).astype(BF16)


def _mod_row(i, t_ctx, seq, tile):
    start = i * tile
    return jnp.where(start < t_ctx, 0, 1 + (start - t_ctx) // seq)


def _inproj(xs, mod, w, t_ctx, seq):
    t, d = xs.shape
    return pl.pallas_call(
        _inproj_kernel,
        grid=(t // TMB,),
        in_specs=[pl.BlockSpec((TMB, d), lambda i: (i, 0)),
                  pl.BlockSpec((None, 6, d), lambda i: (_mod_row(i, t_ctx, seq, TMB), 0, 0)),
                  pl.BlockSpec(w.shape, lambda i: (0, 0))],
        out_specs=[pl.BlockSpec((TMB, D_SSM), lambda i: (i, 0)),
                   pl.BlockSpec((TMB, D_CONV), lambda i: (i, 0))],
        out_shape=[jax.ShapeDtypeStruct((t, D_SSM), BF16),
                   jax.ShapeDtypeStruct((t, D_CONV), BF16)],
        compiler_params=_cp("arbitrary"),
        name="inproj",
    )(xs, mod, w)


def _s5_mats(a_re, a_im, log_dt, b_re, b_im, c_re, c_im, d_skip):
    hp = lax.Precision.HIGHEST
    q = CHUNK
    nl = a_re.shape[0]
    dt = jnp.exp(log_dt)[..., None]
    ldr, ldi = a_re * dt, a_im * dt
    ks = jnp.arange(-(q - 1), q + 1, dtype=F32)[:, None]
    mag = jnp.exp(ks * ldr[..., None, :])
    ang = ks * ldi[..., None, :]
    pr, pi = mag * jnp.cos(ang), mag * jnp.sin(ang)

    def powers(d, lo, hi, rev=False):
        r, i = pr[:, d, :, lo + q - 1:hi + q - 1], pi[:, d, :, lo + q - 1:hi + q - 1]
        return (r[:, :, ::-1], i[:, :, ::-1]) if rev else (r, i)

    nr, ni = pr[..., q, :] - 1.0, pi[..., q, :]
    den = a_re * a_re + a_im * a_im
    qr, qi = (nr * a_re + ni * a_im) / den, (ni * a_re - nr * a_im) / den
    bbr = qr[..., None] * b_re - qi[..., None] * b_im
    bbi = qr[..., None] * b_im + qi[..., None] * b_re

    def c_times(d, pw):
        cr, ci = c_re[:, d][:, :, None], c_im[:, d][:, :, None]
        r, i = pw[0][:, :, :, None, :], pw[1][:, :, :, None, :]
        return cr * r - ci * i, cr * i + ci * r

    def b_times(d, pw):
        br, bi = bbr[:, d][:, :, :, None, :], bbi[:, d][:, :, :, None, :]
        r, i = pw[0].swapaxes(2, 3)[..., None], pw[1].swapaxes(2, 3)[..., None]
        return r * br - i * bi, r * bi + i * br

    def b_times_t(d, pw):
        br, bi = bbr[:, d].swapaxes(-1, -2)[:, :, None], bbi[:, d].swapaxes(-1, -2)[:, :, None]
        r, i = pw[0][:, :, :, None, :], pw[1][:, :, :, None, :]
        return r * br - i * bi, r * bi + i * br

    def lag_kernel(d, pw_t, pw_s):
        ur, ui = c_times(d, pw_t)
        vr, vi = b_times(d, pw_s)
        ur, ui = ur.reshape(nl, SSM_G, CW, SSM_P), ui.reshape(nl, SSM_G, CW, SSM_P)
        vr, vi = vr.reshape(nl, SSM_G, SSM_P, CW), vi.reshape(nl, SSM_G, SSM_P, CW)
        return (jnp.einsum('lgpa,lgbp->lgab', vr, ur, precision=hp)
                - jnp.einsum('lgpa,lgbp->lgab', vi, ui, precision=hp))

    kf = lag_kernel(0, powers(0, 0, q), powers(0, -(q - 1), 1, rev=True))
    kb = lag_kernel(1, powers(1, -(q - 1), 1, rev=True), powers(1, 0, q))
    s_i = (jnp.arange(CW) // SSM_H)[:, None]
    t_i = (jnp.arange(CW) // SSM_H)[None, :]
    tm = jnp.where(t_i >= s_i, kf, 0.0) + jnp.where(s_i >= t_i, kb, 0.0)

    f_re, f_im = b_times_t(0, powers(0, 0, q, rev=True))
    g_re, g_im = b_times_t(1, powers(1, 0, q))
    rm = jnp.concatenate([f_re, g_re, f_im, g_im], axis=-1).reshape(nl, SSM_G, CW, 4 * SSM_P)

    of_r, of_i = c_times(0, powers(0, 1, q + 1))
    ob_r, ob_i = c_times(1, powers(1, 1, q + 1, rev=True))
    fix_o = lambda a: a.transpose(0, 1, 4, 2, 3).reshape(nl, SSM_G, SSM_P, CW)
    om = jnp.concatenate([fix_o(of_r), fix_o(ob_r), fix_o(-of_i), fix_o(-ob_i)], axis=2)

    last = 2 * q - 1
    lq = jnp.stack([jnp.concatenate([pr[:, 0, :, last], pr[:, 1, :, last]], -1),
                    jnp.concatenate([pi[:, 0, :, last], pi[:, 1, :, last]], -1)], axis=2)
    dvec = jnp.tile(d_skip[:, :, None, :], (1, 1, q, 1)).reshape(nl, SSM_G, 1, CW)
    return tm.astype(BF16), rm.astype(BF16), om.astype(BF16), lq, dvec


def _scan_kernel(x_ref, t_ref, r_ref, o_ref, lq_ref, d_ref, h0_ref, y_ref, fin_ref,
                 rr_ref, st_ref, *, n_chunks, bsz):
    p2 = 2 * SSM_P
    x = x_ref[...]
    rr_ref[...] = jnp.dot(x, r_ref[...], preferred_element_type=F32)
    lq = lq_ref[...]
    lre, lim = lq[0:1], lq[1:2]
    is_f = lax.broadcasted_iota(jnp.int32, (bsz, p2), 1) < SSM_P
    h0 = h0_ref[...]

    def step(j, carry):
        sre, sim = carry
        rf = pl.multiple_of(j * bsz, bsz)
        rb = pl.multiple_of((n_chunks - 1 - j) * bsz, bsz)
        st_ref[pl.ds(rf, bsz), 0:SSM_P] = sre[:, 0:SSM_P]
        st_ref[pl.ds(rb, bsz), SSM_P:p2] = sre[:, SSM_P:p2]
        st_ref[pl.ds(rf, bsz), p2:p2 + SSM_P] = sim[:, 0:SSM_P]
        st_ref[pl.ds(rb, bsz), p2 + SSM_P:2 * p2] = sim[:, SSM_P:p2]
        r_re = jnp.where(is_f, rr_ref[pl.ds(rf, bsz), 0:p2], rr_ref[pl.ds(rb, bsz), 0:p2])
        r_im = jnp.where(is_f, rr_ref[pl.ds(rf, bsz), p2:2 * p2], rr_ref[pl.ds(rb, bsz), p2:2 * p2])
        return lre * sre - lim * sim + r_re, lre * sim + lim * sre + r_im

    sre, sim = lax.fori_loop(0, n_chunks, step, (h0[:, 0:p2], h0[:, p2:2 * p2]))
    fin_ref[...] = jnp.concatenate([sre, sim], axis=1)
    y = jnp.dot(x, t_ref[...], preferred_element_type=F32)
    y = y + jnp.dot(st_ref[...].astype(BF16), o_ref[...], preferred_element_type=F32)
    y = y + d_ref[...] * x.astype(F32)
    y_ref[...] = jax.nn.gelu(y, approximate=True).astype(BF16)


def _scan(xg, tm, rm, om, lq, dvec, h0, bsz):
    g, n, _ = xg.shape
    n_chunks = n // bsz
    mat = pl.BlockSpec((None, CW, CW), lambda i: (i, 0, 0))
    return pl.pallas_call(
        functools.partial(_scan_kernel, n_chunks=n_chunks, bsz=bsz),
        grid=(g,),
        in_specs=[pl.BlockSpec((None, n, CW), lambda i: (i, 0, 0)), mat, mat, mat,
                  pl.BlockSpec((None, 2, 2 * SSM_P), lambda i: (i, 0, 0)),
                  pl.BlockSpec((None, 1, CW), lambda i: (i, 0, 0)),
                  pl.BlockSpec((None, bsz, 4 * SSM_P), lambda i: (i, 0, 0))],
        out_specs=[pl.BlockSpec((None, n, CW), lambda i: (i, 0, 0)),
                   pl.BlockSpec((None, bsz, 4 * SSM_P), lambda i: (i, 0, 0))],
        out_shape=[jax.ShapeDtypeStruct((g, n, CW), BF16),
                   jax.ShapeDtypeStruct((g, bsz, 4 * SSM_P), F32)],
        scratch_shapes=[pltpu.VMEM((n, 4 * SSM_P), F32), pltpu.VMEM((n, 4 * SSM_P), F32)],
        compiler_params=_cp("arbitrary"),
        name="s5_scan",
    )(xg, tm, rm, om, lq, dvec, h0)


def _chunk_perm():
    n = CHUNK * LANES
    src = (jnp.arange(n, dtype=jnp.int32).reshape(CHUNK, GROUPS_PER_TILE, SSM_H)
           .transpose(1, 0, 2).reshape(-1))
    return (jnp.arange(n, dtype=jnp.int32)[:, None] == src[None, :]).astype(BF16)


def _to_chunks(u, bsz, length, perm):
    nc = length // CHUNK
    z = (u.reshape(bsz, nc, CHUNK, LANE_TILES, LANES).transpose(3, 1, 0, 2, 4)
         .reshape(LANE_TILES, nc * bsz, CHUNK * LANES))
    w = jnp.einsum('jnk,kc->jnc', z, perm, preferred_element_type=BF16)
    return (w.reshape(LANE_TILES, nc * bsz, GROUPS_PER_TILE, CW).transpose(0, 2, 1, 3)
            .reshape(SSM_G, nc * bsz, CW))


def _from_chunks(y, bsz, length, perm):
    nc = length // CHUNK
    w = (y.reshape(LANE_TILES, GROUPS_PER_TILE, nc * bsz, CW).transpose(0, 2, 1, 3)
         .reshape(LANE_TILES, nc * bsz, CHUNK * LANES))
    z = jnp.einsum('jnc,kc->jnk', w, perm, preferred_element_type=BF16)
    return (z.reshape(LANE_TILES, nc, bsz, CHUNK, LANES).transpose(2, 1, 3, 0, 4)
            .reshape(bsz * length, D_SSM))


def _conv_post(cv, p_ref):
    p = p_ref[...]
    cv = cv + p[0:1]
    y = _silu(_ln(cv) * p[1:2] + p[2:3])
    return y * lax.rsqrt(jnp.mean(y * y, axis=-1, keepdims=True) + LN_EPS) * p[3:4]


def _conv_seq_tile(buf_ref, base, w_ref, b0, c0):
    win = buf_ref[pl.ds(base, CONV_WIN), b0:b0 + LANES]
    acc = jnp.zeros((GRID_W, LANES), F32)
    for ph in range(8):
        wb = win if ph == 0 else pltpu.roll(win, CONV_WIN - ph, axis=0)
        for a in range(4):
            j = 8 * a + ph - 1
            if 0 <= j < CONV_K:
                acc = acc + w_ref[j:j + 1, c0:c0 + LANES] * wb[8 * a:8 * a + GRID_W]
    return acc


def _conv_seq(buf_ref, base, w_ref, c0, width):
    return jnp.concatenate([_conv_seq_tile(buf_ref, base, w_ref, b, c0 + b)
                            for b in range(0, width, LANES)], axis=1)


def _conv_strided(buf_ref, base, w_ref, c0, width):
    parts = []
    for b in range(0, width, LANES):
        acc = jnp.zeros((GRID_W, LANES), F32)
        for j in range(CONV_K):
            acc = acc + w_ref[j:j + 1, c0 + b:c0 + b + LANES] * buf_ref[pl.ds(base + j * GRID_W, GRID_W), b:b + LANES]
        parts.append(acc)
    return jnp.concatenate(parts, axis=1)


def _conv_lat_kernel(z_ref, w_ref, p_ref, o_ref, hb_ref, vb_ref, *, rows):
    half = D_CONV // 2
    hstride = GRID_W + 16
    hb_ref[...] = jnp.zeros(hb_ref.shape, F32)
    vb_ref[pl.ds(0, CONV_PAD * GRID_W), :] = jnp.zeros((CONV_PAD * GRID_W, half), F32)
    vb_ref[pl.ds((CONV_PAD + rows) * GRID_W, CONV_PAD * GRID_W), :] = jnp.zeros((CONV_PAD * GRID_W, half), F32)
    for r in range(rows):
        hb_ref[pl.ds(16 + r * hstride, GRID_W), :] = z_ref[pl.ds(r * GRID_W, GRID_W), 0:half].astype(F32)
    vb_ref[pl.ds(CONV_PAD * GRID_W, rows * GRID_W), :] = z_ref[:, half:D_CONV].astype(F32)

    def row(r, carry):
        hbase = pl.multiple_of(r * hstride, 16)
        vbase = pl.multiple_of(r * GRID_W, GRID_W)
        acc_h = _conv_seq(hb_ref, hbase, w_ref, 0, half)
        acc_v = _conv_strided(vb_ref, vbase, w_ref, half, half)
        out = _conv_post(jnp.concatenate([acc_h, acc_v], axis=1), p_ref)
        o_ref[pl.ds(vbase, GRID_W), :] = out.astype(BF16)
        return carry

    lax.fori_loop(0, rows, row, 0)


def _conv_lat(z, w, p, bsz, seq, blk0):
    rows = seq // GRID_W
    half = D_CONV // 2
    return pl.pallas_call(
        functools.partial(_conv_lat_kernel, rows=rows),
        grid=(bsz,),
        in_specs=[pl.BlockSpec((seq, D_CONV), lambda b: (blk0 + b, 0)),
                  pl.BlockSpec(w.shape, lambda b: (0, 0)),
                  pl.BlockSpec(p.shape, lambda b: (0, 0))],
        out_specs=pl.BlockSpec((seq, D_CONV), lambda b: (b, 0)),
        out_shape=jax.ShapeDtypeStruct((bsz * seq, D_CONV), BF16),
        scratch_shapes=[pltpu.VMEM((rows * (GRID_W + 16) + 16, half), F32),
                        pltpu.VMEM(((rows + 2 * CONV_PAD) * GRID_W, half), F32)],
        compiler_params=_cp("arbitrary"),
        name="conv_latent",
    )(z, w, p)


def _conv_ctx_kernel(z_ref, w_ref, p_ref, o_ref, cb_ref, *, clen):
    cb_ref[pl.ds(0, 16), :] = jnp.zeros((16, D_CONV), F32)
    cb_ref[pl.ds(16 + clen, 16), :] = jnp.zeros((16, D_CONV), F32)
    cb_ref[pl.ds(16, clen), :] = z_ref[...].astype(F32)
    for blk in range(clen // GRID_W):
        base = blk * GRID_W
        acc = _conv_seq(cb_ref, base, w_ref, 0, D_CONV)
        o_ref[pl.ds(base, GRID_W), :] = _conv_post(acc, p_ref).astype(BF16)


def _conv_ctx(z, w, p, bsz, clen):
    return pl.pallas_call(
        functools.partial(_conv_ctx_kernel, clen=clen),
        grid=(bsz,),
        in_specs=[pl.BlockSpec((clen, D_CONV), lambda b: (b, 0)),
                  pl.BlockSpec(w.shape, lambda b: (0, 0)),
                  pl.BlockSpec(p.shape, lambda b: (0, 0))],
        out_specs=pl.BlockSpec((clen, D_CONV), lambda b: (b, 0)),
        out_shape=jax.ShapeDtypeStruct((bsz * clen, D_CONV), BF16),
        scratch_shapes=[pltpu.VMEM((clen + 32, D_CONV), F32)],
        compiler_params=_cp("arbitrary"),
        name="conv_context",
    )(z, w, p)


def _outproj_kernel(ys_ref, yc_ref, x_ref, mod_ref, wglu_ref, pv_ref, wo_ref, ln_ref, o_ref):
    ys = ys_ref[...]
    pv = pv_ref[...]
    gl = jnp.dot(ys, wglu_ref[...], preferred_element_type=F32) + pv[0:1]
    yg = ys.astype(F32) * jax.nn.sigmoid(gl)
    yn = yg * lax.rsqrt(jnp.mean(yg * yg, axis=-1, keepdims=True) + LN_EPS) * pv[1:2]
    y = jnp.dot(yn.astype(BF16), wo_ref[0:D_SSM, :], preferred_element_type=F32)
    y = y + jnp.dot(yc_ref[...], wo_ref[D_SSM:, :], preferred_element_type=F32)
    m = mod_ref[...]
    ln = ln_ref[...]
    o_ref[...] = _ln(ALPHA * x_ref[...] + m[2:3] * y) * ln[0:1] + ln[1:2]


def _outproj(ys, yc, xs, mod, wglu, pv, wo, ln, t_ctx, seq):
    t, d = xs.shape
    full = lambda a: pl.BlockSpec(a.shape, lambda i: (0, 0))
    return pl.pallas_call(
        _outproj_kernel,
        grid=(t // TMB,),
        in_specs=[pl.BlockSpec((TMB, D_SSM), lambda i: (i, 0)),
                  pl.BlockSpec((TMB, D_CONV), lambda i: (i, 0)),
                  pl.BlockSpec((TMB, d), lambda i: (i, 0)),
                  pl.BlockSpec((None, 6, d), lambda i: (_mod_row(i, t_ctx, seq, TMB), 0, 0)),
                  full(wglu), full(pv), full(wo), full(ln)],
        out_specs=pl.BlockSpec((TMB, d), lambda i: (i, 0)),
        out_shape=jax.ShapeDtypeStruct((t, d), F32),
        compiler_params=_cp("arbitrary"),
        name="outproj",
    )(ys, yc, xs, mod, wglu, pv, wo, ln)


ROW_TILE = 8


def _rows_to_tiles(ref, lead, val):
    n = val.shape[0]
    for c in range(ROW_TILE):
        ref[lead + (pl.ds(c, n, stride=ROW_TILE), slice(None))] = val[:, c * LANES:(c + 1) * LANES]


def _tiles_to_rows(ref, lead, n):
    return jnp.concatenate([ref[lead + (pl.ds(c, n, stride=ROW_TILE), slice(None))]
                            for c in range(ROW_TILE)], axis=1)


def _router_kernel(x_ref, mod_ref, wr_ref, rb_ref, h_ref, idx_ref, wt_ref, cnt_ref, carry_ref):
    @pl.when(pl.program_id(0) == 0)
    def _():
        carry_ref[...] = jnp.zeros(carry_ref.shape, F32)

    m = mod_ref[...]
    h = _ln(x_ref[...]) * (1.0 + m[4:5]) + m[3:4]
    _rows_to_tiles(h_ref, (), h)
    logits = lax.dot_general(wr_ref[...], h, (((1,), (1,)), ((), ())),
                             precision=lax.Precision.HIGHEST, preferred_element_type=F32)
    s = jax.nn.sigmoid(logits)
    biased = s + rb_ref[...]
    ninf = -jnp.inf
    tm = s.shape[1]

    row8 = lax.broadcasted_iota(jnp.int32, (GROUP_SIZE, tm), 0).astype(F32)
    gs = []
    for q in range(N_GROUPS):
        v = biased[q * GROUP_SIZE:(q + 1) * GROUP_SIZE]
        m1 = jnp.max(v, axis=0, keepdims=True)
        i1 = jnp.min(jnp.where(v == m1, row8, float(GROUP_SIZE)), axis=0, keepdims=True)
        m2 = jnp.max(jnp.where(row8 == i1, ninf, v), axis=0, keepdims=True)
        gs.append(m1 + m2)
    gsc = jnp.concatenate(gs, axis=0)
    rowg = lax.broadcasted_iota(jnp.int32, (N_GROUPS, tm), 0).astype(F32)
    gsel = jnp.zeros((N_GROUPS, tm), F32)
    for _ in range(TOP_K_GROUPS):
        mx = jnp.max(gsc, axis=0, keepdims=True)
        ii = jnp.min(jnp.where(gsc == mx, rowg, float(N_GROUPS)), axis=0, keepdims=True)
        hit = rowg == ii
        gsel = jnp.where(hit, 1.0, gsel)
        gsc = jnp.where(hit, ninf, gsc)
    emask = jnp.concatenate([jnp.broadcast_to(gsel[q:q + 1], (GROUP_SIZE, tm))
                             for q in range(N_GROUPS)], axis=0)
    masked = jnp.where(emask > 0.0, biased, ninf)

    rowe = lax.broadcasted_iota(jnp.int32, (N_EXPERTS, tm), 0).astype(F32)
    selm = jnp.zeros((N_EXPERTS, tm), F32)
    idxs, ws = [], []
    for _ in range(TOP_K):
        mx = jnp.max(masked, axis=0, keepdims=True)
        ii = jnp.min(jnp.where(masked == mx, rowe, float(N_EXPERTS)), axis=0, keepdims=True)
        hit = rowe == ii
        idxs.append(ii)
        ws.append(jnp.sum(jnp.where(hit, s, 0.0), axis=0, keepdims=True))
        selm = jnp.where(hit, 1.0, selm)
        masked = jnp.where(hit, ninf, masked)
    wsum = ws[0]
    for k in range(1, TOP_K):
        wsum = wsum + ws[k]
    idx_ref[...] = jnp.concatenate(idxs, axis=0).astype(jnp.int32)
    wt_ref[...] = jnp.concatenate([w / wsum * ROUTED_SCALE for w in ws], axis=0)

    new_carry = carry_ref[...] + jnp.sum(selm, axis=1, keepdims=True)
    carry_ref[...] = new_carry
    cnt_ref[...] = new_carry


def _router(xs, mod, wr_t, rb, t_ctx, seq):
    t, d = xs.shape
    full = lambda a: pl.BlockSpec(a.shape, lambda i: (0, 0))
    return pl.pallas_call(
        _router_kernel,
        grid=(t // TMB,),
        in_specs=[pl.BlockSpec((TMB, d), lambda i: (i, 0)),
                  pl.BlockSpec((None, 6, d), lambda i: (_mod_row(i, t_ctx, seq, TMB), 0, 0)),
                  full(wr_t), full(rb)],
        out_specs=[pl.BlockSpec((TMB * ROW_TILE, LANES), lambda i: (i, 0)),
                   pl.BlockSpec((TOP_K, TMB), lambda i: (0, i)),
                   pl.BlockSpec((TOP_K, TMB), lambda i: (0, i)),
                   pl.BlockSpec((N_EXPERTS, 1), lambda i: (0, 0))],
        out_shape=[jax.ShapeDtypeStruct((t * ROW_TILE, LANES), F32),
                   jax.ShapeDtypeStruct((TOP_K, t), jnp.int32),
                   jax.ShapeDtypeStruct((TOP_K, t), F32),
                   jax.ShapeDtypeStruct((N_EXPERTS, 1), F32)],
        scratch_shapes=[pltpu.VMEM((N_EXPERTS, 1), F32)],
        compiler_params=_cp("arbitrary"),
        name="router",
    )(xs, mod, wr_t, rb)


def _experts_kernel(te_ref, tv_ref, src0_ref, srcn_ref, dst_ref, h_ref, wg_ref, wu_ref, wd_ref,
                    y_ref, xbuf, ybuf, gsem, ssem):
    i = pl.program_id(0)
    valid = tv_ref[i] > 0
    prev_valid = tv_ref[jnp.maximum(i - 1, 0)] > 0
    rows = TME * ROW_TILE

    def gather_start(idx_ref, s):
        for r in range(TME):
            pltpu.make_async_copy(
                h_ref.at[pl.ds(pl.multiple_of(idx_ref[0, r], ROW_TILE), ROW_TILE), :],
                xbuf.at[s, pl.ds(r * ROW_TILE, ROW_TILE), :], gsem.at[s]).start()

    def scatter_start(s):
        for r in range(TME):
            pltpu.make_async_copy(
                ybuf.at[s, pl.ds(r * ROW_TILE, ROW_TILE), :],
                y_ref.at[pl.ds(pl.multiple_of(dst_ref[0, r], ROW_TILE), ROW_TILE), :],
                ssem.at[s]).start()

    def gather_wait(s):
        pltpu.make_async_copy(h_ref.at[pl.ds(0, rows), :], xbuf.at[s], gsem.at[s]).wait()

    def scatter_wait(s):
        pltpu.make_async_copy(ybuf.at[s], y_ref.at[pl.ds(0, rows), :], ssem.at[s]).wait()

    @pl.when(i == 0)
    def _():
        gather_start(src0_ref, 0)

    def step(s):
        @pl.when(valid)
        def _():
            @pl.when(i >= 2)
            def _():
                scatter_wait(s)

            gather_wait(s)
            gather_start(srcn_ref, 1 - s)
            x = _tiles_to_rows(xbuf, (s,), TME).astype(BF16)
            a = jnp.dot(x, wg_ref[...].astype(BF16), preferred_element_type=F32)
            b = jnp.dot(x, wu_ref[...].astype(BF16), preferred_element_type=F32)
            hid = (_silu(a) * b).astype(BF16)
            _rows_to_tiles(ybuf, (s,), jnp.dot(hid, wd_ref[...].astype(BF16), preferred_element_type=F32))
            scatter_start(s)

        @pl.when(jnp.logical_not(valid) & prev_valid & (i >= 1))
        def _():
            gather_wait(s)
            scatter_wait(1 - s)

            @pl.when(i >= 2)
            def _():
                scatter_wait(s)

            dump = pltpu.make_async_copy(ybuf.at[s], y_ref.at[pl.ds(y_ref.shape[0] - rows, rows), :],
                                         ssem.at[s])
            dump.start()
            dump.wait()

    for s in range(2):
        pl.when(i % 2 == s)(functools.partial(step, s))


def _experts(tile_expert, tile_valid, src, dst, h2, wg, wu, wd, n_out):
    n_tiles = src.shape[0]
    d, f = wg.shape[1:]
    smem = lambda fn: pl.BlockSpec((None, 1, TME), fn, memory_space=pltpu.SMEM)
    grid_spec = pltpu.PrefetchScalarGridSpec(
        num_scalar_prefetch=2,
        grid=(n_tiles,),
        in_specs=[smem(lambda i, te, tv: (0, 0, 0)),
                  smem(lambda i, te, tv: (jnp.minimum(i + 1, n_tiles - 1), 0, 0)),
                  smem(lambda i, te, tv: (i, 0, 0)),
                  pl.BlockSpec(memory_space=pl.ANY),
                  pl.BlockSpec((None, d, f), lambda i, te, tv: (te[i], 0, 0)),
                  pl.BlockSpec((None, d, f), lambda i, te, tv: (te[i], 0, 0)),
                  pl.BlockSpec((None, f, d), lambda i, te, tv: (te[i], 0, 0))],
        out_specs=pl.BlockSpec(memory_space=pl.ANY),
        scratch_shapes=[pltpu.VMEM((2, TME * ROW_TILE, LANES), F32),
                        pltpu.VMEM((2, TME * ROW_TILE, LANES), F32),
                        pltpu.SemaphoreType.DMA((2,)), pltpu.SemaphoreType.DMA((2,))],
    )
    return pl.pallas_call(
        _experts_kernel,
        grid_spec=grid_spec,
        out_shape=jax.ShapeDtypeStruct((n_out * ROW_TILE, LANES), F32),
        compiler_params=_cp("arbitrary"),
        name="experts",
    )(tile_expert, tile_valid, src, src, dst, h2, wg, wu, wd)


def _combine_kernel(*refs):
    yg_refs = refs[:TOP_K]
    wt_ref, h_ref, x_ref, mod_ref, wsg_ref, wsu_ref, wsd_ref, ln_ref, o_ref = refs[TOP_K:]
    wt = wt_ref[...]
    n = x_ref.shape[0]
    acc = wt[:, 0:1] * _tiles_to_rows(yg_refs[0], (), n)
    for k in range(1, TOP_K):
        acc = acc + wt[:, k:k + 1] * _tiles_to_rows(yg_refs[k], (), n)
    h = _tiles_to_rows(h_ref, (), n).astype(BF16)
    a = jnp.dot(h, wsg_ref[...], preferred_element_type=F32)
    b = jnp.dot(h, wsu_ref[...], preferred_element_type=F32)
    acc = acc + jnp.dot((_silu(a) * b).astype(BF16), wsd_ref[...], preferred_element_type=F32)
    m = mod_ref[...]
    ln = ln_ref[...]
    o_ref[...] = _ln(ALPHA * x_ref[...] + m[5:6] * acc) * ln[0:1] + ln[1:2]


def _combine(yg, wt, h2, xs, mod, wsg, wsu, wsd, ln, t_ctx, seq):
    t, d = xs.shape
    full = lambda a: pl.BlockSpec(a.shape, lambda i: (0, 0))
    n_blocks = t // TM
    slot_specs = [pl.BlockSpec((TM * ROW_TILE, LANES), functools.partial(lambda i, k: (k * n_blocks + i, 0), k=k))
                  for k in range(TOP_K)]
    return pl.pallas_call(
        _combine_kernel,
        grid=(n_blocks,),
        in_specs=slot_specs + [
                  pl.BlockSpec((TM, TOP_K), lambda i: (i, 0)),
                  pl.BlockSpec((TM * ROW_TILE, LANES), lambda i: (i, 0)),
                  pl.BlockSpec((TM, d), lambda i: (i, 0)),
                  pl.BlockSpec((None, 6, d), lambda i: (_mod_row(i, t_ctx, seq, TM), 0, 0)),
                  full(wsg), full(wsu), full(wsd), full(ln)],
        out_specs=pl.BlockSpec((TM, d), lambda i: (i, 0)),
        out_shape=jax.ShapeDtypeStruct((t, d), F32),
        compiler_params=_cp("arbitrary"),
        name="combine",
    )(*([yg] * TOP_K), wt, h2, xs, mod, wsg, wsu, wsd, ln)


def _moe(xs, mod, wr_t, rb, wg, wu, wd, wsg, wsu, wsd, ln, t_ctx, seq):
    t, d = xs.shape
    h2, idx, wt, cnt = _router(xs, mod, wr_t, rb, t_ctx, seq)
    n_pairs = t * TOP_K
    n_rows = n_pairs + N_EXPERTS * TME
    n_tiles = n_rows // TME
    assert n_rows < (1 << KEY_SHIFT) and n_pairs % TME == 0
    counts = cnt[:, 0].astype(jnp.int32)
    n_padding = (-counts) % TME
    slot = jnp.arange(TME, dtype=jnp.int32)[None, :]
    pad_key = jnp.where(slot < n_padding[:, None], jnp.arange(N_EXPERTS, dtype=jnp.int32)[:, None],
                        N_EXPERTS).reshape(-1)
    keys = jnp.concatenate([idx.reshape(-1), pad_key]) * (1 << KEY_SHIFT) + jnp.arange(n_rows, dtype=jnp.int32)
    order = lax.sort(keys)
    pair = order & ((1 << KEY_SHIFT) - 1)
    src = ((pair % t) * ROW_TILE).reshape(n_tiles, 1, TME)
    dst = (jnp.where(pair < n_pairs, pair, n_pairs + pair % TME) * ROW_TILE).reshape(n_tiles, 1, TME)
    first = order[::TME] >> KEY_SHIFT
    tile_expert = jnp.minimum(first, N_EXPERTS - 1)
    tile_valid = (first < N_EXPERTS).astype(jnp.int32)
    ye = _experts(tile_expert, tile_valid, src, dst, h2, wg, wu, wd, n_pairs + TME)
    return _combine(ye, wt.T, h2, xs, mod, wsg, wsu, wsd, ln, t_ctx, seq)


def kernel(x, c, ctx, c_ctx, w_ada, b_ada, w_in, ssm_a_re, ssm_a_im, ssm_log_dt, ssm_b_re, ssm_b_im,
           ssm_c_re, ssm_c_im, ssm_d, w_glu, b_glu, conv_w, conv_b, conv_ln_g, conv_ln_b, mix_norm_g,
           w_out, ln1_g, ln1_b, w_router, router_bias, we_gate, we_up, we_down, ws_gate, ws_up,
           ws_down, ln2_g, ln2_b):
    bsz, seq, d = x.shape
    clen = ctx.shape[1]
    depth = w_in.shape[0]
    t_ctx, t_lat = bsz * clen, bsz * seq
    assert d == D_MODEL and bsz + 1 <= MOD_ROWS
    assert seq % GRID_W == 0 and clen % GRID_W == 0 and t_ctx % seq == 0
    assert seq % TMB == 0 and t_ctx % TMB == 0 and seq % TM == 0 and t_ctx % TM == 0

    xs = jnp.concatenate([ctx.reshape(t_ctx, d), x.reshape(t_lat, d)], axis=0)
    cc = jnp.zeros((MOD_ROWS, d), F32).at[0].set(c_ctx).at[1:bsz + 1].set(c)
    mod_all = _ada(cc, w_ada, b_ada).reshape(depth, MOD_ROWS, 6, d)
    perm = _chunk_perm()
    s5 = _s5_mats(ssm_a_re, ssm_a_im, ssm_log_dt, ssm_b_re, ssm_b_im, ssm_c_re, ssm_c_im, ssm_d)

    for l in range(depth):
        mod = mod_all[l]
        u, z = _inproj(xs, mod, w_in[l].astype(BF16), t_ctx, seq)

        tm, rm, om, lq, dvec = (a[l] for a in s5)
        h0 = jnp.zeros((SSM_G, bsz, 4 * SSM_P), F32)
        yc, fin = _scan(_to_chunks(u[:t_ctx], bsz, clen, perm), tm, rm, om, lq, dvec, h0, bsz)
        yl, _ = _scan(_to_chunks(u[t_ctx:], bsz, seq, perm), tm, rm, om, lq, dvec, fin, bsz)
        ys = jnp.concatenate([_from_chunks(yc, bsz, clen, perm), _from_chunks(yl, bsz, seq, perm)], axis=0)

        cp = jnp.stack([conv_b[l], conv_ln_g[l], conv_ln_b[l], mix_norm_g[l, D_SSM:]])
        zc = jnp.concatenate([_conv_ctx(z, conv_w[l], cp, bsz, clen),
                              _conv_lat(z, conv_w[l], cp, bsz, seq, t_ctx // seq)], axis=0)

        pv = jnp.stack([b_glu[l], mix_norm_g[l, :D_SSM]])
        xs = _outproj(ys, zc, xs, mod, w_glu[l].astype(BF16), pv, w_out[l].astype(BF16),
                      jnp.stack([ln1_g[l], ln1_b[l]]), t_ctx, seq)

        xs = _moe(xs, mod, w_router[l].T, router_bias[l][:, None], we_gate[l], we_up[l], we_down[l],
                  ws_gate[l].astype(BF16), ws_up[l].astype(BF16), ws_down[l].astype(BF16),
                  jnp.stack([ln2_g[l], ln2_b[l]]), t_ctx, seq)

    return xs[t_ctx:].reshape(bsz, seq, d)
```

```python
import functools
import math

import jax
import jax.numpy as jnp
from jax import lax
from jax.experimental import pallas as pl
from jax.experimental.pallas import tpu as pltpu

F32 = jnp.float32
BF16 = jnp.bfloat16

D_MODEL = 1024
DEPTH = 4
GRID_W = 64
D_SSM = 512
D_CONV = 512
SSM_H = 16
SSM_G = 32
SSM_P = 64
CONV_K = 31
CONV_PAD = 15
CONV_WIN = GRID_W + 32
N_EXPERTS = 64
TOP_K = 8
N_GROUPS = 8
TOP_K_GROUPS = 4
GROUP_SIZE = N_EXPERTS // N_GROUPS
D_EXPERT = 256
ROUTED_SCALE = 2.5
ALPHA = (2 * DEPTH) ** 0.25
LN_EPS = 1e-5

LANES = 128
CHUNK = 16
CW = CHUNK * SSM_H
LANE_TILES = D_SSM // LANES
GROUPS_PER_TILE = LANES // SSM_H
MOD_ROWS = 32
TM = 256
TMB = 512
TME = 512
KEY_SHIFT = 19
VMEM_LIMIT = 48 * 1024 * 1024


def _cp(*sem):
    return pltpu.CompilerParams(dimension_semantics=sem, vmem_limit_bytes=VMEM_LIMIT)


def _ln(x):
    mu = jnp.mean(x, axis=-1, keepdims=True)
    xc = x - mu
    var = jnp.mean(xc * xc, axis=-1, keepdims=True)
    return xc * lax.rsqrt(var + LN_EPS)


def _silu(x):
    return x * jax.nn.sigmoid(x)


def _ada_kernel(c_ref, w_ref, b_ref, o_ref):
    a = _silu(c_ref[...]).astype(BF16)
    o_ref[...] = jnp.dot(a, w_ref[...].astype(BF16), preferred_element_type=F32) + b_ref[...]


def _ada(cc, w_ada, b_ada):
    depth, d, n = w_ada.shape
    tn = 1536
    return pl.pallas_call(
        _ada_kernel,
        grid=(depth, n // tn),
        in_specs=[pl.BlockSpec((MOD_ROWS, d), lambda l, j: (0, 0)),
                  pl.BlockSpec((None, d, tn), lambda l, j: (l, 0, j)),
                  pl.BlockSpec((None, 1, tn), lambda l, j: (l, 0, j))],
        out_specs=pl.BlockSpec((None, MOD_ROWS, tn), lambda l, j: (l, 0, j)),
        out_shape=jax.ShapeDtypeStruct((depth, MOD_ROWS, n), F32),
        compiler_params=_cp("arbitrary", "arbitrary"),
        name="ada",
    )(cc, w_ada, b_ada.reshape(depth, 1, n))


def _inproj_kernel(x_ref, mod_ref, w_ref, u_ref, z_ref):
    m = mod_ref[...]
    h = _ln(x_ref[...]) * (1.0 + m[1:2]) + m[0:1]
    p = jnp.dot(h.astype(BF16), w_ref[...], preferred_element_type=F32)
    u_ref[...] = p[:, :D_SSM].astype(BF16)
    v = p[:, D_SSM:D_SSM + D_CONV]
    g = p[:, D_SSM + D_CONV:]
    z_ref[...] = (v * jax.nn.sigmoid(g)).astype(BF16)


def _mod_row(i, t_ctx, seq, tile):
    start = i * tile
    return jnp.where(start < t_ctx, 0, 1 + (start - t_ctx) // seq)


def _inproj(xs, mod, w, t_ctx, seq):
    t, d = xs.shape
    return pl.pallas_call(
        _inproj_kernel,
        grid=(t // TMB,),
        in_specs=[pl.BlockSpec((TMB, d), lambda i: (i, 0)),
                  pl.BlockSpec((None, 6, d), lambda i: (_mod_row(i, t_ctx, seq, TMB), 0, 0)),
                  pl.BlockSpec(w.shape, lambda i: (0, 0))],
        out_specs=[pl.BlockSpec((TMB, D_SSM), lambda i: (i, 0)),
                   pl.BlockSpec((TMB, D_CONV), lambda i: (i, 0))],
        out_shape=[jax.ShapeDtypeStruct((t, D_SSM), BF16),
                   jax.ShapeDtypeStruct((t, D_CONV), BF16)],
        compiler_params=_cp("arbitrary"),
        name="inproj",
    )(xs, mod, w)


def _s5_mats(a_re, a_im, log_dt, b_re, b_im, c_re, c_im, d_skip):
    hp = lax.Precision.HIGHEST
    q = CHUNK
    nl = a_re.shape[0]
    dt = jnp.exp(log_dt)[..., None]
    ldr, ldi = a_re * dt, a_im * dt
    ks = jnp.arange(-(q - 1), q + 1, dtype=F32)[:, None]
    mag = jnp.exp(ks * ldr[..., None, :])
    ang = ks * ldi[..., None, :]
    pr, pi = mag * jnp.cos(ang), mag * jnp.sin(ang)

    def powers(d, lo, hi, rev=False):
        r, i = pr[:, d, :, lo + q - 1:hi + q - 1], pi[:, d, :, lo + q - 1:hi + q - 1]
        return (r[:, :, ::-1], i[:, :, ::-1]) if rev else (r, i)

    nr, ni = pr[..., q, :] - 1.0, pi[..., q, :]
    den = a_re * a_re + a_im * a_im
    qr, qi = (nr * a_re + ni * a_im) / den, (ni * a_re - nr * a_im) / den
    bbr = qr[..., None] * b_re - qi[..., None] * b_im
    bbi = qr[..., None] * b_im + qi[..., None] * b_re

    def c_times(d, pw):
        cr, ci = c_re[:, d][:, :, None], c_im[:, d][:, :, None]
        r, i = pw[0][:, :, :, None, :], pw[1][:, :, :, None, :]
        return cr * r - ci * i, cr * i + ci * r

    def b_times(d, pw):
        br, bi = bbr[:, d][:, :, :, None, :], bbi[:, d][:, :, :, None, :]
        r, i = pw[0].swapaxes(2, 3)[..., None], pw[1].swapaxes(2, 3)[..., None]
        return r * br - i * bi, r * bi + i * br

    def b_times_t(d, pw):
        br, bi = bbr[:, d].swapaxes(-1, -2)[:, :, None], bbi[:, d].swapaxes(-1, -2)[:, :, None]
        r, i = pw[0][:, :, :, None, :], pw[1][:, :, :, None, :]
        return r * br - i * bi, r * bi + i * br

    def lag_kernel(d, pw_t, pw_s):
        ur, ui = c_times(d, pw_t)
        vr, vi = b_times(d, pw_s)
        ur, ui = ur.reshape(nl, SSM_G, CW, SSM_P), ui.reshape(nl, SSM_G, CW, SSM_P)
        vr, vi = vr.reshape(nl, SSM_G, SSM_P, CW), vi.reshape(nl, SSM_G, SSM_P, CW)
        return (jnp.einsum('lgpa,lgbp->lgab', vr, ur, precision=hp)
                - jnp.einsum('lgpa,lgbp->lgab', vi, ui, precision=hp))

    kf = lag_kernel(0, powers(0, 0, q), powers(0, -(q - 1), 1, rev=True))
    kb = lag_kernel(1, powers(1, -(q - 1), 1, rev=True), powers(1, 0, q))
    s_i = (jnp.arange(CW) // SSM_H)[:, None]
    t_i = (jnp.arange(CW) // SSM_H)[None, :]
    tm = jnp.where(t_i >= s_i, kf, 0.0) + jnp.where(s_i >= t_i, kb, 0.0)

    f_re, f_im = b_times_t(0, powers(0, 0, q, rev=True))
    g_re, g_im = b_times_t(1, powers(1, 0, q))
    rm = jnp.concatenate([f_re, g_re, f_im, g_im], axis=-1).reshape(nl, SSM_G, CW, 4 * SSM_P)

    of_r, of_i = c_times(0, powers(0, 1, q + 1))
    ob_r, ob_i = c_times(1, powers(1, 1, q + 1, rev=True))
    fix_o = lambda a: a.transpose(0, 1, 4, 2, 3).reshape(nl, SSM_G, SSM_P, CW)
    om = jnp.concatenate([fix_o(of_r), fix_o(ob_r), fix_o(-of_i), fix_o(-ob_i)], axis=2)

    last = 2 * q - 1
    lq = jnp.stack([jnp.concatenate([pr[:, 0, :, last], pr[:, 1, :, last]], -1),
                    jnp.concatenate([pi[:, 0, :, last], pi[:, 1, :, last]], -1)], axis=2)
    dvec = jnp.tile(d_skip[:, :, None, :], (1, 1, q, 1)).reshape(nl, SSM_G, 1, CW)
    return tm.astype(BF16), rm.astype(BF16), om.astype(BF16), lq, dvec


def _scan_kernel(x_ref, t_ref, r_ref, o_ref, lq_ref, d_ref, h0_ref, y_ref, fin_ref,
                 rr_ref, st_ref, *, n_chunks, bsz):
    p2 = 2 * SSM_P
    x = x_ref[...]
    rr_ref[...] = jnp.dot(x, r_ref[...], preferred_element_type=F32)
    lq = lq_ref[...]
    lre, lim = lq[0:1], lq[1:2]
    is_f = lax.broadcasted_iota(jnp.int32, (bsz, p2), 1) < SSM_P
    h0 = h0_ref[...]

    def step(j, carry):
        sre, sim = carry
        rf = pl.multiple_of(j * bsz, bsz)
        rb = pl.multiple_of((n_chunks - 1 - j) * bsz, bsz)
        st_ref[pl.ds(rf, bsz), 0:SSM_P] = sre[:, 0:SSM_P]
        st_ref[pl.ds(rb, bsz), SSM_P:p2] = sre[:, SSM_P:p2]
        st_ref[pl.ds(rf, bsz), p2:p2 + SSM_P] = sim[:, 0:SSM_P]
        st_ref[pl.ds(rb, bsz), p2 + SSM_P:2 * p2] = sim[:, SSM_P:p2]
        r_re = jnp.where(is_f, rr_ref[pl.ds(rf, bsz), 0:p2], rr_ref[pl.ds(rb, bsz), 0:p2])
        r_im = jnp.where(is_f, rr_ref[pl.ds(rf, bsz), p2:2 * p2], rr_ref[pl.ds(rb, bsz), p2:2 * p2])
        return lre * sre - lim * sim + r_re, lre * sim + lim * sre + r_im

    sre, sim = lax.fori_loop(0, n_chunks, step, (h0[:, 0:p2], h0[:, p2:2 * p2]))
    fin_ref[...] = jnp.concatenate([sre, sim], axis=1)
    y = jnp.dot(x, t_ref[...], preferred_element_type=F32)
    y = y + jnp.dot(st_ref[...].astype(BF16), o_ref[...], preferred_element_type=F32)
    y = y + d_ref[...] * x.astype(F32)
    y_ref[...] = jax.nn.gelu(y, approximate=True).astype(BF16)


def _scan(xg, tm, rm, om, lq, dvec, h0, bsz):
    g, n, _ = xg.shape
    n_chunks = n // bsz
    mat = pl.BlockSpec((None, CW, CW), lambda i: (i, 0, 0))
    return pl.pallas_call(
        functools.partial(_scan_kernel, n_chunks=n_chunks, bsz=bsz),
        grid=(g,),
        in_specs=[pl.BlockSpec((None, n, CW), lambda i: (i, 0, 0)), mat, mat, mat,
                  pl.BlockSpec((None, 2, 2 * SSM_P), lambda i: (i, 0, 0)),
                  pl.BlockSpec((None, 1, CW), lambda i: (i, 0, 0)),
                  pl.BlockSpec((None, bsz, 4 * SSM_P), lambda i: (i, 0, 0))],
        out_specs=[pl.BlockSpec((None, n, CW), lambda i: (i, 0, 0)),
                   pl.BlockSpec((None, bsz, 4 * SSM_P), lambda i: (i, 0, 0))],
        out_shape=[jax.ShapeDtypeStruct((g, n, CW), BF16),
                   jax.ShapeDtypeStruct((g, bsz, 4 * SSM_P), F32)],
        scratch_shapes=[pltpu.VMEM((n, 4 * SSM_P), F32), pltpu.VMEM((n, 4 * SSM_P), F32)],
        compiler_params=_cp("arbitrary"),
        name="s5_scan",
    )(xg, tm, rm, om, lq, dvec, h0)


def _chunk_perm():
    n = CHUNK * LANES
    src = (jnp.arange(n, dtype=jnp.int32).reshape(CHUNK, GROUPS_PER_TILE, SSM_H)
           .transpose(1, 0, 2).reshape(-1))
    return (jnp.arange(n, dtype=jnp.int32)[:, None] == src[None, :]).astype(BF16)


def _to_chunks(u, bsz, length, perm):
    nc = length // CHUNK
    z = (u.reshape(bsz, nc, CHUNK, LANE_TILES, LANES).transpose(3, 1, 0, 2, 4)
         .reshape(LANE_TILES, nc * bsz, CHUNK * LANES))
    w = jnp.einsum('jnk,kc->jnc', z, perm, preferred_element_type=BF16)
    return (w.reshape(LANE_TILES, nc * bsz, GROUPS_PER_TILE, CW).transpose(0, 2, 1, 3)
            .reshape(SSM_G, nc * bsz, CW))


def _from_chunks(y, bsz, length, perm):
    nc = length // CHUNK
    w = (y.reshape(LANE_TILES, GROUPS_PER_TILE, nc * bsz, CW).transpose(0, 2, 1, 3)
         .reshape(LANE_TILES, nc * bsz, CHUNK * LANES))
    z = jnp.einsum('jnc,kc->jnk', w, perm, preferred_element_type=BF16)
    return (z.reshape(LANE_TILES, nc, bsz, CHUNK, LANES).transpose(2, 1, 3, 0, 4)
            .reshape(bsz * length, D_SSM))


def _conv_post(cv, p_ref):
    p = p_ref[...]
    cv = cv + p[0:1]
    y = _silu(_ln(cv) * p[1:2] + p[2:3])
    return y * lax.rsqrt(jnp.mean(y * y, axis=-1, keepdims=True) + LN_EPS) * p[3:4]


def _conv_seq_tile(buf_ref, base, w_ref, b0, c0):
    win = buf_ref[pl.ds(base, CONV_WIN), b0:b0 + LANES]
    acc = jnp.zeros((GRID_W, LANES), F32)
    for ph in range(8):
        wb = win if ph == 0 else pltpu.roll(win, CONV_WIN - ph, axis=0)
        for a in range(4):
            j = 8 * a + ph - 1
            if 0 <= j < CONV_K:
                acc = acc + w_ref[j:j + 1, c0:c0 + LANES] * wb[8 * a:8 * a + GRID_W]
    return acc


def _conv_seq(buf_ref, base, w_ref, c0, width):
    return jnp.concatenate([_conv_seq_tile(buf_ref, base, w_ref, b, c0 + b)
                            for b in range(0, width, LANES)], axis=1)


def _conv_lat_kernel(z_ref, w_ref, p_ref, o_ref, hb_ref, vb_ref, cv_ref, *, rows):
    half = D_CONV // 2
    hstride = GRID_W + 16
    hb_ref[...] = jnp.zeros(hb_ref.shape, F32)
    vb_ref[pl.ds(0, CONV_PAD * GRID_W), :] = jnp.zeros((CONV_PAD * GRID_W, half), F32)
    vb_ref[pl.ds((CONV_PAD + rows) * GRID_W, CONV_PAD * GRID_W), :] = jnp.zeros((CONV_PAD * GRID_W, half), F32)
    for r in range(rows):
        hb_ref[pl.ds(16 + r * hstride, GRID_W), :] = z_ref[pl.ds(r * GRID_W, GRID_W), 0:half].astype(F32)
    vb_ref[pl.ds(CONV_PAD * GRID_W, rows * GRID_W), :] = z_ref[:, half:D_CONV].astype(F32)

    def along_row(b):
        def body(r, carry):
            hbase = pl.multiple_of(r * hstride, 16)
            vbase = pl.multiple_of(r * GRID_W, GRID_W)
            cv_ref[pl.ds(vbase, GRID_W), b:b + LANES] = _conv_seq_tile(hb_ref, hbase, w_ref, b, b)
            return carry
        lax.fori_loop(0, rows, body, 0)

    def along_col(b):
        def body(r, carry):
            vbase = pl.multiple_of(r * GRID_W, GRID_W)
            acc = jnp.zeros((GRID_W, LANES), F32)
            for j in range(CONV_K):
                acc = acc + (w_ref[j:j + 1, half + b:half + b + LANES]
                             * vb_ref[pl.ds(vbase + j * GRID_W, GRID_W), b:b + LANES])
            cv_ref[pl.ds(vbase, GRID_W), half + b:half + b + LANES] = acc
            return carry
        lax.fori_loop(0, rows, body, 0)

    for b in range(0, half, LANES):
        along_row(b)
        along_col(b)

    def post(r, carry):
        vbase = pl.multiple_of(r * GRID_W, GRID_W)
        o_ref[pl.ds(vbase, GRID_W), :] = _conv_post(cv_ref[pl.ds(vbase, GRID_W), :], p_ref).astype(BF16)
        return carry

    lax.fori_loop(0, rows, post, 0)


def _conv_lat(z, w, p, bsz, seq, blk0):
    rows = seq // GRID_W
    half = D_CONV // 2
    return pl.pallas_call(
        functools.partial(_conv_lat_kernel, rows=rows),
        grid=(bsz,),
        in_specs=[pl.BlockSpec((seq, D_CONV), lambda b: (blk0 + b, 0)),
                  pl.BlockSpec(w.shape, lambda b: (0, 0)),
                  pl.BlockSpec(p.shape, lambda b: (0, 0))],
        out_specs=pl.BlockSpec((seq, D_CONV), lambda b: (b, 0)),
        out_shape=jax.ShapeDtypeStruct((bsz * seq, D_CONV), BF16),
        scratch_shapes=[pltpu.VMEM((rows * (GRID_W + 16) + 16, half), F32),
                        pltpu.VMEM(((rows + 2 * CONV_PAD) * GRID_W, half), F32),
                        pltpu.VMEM((seq, D_CONV), F32)],
        compiler_params=_cp("arbitrary"),
        name="conv_latent",
    )(z, w, p)


def _conv_ctx_kernel(z_ref, w_ref, p_ref, o_ref, cb_ref, *, clen):
    cb_ref[pl.ds(0, 16), :] = jnp.zeros((16, D_CONV), F32)
    cb_ref[pl.ds(16 + clen, 16), :] = jnp.zeros((16, D_CONV), F32)
    cb_ref[pl.ds(16, clen), :] = z_ref[...].astype(F32)
    for blk in range(clen // GRID_W):
        base = blk * GRID_W
        acc = _conv_seq(cb_ref, base, w_ref, 0, D_CONV)
        o_ref[pl.ds(base, GRID_W), :] = _conv_post(acc, p_ref).astype(BF16)


def _conv_ctx(z, w, p, bsz, clen):
    return pl.pallas_call(
        functools.partial(_conv_ctx_kernel, clen=clen),
        grid=(bsz,),
        in_specs=[pl.BlockSpec((clen, D_CONV), lambda b: (b, 0)),
                  pl.BlockSpec(w.shape, lambda b: (0, 0)),
                  pl.BlockSpec(p.shape, lambda b: (0, 0))],
        out_specs=pl.BlockSpec((clen, D_CONV), lambda b: (b, 0)),
        out_shape=jax.ShapeDtypeStruct((bsz * clen, D_CONV), BF16),
        scratch_shapes=[pltpu.VMEM((clen + 32, D_CONV), F32)],
        compiler_params=_cp("arbitrary"),
        name="conv_context",
    )(z, w, p)


def _outproj_kernel(ysc_ref, ysl_ref, ycc_ref, ycl_ref, x_ref, mod_ref, wglu_ref, pv_ref, wo_ref, ln_ref,
                    o_ref, *, n_ctx_tiles):
    is_ctx = pl.program_id(0) < n_ctx_tiles
    ys = jnp.where(is_ctx, ysc_ref[...], ysl_ref[...])
    yc = jnp.where(is_ctx, ycc_ref[...], ycl_ref[...])
    pv = pv_ref[...]
    gl = jnp.dot(ys, wglu_ref[...], preferred_element_type=F32) + pv[0:1]
    yg = ys.astype(F32) * jax.nn.sigmoid(gl)
    yn = yg * lax.rsqrt(jnp.mean(yg * yg, axis=-1, keepdims=True) + LN_EPS) * pv[1:2]
    y = jnp.dot(yn.astype(BF16), wo_ref[0:D_SSM, :], preferred_element_type=F32)
    y = y + jnp.dot(yc, wo_ref[D_SSM:, :], preferred_element_type=F32)
    m = mod_ref[...]
    ln = ln_ref[...]
    o_ref[...] = _ln(ALPHA * x_ref[...] + m[2:3] * y) * ln[0:1] + ln[1:2]


def _outproj(ys_ctx, ys_lat, yc_ctx, yc_lat, xs, mod, wglu, pv, wo, ln, t_ctx, seq):
    t, d = xs.shape
    full = lambda a: pl.BlockSpec(a.shape, lambda i: (0, 0))
    nct = t_ctx // TMB
    ctx_tile = lambda i: (jnp.minimum(i, nct - 1), 0)
    lat_tile = lambda i: (jnp.maximum(i - nct, 0), 0)
    return pl.pallas_call(
        functools.partial(_outproj_kernel, n_ctx_tiles=nct),
        grid=(t // TMB,),
        in_specs=[pl.BlockSpec((TMB, D_SSM), ctx_tile),
                  pl.BlockSpec((TMB, D_SSM), lat_tile),
                  pl.BlockSpec((TMB, D_CONV), ctx_tile),
                  pl.BlockSpec((TMB, D_CONV), lat_tile),
                  pl.BlockSpec((TMB, d), lambda i: (i, 0)),
                  pl.BlockSpec((None, 6, d), lambda i: (_mod_row(i, t_ctx, seq, TMB), 0, 0)),
                  full(wglu), full(pv), full(wo), full(ln)],
        out_specs=pl.BlockSpec((TMB, d), lambda i: (i, 0)),
        out_shape=jax.ShapeDtypeStruct((t, d), F32),
        compiler_params=_cp("arbitrary"),
        name="outproj",
    )(ys_ctx, ys_lat, yc_ctx, yc_lat, xs, mod, wglu, pv, wo, ln)


ROW_TILE = 8


def _rows_to_tiles(ref, lead, val):
    n = val.shape[0]
    for c in range(ROW_TILE):
        ref[lead + (pl.ds(c, n, stride=ROW_TILE), slice(None))] = val[:, c * LANES:(c + 1) * LANES]


def _tiles_to_rows(ref, lead, n):
    return jnp.concatenate([ref[lead + (pl.ds(c, n, stride=ROW_TILE), slice(None))]
                            for c in range(ROW_TILE)], axis=1)


def _router_kernel(x_ref, mod_ref, wr_ref, rb_ref, h_ref, idx_ref, wt_ref, cnt_ref, carry_ref):
    @pl.when(pl.program_id(0) == 0)
    def _():
        carry_ref[...] = jnp.zeros(carry_ref.shape, F32)

    m = mod_ref[...]
    h = _ln(x_ref[...]) * (1.0 + m[4:5]) + m[3:4]
    _rows_to_tiles(h_ref, (), h)
    logits = lax.dot_general(wr_ref[...], h, (((1,), (1,)), ((), ())),
                             precision=lax.Precision.HIGHEST, preferred_element_type=F32)
    s = jax.nn.sigmoid(logits)
    biased = s + rb_ref[...]
    ninf = -jnp.inf
    tm = s.shape[1]

    row8 = lax.broadcasted_iota(jnp.int32, (GROUP_SIZE, tm), 0).astype(F32)
    gs = []
    for q in range(N_GROUPS):
        v = biased[q * GROUP_SIZE:(q + 1) * GROUP_SIZE]
        m1 = jnp.max(v, axis=0, keepdims=True)
        i1 = jnp.min(jnp.where(v == m1, row8, float(GROUP_SIZE)), axis=0, keepdims=True)
        m2 = jnp.max(jnp.where(row8 == i1, ninf, v), axis=0, keepdims=True)
        gs.append(m1 + m2)
    gsc = jnp.concatenate(gs, axis=0)
    rowg = lax.broadcasted_iota(jnp.int32, (N_GROUPS, tm), 0).astype(F32)
    gsel = jnp.zeros((N_GROUPS, tm), F32)
    for _ in range(TOP_K_GROUPS):
        mx = jnp.max(gsc, axis=0, keepdims=True)
        ii = jnp.min(jnp.where(gsc == mx, rowg, float(N_GROUPS)), axis=0, keepdims=True)
        hit = rowg == ii
        gsel = jnp.where(hit, 1.0, gsel)
        gsc = jnp.where(hit, ninf, gsc)
    emask = jnp.concatenate([jnp.broadcast_to(gsel[q:q + 1], (GROUP_SIZE, tm))
                             for q in range(N_GROUPS)], axis=0)
    masked = jnp.where(emask > 0.0, biased, ninf)

    rowe = lax.broadcasted_iota(jnp.int32, (N_EXPERTS, tm), 0).astype(F32)
    selm = jnp.zeros((N_EXPERTS, tm), F32)
    idxs, ws = [], []
    for _ in range(TOP_K):
        mx = jnp.max(masked, axis=0, keepdims=True)
        ii = jnp.min(jnp.where(masked == mx, rowe, float(N_EXPERTS)), axis=0, keepdims=True)
        hit = rowe == ii
        idxs.append(ii)
        ws.append(jnp.sum(jnp.where(hit, s, 0.0), axis=0, keepdims=True))
        selm = jnp.where(hit, 1.0, selm)
        masked = jnp.where(hit, ninf, masked)
    wsum = ws[0]
    for k in range(1, TOP_K):
        wsum = wsum + ws[k]
    idx_ref[...] = jnp.concatenate(idxs, axis=0).astype(jnp.int32)
    wt_ref[...] = jnp.concatenate([w / wsum * ROUTED_SCALE for w in ws], axis=0)

    new_carry = carry_ref[...] + jnp.sum(selm, axis=1, keepdims=True)
    carry_ref[...] = new_carry
    cnt_ref[...] = new_carry


def _router(xs, mod, wr_t, rb, t_ctx, seq):
    t, d = xs.shape
    full = lambda a: pl.BlockSpec(a.shape, lambda i: (0, 0))
    return pl.pallas_call(
        _router_kernel,
        grid=(t // TMB,),
        in_specs=[pl.BlockSpec((TMB, d), lambda i: (i, 0)),
                  pl.BlockSpec((None, 6, d), lambda i: (_mod_row(i, t_ctx, seq, TMB), 0, 0)),
                  full(wr_t), full(rb)],
        out_specs=[pl.BlockSpec((TMB * ROW_TILE, LANES), lambda i: (i, 0)),
                   pl.BlockSpec((TOP_K, TMB), lambda i: (0, i)),
                   pl.BlockSpec((TOP_K, TMB), lambda i: (0, i)),
                   pl.BlockSpec((N_EXPERTS, 1), lambda i: (0, 0))],
        out_shape=[jax.ShapeDtypeStruct((t * ROW_TILE, LANES), F32),
                   jax.ShapeDtypeStruct((TOP_K, t), jnp.int32),
                   jax.ShapeDtypeStruct((TOP_K, t), F32),
                   jax.ShapeDtypeStruct((N_EXPERTS, 1), F32)],
        scratch_shapes=[pltpu.VMEM((N_EXPERTS, 1), F32)],
        compiler_params=_cp("arbitrary"),
        name="router",
    )(xs, mod, wr_t, rb)


def _experts_kernel(te_ref, tv_ref, src0_ref, srcn_ref, dst_ref, h_ref, wg_ref, wu_ref, wd_ref,
                    y_ref, xbuf, ybuf, gsem, ssem):
    i = pl.program_id(0)
    valid = tv_ref[i] > 0
    prev_valid = tv_ref[jnp.maximum(i - 1, 0)] > 0
    rows = TME * ROW_TILE

    def gather_start(idx_ref, s):
        for r in range(TME):
            pltpu.make_async_copy(
                h_ref.at[pl.ds(pl.multiple_of(idx_ref[0, r], ROW_TILE), ROW_TILE), :],
                xbuf.at[s, pl.ds(r * ROW_TILE, ROW_TILE), :], gsem.at[s]).start()

    def scatter_start(s):
        for r in range(TME):
            pltpu.make_async_copy(
                ybuf.at[s, pl.ds(r * ROW_TILE, ROW_TILE), :],
                y_ref.at[pl.ds(pl.multiple_of(dst_ref[0, r], ROW_TILE), ROW_TILE), :],
                ssem.at[s]).start()

    def gather_wait(s):
        pltpu.make_async_copy(h_ref.at[pl.ds(0, rows), :], xbuf.at[s], gsem.at[s]).wait()

    def scatter_wait(s):
        pltpu.make_async_copy(ybuf.at[s], y_ref.at[pl.ds(0, rows), :], ssem.at[s]).wait()

    @pl.when(i == 0)
    def _():
        gather_start(src0_ref, 0)

    def step(s):
        @pl.when(valid)
        def _():
            @pl.when(i >= 2)
            def _():
                scatter_wait(s)

            gather_wait(s)
            gather_start(srcn_ref, 1 - s)
            x = _tiles_to_rows(xbuf, (s,), TME).astype(BF16)
            a = jnp.dot(x, wg_ref[...].astype(BF16), preferred_element_type=F32)
            b = jnp.dot(x, wu_ref[...].astype(BF16), preferred_element_type=F32)
            hid = (_silu(a) * b).astype(BF16)
            _rows_to_tiles(ybuf, (s,), jnp.dot(hid, wd_ref[...].astype(BF16), preferred_element_type=F32))
            scatter_start(s)

        @pl.when(jnp.logical_not(valid) & prev_valid & (i >= 1))
        def _():
            gather_wait(s)
            scatter_wait(1 - s)

            @pl.when(i >= 2)
            def _():
                scatter_wait(s)

            dump = pltpu.make_async_copy(ybuf.at[s], y_ref.at[pl.ds(y_ref.shape[0] - rows, rows), :],
                                         ssem.at[s])
            dump.start()
            dump.wait()

    for s in range(2):
        pl.when(i % 2 == s)(functools.partial(step, s))


def _experts(tile_expert, tile_valid, src, dst, h2, wg, wu, wd, n_out):
    n_tiles = src.shape[0]
    d, f = wg.shape[1:]
    smem = lambda fn: pl.BlockSpec((None, 1, TME), fn, memory_space=pltpu.SMEM)
    grid_spec = pltpu.PrefetchScalarGridSpec(
        num_scalar_prefetch=2,
        grid=(n_tiles,),
        in_specs=[smem(lambda i, te, tv: (0, 0, 0)),
                  smem(lambda i, te, tv: (jnp.minimum(i + 1, n_tiles - 1), 0, 0)),
                  smem(lambda i, te, tv: (i, 0, 0)),
                  pl.BlockSpec(memory_space=pl.ANY),
                  pl.BlockSpec((None, d, f), lambda i, te, tv: (te[i], 0, 0)),
                  pl.BlockSpec((None, d, f), lambda i, te, tv: (te[i], 0, 0)),
                  pl.BlockSpec((None, f, d), lambda i, te, tv: (te[i], 0, 0))],
        out_specs=pl.BlockSpec(memory_space=pl.ANY),
        scratch_shapes=[pltpu.VMEM((2, TME * ROW_TILE, LANES), F32),
                        pltpu.VMEM((2, TME * ROW_TILE, LANES), F32),
                        pltpu.SemaphoreType.DMA((2,)), pltpu.SemaphoreType.DMA((2,))],
    )
    return pl.pallas_call(
        _experts_kernel,
        grid_spec=grid_spec,
        out_shape=jax.ShapeDtypeStruct((n_out * ROW_TILE, LANES), F32),
        compiler_params=_cp("arbitrary"),
        name="experts",
    )(tile_expert, tile_valid, src, src, dst, h2, wg, wu, wd)


def _combine_kernel(*refs):
    yg_refs = refs[:TOP_K]
    wt_ref, h_ref, x_ref, mod_ref, wsg_ref, wsu_ref, wsd_ref, ln_ref, o_ref = refs[TOP_K:]
    wt = wt_ref[...]
    n = x_ref.shape[0]
    acc = wt[:, 0:1] * _tiles_to_rows(yg_refs[0], (), n)
    for k in range(1, TOP_K):
        acc = acc + wt[:, k:k + 1] * _tiles_to_rows(yg_refs[k], (), n)
    h = _tiles_to_rows(h_ref, (), n).astype(BF16)
    a = jnp.dot(h, wsg_ref[...], preferred_element_type=F32)
    b = jnp.dot(h, wsu_ref[...], preferred_element_type=F32)
    acc = acc + jnp.dot((_silu(a) * b).astype(BF16), wsd_ref[...], preferred_element_type=F32)
    m = mod_ref[...]
    ln = ln_ref[...]
    o_ref[...] = _ln(ALPHA * x_ref[...] + m[5:6] * acc) * ln[0:1] + ln[1:2]


def _combine(yg, wt, h2, xs, mod, wsg, wsu, wsd, ln, t_ctx, seq):
    t, d = xs.shape
    full = lambda a: pl.BlockSpec(a.shape, lambda i: (0, 0))
    n_blocks = t // TM
    slot_specs = [pl.BlockSpec((TM * ROW_TILE, LANES), functools.partial(lambda i, k: (k * n_blocks + i, 0), k=k))
                  for k in range(TOP_K)]
    return pl.pallas_call(
        _combine_kernel,
        grid=(n_blocks,),
        in_specs=slot_specs + [
                  pl.BlockSpec((TM, TOP_K), lambda i: (i, 0)),
                  pl.BlockSpec((TM * ROW_TILE, LANES), lambda i: (i, 0)),
                  pl.BlockSpec((TM, d), lambda i: (i, 0)),
                  pl.BlockSpec((None, 6, d), lambda i: (_mod_row(i, t_ctx, seq, TM), 0, 0)),
                  full(wsg), full(wsu), full(wsd), full(ln)],
        out_specs=pl.BlockSpec((TM, d), lambda i: (i, 0)),
        out_shape=jax.ShapeDtypeStruct((t, d), F32),
        compiler_params=_cp("arbitrary"),
        name="combine",
    )(*([yg] * TOP_K), wt, h2, xs, mod, wsg, wsu, wsd, ln)


def _moe(xs, mod, wr_t, rb, wg, wu, wd, wsg, wsu, wsd, ln, t_ctx, seq):
    t, d = xs.shape
    h2, idx, wt, cnt = _router(xs, mod, wr_t, rb, t_ctx, seq)
    n_pairs = t * TOP_K
    n_rows = n_pairs + N_EXPERTS * TME
    n_tiles = n_rows // TME
    assert n_rows < (1 << KEY_SHIFT) and n_pairs % TME == 0
    counts = cnt[:, 0].astype(jnp.int32)
    n_padding = (-counts) % TME
    slot = jnp.arange(TME, dtype=jnp.int32)[None, :]
    pad_key = jnp.where(slot < n_padding[:, None], jnp.arange(N_EXPERTS, dtype=jnp.int32)[:, None],
                        N_EXPERTS).reshape(-1)
    keys = jnp.concatenate([idx.reshape(-1), pad_key]) * (1 << KEY_SHIFT) + jnp.arange(n_rows, dtype=jnp.int32)
    order = lax.sort(keys)
    pair = order & ((1 << KEY_SHIFT) - 1)
    src = ((pair % t) * ROW_TILE).reshape(n_tiles, 1, TME)
    dst = (jnp.where(pair < n_pairs, pair, n_pairs + pair % TME) * ROW_TILE).reshape(n_tiles, 1, TME)
    first = order[::TME] >> KEY_SHIFT
    tile_expert = jnp.minimum(first, N_EXPERTS - 1)
    tile_valid = (first < N_EXPERTS).astype(jnp.int32)
    ye = _experts(tile_expert, tile_valid, src, dst, h2, wg, wu, wd, n_pairs + TME)
    return _combine(ye, wt.T, h2, xs, mod, wsg, wsu, wsd, ln, t_ctx, seq)


def kernel(x, c, ctx, c_ctx, w_ada, b_ada, w_in, ssm_a_re, ssm_a_im, ssm_log_dt, ssm_b_re, ssm_b_im,
           ssm_c_re, ssm_c_im, ssm_d, w_glu, b_glu, conv_w, conv_b, conv_ln_g, conv_ln_b, mix_norm_g,
           w_out, ln1_g, ln1_b, w_router, router_bias, we_gate, we_up, we_down, ws_gate, ws_up,
           ws_down, ln2_g, ln2_b):
    bsz, seq, d = x.shape
    clen = ctx.shape[1]
    depth = w_in.shape[0]
    t_ctx, t_lat = bsz * clen, bsz * seq
    assert d == D_MODEL and bsz + 1 <= MOD_ROWS
    assert seq % GRID_W == 0 and clen % GRID_W == 0 and t_ctx % seq == 0
    assert seq % TMB == 0 and t_ctx % TMB == 0 and seq % TM == 0 and t_ctx % TM == 0

    xs = jnp.concatenate([ctx.reshape(t_ctx, d), x.reshape(t_lat, d)], axis=0)
    cc = jnp.zeros((MOD_ROWS, d), F32).at[0].set(c_ctx).at[1:bsz + 1].set(c)
    mod_all = _ada(cc, w_ada, b_ada).reshape(depth, MOD_ROWS, 6, d)
    perm = _chunk_perm()
    s5 = _s5_mats(ssm_a_re, ssm_a_im, ssm_log_dt, ssm_b_re, ssm_b_im, ssm_c_re, ssm_c_im, ssm_d)

    for l in range(depth):
        mod = mod_all[l]
        u, z = _inproj(xs, mod, w_in[l].astype(BF16), t_ctx, seq)

        tm, rm, om, lq, dvec = (a[l] for a in s5)
        h0 = jnp.zeros((SSM_G, bsz, 4 * SSM_P), F32)
        yc, fin = _scan(_to_chunks(u[:t_ctx], bsz, clen, perm), tm, rm, om, lq, dvec, h0, bsz)
        yl, _ = _scan(_to_chunks(u[t_ctx:], bsz, seq, perm), tm, rm, om, lq, dvec, fin, bsz)
        cp = jnp.stack([conv_b[l], conv_ln_g[l], conv_ln_b[l], mix_norm_g[l, D_SSM:]])
        pv = jnp.stack([b_glu[l], mix_norm_g[l, :D_SSM]])
        xs = _outproj(_from_chunks(yc, bsz, clen, perm), _from_chunks(yl, bsz, seq, perm),
                      _conv_ctx(z, conv_w[l], cp, bsz, clen),
                      _conv_lat(z, conv_w[l], cp, bsz, seq, t_ctx // seq),
                      xs, mod, w_glu[l].astype(BF16), pv, w_out[l].astype(BF16),
                      jnp.stack([ln1_g[l], ln1_b[l]]), t_ctx, seq)

        xs = _moe(xs, mod, w_router[l].T, router_bias[l][:, None], we_gate[l], we_up[l], we_down[l],
                  ws_gate[l].astype(BF16), ws_up[l].astype(BF16), ws_down[l].astype(BF16),
                  jnp.stack([ln2_g[l], ln2_b[l]]), t_ctx, seq)

    return xs[t_ctx:].reshape(bsz, seq, d)
```

```python
import functools
import math

import jax
import jax.numpy as jnp
from jax import lax
from jax.experimental import pallas as pl
from jax.experimental.pallas import tpu as pltpu

F32 = jnp.float32
BF16 = jnp.bfloat16

D_MODEL = 1024
DEPTH = 4
GRID_W = 64
D_SSM = 512
D_CONV = 512
SSM_H = 16
SSM_G = 32
SSM_P = 64
CONV_K = 31
CONV_PAD = 15
CONV_WIN = GRID_W + 32
N_EXPERTS = 64
TOP_K = 8
N_GROUPS = 8
TOP_K_GROUPS = 4
GROUP_SIZE = N_EXPERTS // N_GROUPS
D_EXPERT = 256
ROUTED_SCALE = 2.5
ALPHA = (2 * DEPTH) ** 0.25
LN_EPS = 1e-5

LANES = 128
CHUNK = 16
CW = CHUNK * SSM_H
LANE_TILES = D_SSM // LANES
GROUPS_PER_TILE = LANES // SSM_H
MOD_ROWS = 32
TM = 256
TMB = 512
TME = 512
KEY_SHIFT = 19
VMEM_LIMIT = 48 * 1024 * 1024


def _cp(*sem):
    return pltpu.CompilerParams(dimension_semantics=sem, vmem_limit_bytes=VMEM_LIMIT)


def _ln(x):
    mu = jnp.mean(x, axis=-1, keepdims=True)
    xc = x - mu
    var = jnp.mean(xc * xc, axis=-1, keepdims=True)
    return xc * lax.rsqrt(var + LN_EPS)


def _silu(x):
    return x * jax.nn.sigmoid(x)


def _ada_kernel(c_ref, w_ref, b_ref, o_ref):
    a = _silu(c_ref[...]).astype(BF16)
    o_ref[...] = jnp.dot(a, w_ref[...].astype(BF16), preferred_element_type=F32) + b_ref[...]


def _ada(cc, w_ada, b_ada):
    depth, d, n = w_ada.shape
    tn = 1536
    return pl.pallas_call(
        _ada_kernel,
        grid=(depth, n // tn),
        in_specs=[pl.BlockSpec((MOD_ROWS, d), lambda l, j: (0, 0)),
                  pl.BlockSpec((None, d, tn), lambda l, j: (l, 0, j)),
                  pl.BlockSpec((None, 1, tn), lambda l, j: (l, 0, j))],
        out_specs=pl.BlockSpec((None, MOD_ROWS, tn), lambda l, j: (l, 0, j)),
        out_shape=jax.ShapeDtypeStruct((depth, MOD_ROWS, n), F32),
        compiler_params=_cp("arbitrary", "arbitrary"),
        name="ada",
    )(cc, w_ada, b_ada.reshape(depth, 1, n))


def _inproj_kernel(x_ref, mod_ref, w_ref, u_ref, z_ref):
    m = mod_ref[...]
    h = _ln(x_ref[...]) * (1.0 + m[1:2]) + m[0:1]
    p = jnp.dot(h.astype(BF16), w_ref[...], preferred_element_type=F32)
    u_ref[...] = p[:, :D_SSM].astype(BF16)
    v = p[:, D_SSM:D_SSM + D_CONV]
    g = p[:, D_SSM + D_CONV:]
    z_ref[...] = (v * jax.nn.sigmoid(g)).astype(BF16)


def _mod_row(i, t_ctx, seq, tile):
    start = i * tile
    return jnp.where(start < t_ctx, 0, 1 + (start - t_ctx) // seq)


def _inproj(xs, mod, w, t_ctx, seq):
    t, d = xs.shape
    return pl.pallas_call(
        _inproj_kernel,
        grid=(t // TMB,),
        in_specs=[pl.BlockSpec((TMB, d), lambda i: (i, 0)),
                  pl.BlockSpec((None, 6, d), lambda i: (_mod_row(i, t_ctx, seq, TMB), 0, 0)),
                  pl.BlockSpec(w.shape, lambda i: (0, 0))],
        out_specs=[pl.BlockSpec((TMB, D_SSM), lambda i: (i, 0)),
                   pl.BlockSpec((TMB, D_CONV), lambda i: (i, 0))],
        out_shape=[jax.ShapeDtypeStruct((t, D_SSM), BF16),
                   jax.ShapeDtypeStruct((t, D_CONV), BF16)],
        compiler_params=_cp("arbitrary"),
        name="inproj",
    )(xs, mod, w)


def _s5_mats(a_re, a_im, log_dt, b_re, b_im, c_re, c_im, d_skip):
    hp = lax.Precision.HIGHEST
    q = CHUNK
    nl = a_re.shape[0]
    dt = jnp.exp(log_dt)[..., None]
    ldr, ldi = a_re * dt, a_im * dt
    ks = jnp.arange(-(q - 1), q + 1, dtype=F32)[:, None]
    mag = jnp.exp(ks * ldr[..., None, :])
    ang = ks * ldi[..., None, :]
    pr, pi = mag * jnp.cos(ang), mag * jnp.sin(ang)

    def powers(d, lo, hi, rev=False):
        r, i = pr[:, d, :, lo + q - 1:hi + q - 1], pi[:, d, :, lo + q - 1:hi + q - 1]
        return (r[:, :, ::-1], i[:, :, ::-1]) if rev else (r, i)

    nr, ni = pr[..., q, :] - 1.0, pi[..., q, :]
    den = a_re * a_re + a_im * a_im
    qr, qi = (nr * a_re + ni * a_im) / den, (ni * a_re - nr * a_im) / den
    bbr = qr[..., None] * b_re - qi[..., None] * b_im
    bbi = qr[..., None] * b_im + qi[..., None] * b_re

    def c_times(d, pw):
        cr, ci = c_re[:, d][:, :, None], c_im[:, d][:, :, None]
        r, i = pw[0][:, :, :, None, :], pw[1][:, :, :, None, :]
        return cr * r - ci * i, cr * i + ci * r

    def b_times(d, pw):
        br, bi = bbr[:, d][:, :, :, None, :], bbi[:, d][:, :, :, None, :]
        r, i = pw[0].swapaxes(2, 3)[..., None], pw[1].swapaxes(2, 3)[..., None]
        return r * br - i * bi, r * bi + i * br

    def b_times_t(d, pw):
        br, bi = bbr[:, d].swapaxes(-1, -2)[:, :, None], bbi[:, d].swapaxes(-1, -2)[:, :, None]
        r, i = pw[0][:, :, :, None, :], pw[1][:, :, :, None, :]
        return r * br - i * bi, r * bi + i * br

    def lag_kernel(d, pw_t, pw_s):
        ur, ui = c_times(d, pw_t)
        vr, vi = b_times(d, pw_s)
        ur, ui = ur.reshape(nl, SSM_G, CW, SSM_P), ui.reshape(nl, SSM_G, CW, SSM_P)
        vr, vi = vr.reshape(nl, SSM_G, SSM_P, CW), vi.reshape(nl, SSM_G, SSM_P, CW)
        return (jnp.einsum('lgpa,lgbp->lgab', vr, ur, precision=hp)
                - jnp.einsum('lgpa,lgbp->lgab', vi, ui, precision=hp))

    kf = lag_kernel(0, powers(0, 0, q), powers(0, -(q - 1), 1, rev=True))
    kb = lag_kernel(1, powers(1, -(q - 1), 1, rev=True), powers(1, 0, q))
    s_i = (jnp.arange(CW) // SSM_H)[:, None]
    t_i = (jnp.arange(CW) // SSM_H)[None, :]
    tm = jnp.where(t_i >= s_i, kf, 0.0) + jnp.where(s_i >= t_i, kb, 0.0)

    f_re, f_im = b_times_t(0, powers(0, 0, q, rev=True))
    g_re, g_im = b_times_t(1, powers(1, 0, q))
    rm = jnp.concatenate([f_re, g_re, f_im, g_im], axis=-1).reshape(nl, SSM_G, CW, 4 * SSM_P)

    of_r, of_i = c_times(0, powers(0, 1, q + 1))
    ob_r, ob_i = c_times(1, powers(1, 1, q + 1, rev=True))
    fix_o = lambda a: a.transpose(0, 1, 4, 2, 3).reshape(nl, SSM_G, SSM_P, CW)
    om = jnp.concatenate([fix_o(of_r), fix_o(ob_r), fix_o(-of_i), fix_o(-ob_i)], axis=2)

    last = 2 * q - 1
    lq = jnp.stack([jnp.concatenate([pr[:, 0, :, last], pr[:, 1, :, last]], -1),
                    jnp.concatenate([pi[:, 0, :, last], pi[:, 1, :, last]], -1)], axis=2)
    dvec = jnp.tile(d_skip[:, :, None, :], (1, 1, q, 1)).reshape(nl, SSM_G, 1, CW)
    return tm.astype(BF16), rm.astype(BF16), om.astype(BF16), lq, dvec


def _scan_kernel(x_ref, t_ref, r_ref, o_ref, lq_ref, d_ref, h0_ref, y_ref, fin_ref,
                 rr_ref, st_ref, *, n_chunks, bsz):
    p2 = 2 * SSM_P
    x = x_ref[...]
    rr_ref[...] = jnp.dot(x, r_ref[...], preferred_element_type=F32)
    lq = lq_ref[...]
    lre, lim = lq[0:1], lq[1:2]
    is_f = lax.broadcasted_iota(jnp.int32, (bsz, p2), 1) < SSM_P
    h0 = h0_ref[...]

    def step(j, carry):
        sre, sim = carry
        rf = pl.multiple_of(j * bsz, bsz)
        rb = pl.multiple_of((n_chunks - 1 - j) * bsz, bsz)
        st_ref[pl.ds(rf, bsz), 0:SSM_P] = sre[:, 0:SSM_P]
        st_ref[pl.ds(rb, bsz), SSM_P:p2] = sre[:, SSM_P:p2]
        st_ref[pl.ds(rf, bsz), p2:p2 + SSM_P] = sim[:, 0:SSM_P]
        st_ref[pl.ds(rb, bsz), p2 + SSM_P:2 * p2] = sim[:, SSM_P:p2]
        r_re = jnp.where(is_f, rr_ref[pl.ds(rf, bsz), 0:p2], rr_ref[pl.ds(rb, bsz), 0:p2])
        r_im = jnp.where(is_f, rr_ref[pl.ds(rf, bsz), p2:2 * p2], rr_ref[pl.ds(rb, bsz), p2:2 * p2])
        return lre * sre - lim * sim + r_re, lre * sim + lim * sre + r_im

    sre, sim = lax.fori_loop(0, n_chunks, step, (h0[:, 0:p2], h0[:, p2:2 * p2]))
    fin_ref[...] = jnp.concatenate([sre, sim], axis=1)
    y = jnp.dot(x, t_ref[...], preferred_element_type=F32)
    y = y + jnp.dot(st_ref[...].astype(BF16), o_ref[...], preferred_element_type=F32)
    y = y + d_ref[...] * x.astype(F32)
    y_ref[...] = jax.nn.gelu(y, approximate=True).astype(BF16)


def _scan(xg, tm, rm, om, lq, dvec, h0, bsz):
    g, n, _ = xg.shape
    n_chunks = n // bsz
    mat = pl.BlockSpec((None, CW, CW), lambda i: (i, 0, 0))
    return pl.pallas_call(
        functools.partial(_scan_kernel, n_chunks=n_chunks, bsz=bsz),
        grid=(g,),
        in_specs=[pl.BlockSpec((None, n, CW), lambda i: (i, 0, 0)), mat, mat, mat,
                  pl.BlockSpec((None, 2, 2 * SSM_P), lambda i: (i, 0, 0)),
                  pl.BlockSpec((None, 1, CW), lambda i: (i, 0, 0)),
                  pl.BlockSpec((None, bsz, 4 * SSM_P), lambda i: (i, 0, 0))],
        out_specs=[pl.BlockSpec((None, n, CW), lambda i: (i, 0, 0)),
                   pl.BlockSpec((None, bsz, 4 * SSM_P), lambda i: (i, 0, 0))],
        out_shape=[jax.ShapeDtypeStruct((g, n, CW), BF16),
                   jax.ShapeDtypeStruct((g, bsz, 4 * SSM_P), F32)],
        scratch_shapes=[pltpu.VMEM((n, 4 * SSM_P), F32), pltpu.VMEM((n, 4 * SSM_P), F32)],
        compiler_params=_cp("arbitrary"),
        name="s5_scan",
    )(xg, tm, rm, om, lq, dvec, h0)


def _chunk_perm():
    n = CHUNK * LANES
    src = (jnp.arange(n, dtype=jnp.int32).reshape(CHUNK, GROUPS_PER_TILE, SSM_H)
           .transpose(1, 0, 2).reshape(-1))
    return (jnp.arange(n, dtype=jnp.int32)[:, None] == src[None, :]).astype(BF16)


def _to_chunks(u, bsz, length, perm):
    nc = length // CHUNK
    z = (u.reshape(bsz, nc, CHUNK, LANE_TILES, LANES).transpose(3, 1, 0, 2, 4)
         .reshape(LANE_TILES, nc * bsz, CHUNK * LANES))
    w = jnp.einsum('jnk,kc->jnc', z, perm, preferred_element_type=BF16)
    return (w.reshape(LANE_TILES, nc * bsz, GROUPS_PER_TILE, CW).transpose(0, 2, 1, 3)
            .reshape(SSM_G, nc * bsz, CW))


def _from_chunks(y, bsz, length, perm):
    nc = length // CHUNK
    w = (y.reshape(LANE_TILES, GROUPS_PER_TILE, nc * bsz, CW).transpose(0, 2, 1, 3)
         .reshape(LANE_TILES, nc * bsz, CHUNK * LANES))
    z = jnp.einsum('jnc,kc->jnk', w, perm, preferred_element_type=BF16)
    return (z.reshape(LANE_TILES, nc, bsz, CHUNK, LANES).transpose(2, 1, 3, 0, 4)
            .reshape(bsz * length, D_SSM))


def _conv_post(cv, p_ref):
    p = p_ref[...]
    cv = cv + p[0:1]
    y = _silu(_ln(cv) * p[1:2] + p[2:3])
    return y * lax.rsqrt(jnp.mean(y * y, axis=-1, keepdims=True) + LN_EPS) * p[3:4]


def _conv_seq_tile(buf_ref, base, w_ref, b0, c0):
    win = buf_ref[pl.ds(base, CONV_WIN), b0:b0 + LANES]
    acc = jnp.zeros((GRID_W, LANES), F32)
    for ph in range(8):
        wb = win if ph == 0 else pltpu.roll(win, CONV_WIN - ph, axis=0)
        for a in range(4):
            j = 8 * a + ph - 1
            if 0 <= j < CONV_K:
                acc = acc + w_ref[j:j + 1, c0:c0 + LANES] * wb[8 * a:8 * a + GRID_W]
    return acc


def _conv_seq(buf_ref, base, w_ref, c0, width):
    return jnp.concatenate([_conv_seq_tile(buf_ref, base, w_ref, b, c0 + b)
                            for b in range(0, width, LANES)], axis=1)


def _conv_lat_kernel(z_ref, w_ref, p_ref, o_ref, hb_ref, vb_ref, cv_ref, *, rows):
    half = D_CONV // 2
    hstride = GRID_W + 16
    hb_ref[...] = jnp.zeros(hb_ref.shape, F32)
    vb_ref[pl.ds(0, CONV_PAD * GRID_W), :] = jnp.zeros((CONV_PAD * GRID_W, half), F32)
    vb_ref[pl.ds((CONV_PAD + rows) * GRID_W, CONV_PAD * GRID_W), :] = jnp.zeros((CONV_PAD * GRID_W, half), F32)
    for r in range(rows):
        hb_ref[pl.ds(16 + r * hstride, GRID_W), :] = z_ref[pl.ds(r * GRID_W, GRID_W), 0:half].astype(F32)
    vb_ref[pl.ds(CONV_PAD * GRID_W, rows * GRID_W), :] = z_ref[:, half:D_CONV].astype(F32)

    def along_row(b):
        def body(r, carry):
            hbase = pl.multiple_of(r * hstride, 16)
            vbase = pl.multiple_of(r * GRID_W, GRID_W)
            cv_ref[pl.ds(vbase, GRID_W), b:b + LANES] = _conv_seq_tile(hb_ref, hbase, w_ref, b, b)
            return carry
        lax.fori_loop(0, rows, body, 0)

    def along_col(b):
        def body(r, carry):
            vbase = pl.multiple_of(r * GRID_W, GRID_W)
            acc = jnp.zeros((GRID_W, LANES), F32)
            for j in range(CONV_K):
                acc = acc + (w_ref[j:j + 1, half + b:half + b + LANES]
                             * vb_ref[pl.ds(vbase + j * GRID_W, GRID_W), b:b + LANES])
            cv_ref[pl.ds(vbase, GRID_W), half + b:half + b + LANES] = acc
            return carry
        lax.fori_loop(0, rows, body, 0)

    for b in range(0, half, LANES):
        along_row(b)
        along_col(b)

    def post(r, carry):
        vbase = pl.multiple_of(r * GRID_W, GRID_W)
        o_ref[pl.ds(vbase, GRID_W), :] = _conv_post(cv_ref[pl.ds(vbase, GRID_W), :], p_ref).astype(BF16)
        return carry

    lax.fori_loop(0, rows, post, 0, unroll=4)


def _conv_lat(z, w, p, bsz, seq, blk0):
    rows = seq // GRID_W
    half = D_CONV // 2
    return pl.pallas_call(
        functools.partial(_conv_lat_kernel, rows=rows),
        grid=(bsz,),
        in_specs=[pl.BlockSpec((seq, D_CONV), lambda b: (blk0 + b, 0)),
                  pl.BlockSpec(w.shape, lambda b: (0, 0)),
                  pl.BlockSpec(p.shape, lambda b: (0, 0))],
        out_specs=pl.BlockSpec((seq, D_CONV), lambda b: (b, 0)),
        out_shape=jax.ShapeDtypeStruct((bsz * seq, D_CONV), BF16),
        scratch_shapes=[pltpu.VMEM((rows * (GRID_W + 16) + 16, half), F32),
                        pltpu.VMEM(((rows + 2 * CONV_PAD) * GRID_W, half), F32),
                        pltpu.VMEM((seq, D_CONV), F32)],
        compiler_params=_cp("arbitrary"),
        name="conv_latent",
    )(z, w, p)


def _conv_ctx_kernel(z_ref, w_ref, p_ref, o_ref, cb_ref, *, clen):
    cb_ref[pl.ds(0, 16), :] = jnp.zeros((16, D_CONV), F32)
    cb_ref[pl.ds(16 + clen, 16), :] = jnp.zeros((16, D_CONV), F32)
    cb_ref[pl.ds(16, clen), :] = z_ref[...].astype(F32)
    for blk in range(clen // GRID_W):
        base = blk * GRID_W
        acc = _conv_seq(cb_ref, base, w_ref, 0, D_CONV)
        o_ref[pl.ds(base, GRID_W), :] = _conv_post(acc, p_ref).astype(BF16)


def _conv_ctx(z, w, p, bsz, clen):
    return pl.pallas_call(
        functools.partial(_conv_ctx_kernel, clen=clen),
        grid=(bsz,),
        in_specs=[pl.BlockSpec((clen, D_CONV), lambda b: (b, 0)),
                  pl.BlockSpec(w.shape, lambda b: (0, 0)),
                  pl.BlockSpec(p.shape, lambda b: (0, 0))],
        out_specs=pl.BlockSpec((clen, D_CONV), lambda b: (b, 0)),
        out_shape=jax.ShapeDtypeStruct((bsz * clen, D_CONV), BF16),
        scratch_shapes=[pltpu.VMEM((clen + 32, D_CONV), F32)],
        compiler_params=_cp("arbitrary"),
        name="conv_context",
    )(z, w, p)


def _outproj_kernel(ysc_ref, ysl_ref, ycc_ref, ycl_ref, x_ref, mod_ref, wglu_ref, pv_ref, wo_ref, ln_ref,
                    o_ref, *, n_ctx_tiles):
    is_ctx = pl.program_id(0) < n_ctx_tiles
    ys = jnp.where(is_ctx, ysc_ref[...], ysl_ref[...])
    yc = jnp.where(is_ctx, ycc_ref[...], ycl_ref[...])
    pv = pv_ref[...]
    gl = jnp.dot(ys, wglu_ref[...], preferred_element_type=F32) + pv[0:1]
    yg = ys.astype(F32) * jax.nn.sigmoid(gl)
    yn = yg * lax.rsqrt(jnp.mean(yg * yg, axis=-1, keepdims=True) + LN_EPS) * pv[1:2]
    y = jnp.dot(yn.astype(BF16), wo_ref[0:D_SSM, :], preferred_element_type=F32)
    y = y + jnp.dot(yc, wo_ref[D_SSM:, :], preferred_element_type=F32)
    m = mod_ref[...]
    ln = ln_ref[...]
    o_ref[...] = _ln(ALPHA * x_ref[...] + m[2:3] * y) * ln[0:1] + ln[1:2]


def _outproj(ys_ctx, ys_lat, yc_ctx, yc_lat, xs, mod, wglu, pv, wo, ln, t_ctx, seq):
    t, d = xs.shape
    full = lambda a: pl.BlockSpec(a.shape, lambda i: (0, 0))
    nct = t_ctx // TMB
    ctx_tile = lambda i: (jnp.minimum(i, nct - 1), 0)
    lat_tile = lambda i: (jnp.maximum(i - nct, 0), 0)
    return pl.pallas_call(
        functools.partial(_outproj_kernel, n_ctx_tiles=nct),
        grid=(t // TMB,),
        in_specs=[pl.BlockSpec((TMB, D_SSM), ctx_tile),
                  pl.BlockSpec((TMB, D_SSM), lat_tile),
                  pl.BlockSpec((TMB, D_CONV), ctx_tile),
                  pl.BlockSpec((TMB, D_CONV), lat_tile),
                  pl.BlockSpec((TMB, d), lambda i: (i, 0)),
                  pl.BlockSpec((None, 6, d), lambda i: (_mod_row(i, t_ctx, seq, TMB), 0, 0)),
                  full(wglu), full(pv), full(wo), full(ln)],
        out_specs=pl.BlockSpec((TMB, d), lambda i: (i, 0)),
        out_shape=jax.ShapeDtypeStruct((t, d), F32),
        compiler_params=_cp("arbitrary"),
        name="outproj",
    )(ys_ctx, ys_lat, yc_ctx, yc_lat, xs, mod, wglu, pv, wo, ln)


ROW_TILE = 8


def _rows_to_tiles(ref, lead, val):
    n = val.shape[0]
    for c in range(ROW_TILE):
        ref[lead + (pl.ds(c, n, stride=ROW_TILE), slice(None))] = val[:, c * LANES:(c + 1) * LANES]


def _tiles_to_rows(ref, lead, n):
    return jnp.concatenate([ref[lead + (pl.ds(c, n, stride=ROW_TILE), slice(None))]
                            for c in range(ROW_TILE)], axis=1)


def _router_kernel(x_ref, mod_ref, wr_ref, rb_ref, h_ref, idx_ref, wt_ref, cnt_ref, carry_ref):
    @pl.when(pl.program_id(0) == 0)
    def _():
        carry_ref[...] = jnp.zeros(carry_ref.shape, F32)

    m = mod_ref[...]
    h = _ln(x_ref[...]) * (1.0 + m[4:5]) + m[3:4]
    _rows_to_tiles(h_ref, (), h)
    logits = lax.dot_general(wr_ref[...], h, (((1,), (1,)), ((), ())),
                             precision=lax.Precision.HIGHEST, preferred_element_type=F32)
    carry = carry_ref[...]
    for c0 in range(0, logits.shape[1], LANES):
        idx, wts, selm = _select_experts(logits[:, c0:c0 + LANES], rb_ref[...])
        idx_ref[:, c0:c0 + LANES] = idx
        wt_ref[:, c0:c0 + LANES] = wts
        carry = carry + jnp.sum(selm, axis=1, keepdims=True)
    carry_ref[...] = carry
    cnt_ref[...] = carry


def _select_experts(logits, bias):
    s = jax.nn.sigmoid(logits)
    biased = s + bias
    ninf = -jnp.inf
    tm = s.shape[1]

    row8 = lax.broadcasted_iota(jnp.int32, (GROUP_SIZE, tm), 0).astype(F32)
    gs = []
    for q in range(N_GROUPS):
        v = biased[q * GROUP_SIZE:(q + 1) * GROUP_SIZE]
        m1 = jnp.max(v, axis=0, keepdims=True)
        i1 = jnp.min(jnp.where(v == m1, row8, float(GROUP_SIZE)), axis=0, keepdims=True)
        m2 = jnp.max(jnp.where(row8 == i1, ninf, v), axis=0, keepdims=True)
        gs.append(m1 + m2)
    gsc = jnp.concatenate(gs, axis=0)
    rowg = lax.broadcasted_iota(jnp.int32, (N_GROUPS, tm), 0).astype(F32)
    gsel = jnp.zeros((N_GROUPS, tm), F32)
    for _ in range(TOP_K_GROUPS):
        mx = jnp.max(gsc, axis=0, keepdims=True)
        ii = jnp.min(jnp.where(gsc == mx, rowg, float(N_GROUPS)), axis=0, keepdims=True)
        hit = rowg == ii
        gsel = jnp.where(hit, 1.0, gsel)
        gsc = jnp.where(hit, ninf, gsc)
    emask = jnp.concatenate([jnp.broadcast_to(gsel[q:q + 1], (GROUP_SIZE, tm))
                             for q in range(N_GROUPS)], axis=0)
    masked = jnp.where(emask > 0.0, biased, ninf)

    rowe = lax.broadcasted_iota(jnp.int32, (N_EXPERTS, tm), 0).astype(F32)
    selm = jnp.zeros((N_EXPERTS, tm), F32)
    idxs, ws = [], []
    for _ in range(TOP_K):
        mx = jnp.max(masked, axis=0, keepdims=True)
        ii = jnp.min(jnp.where(masked == mx, rowe, float(N_EXPERTS)), axis=0, keepdims=True)
        hit = rowe == ii
        idxs.append(ii)
        ws.append(jnp.sum(jnp.where(hit, s, 0.0), axis=0, keepdims=True))
        selm = jnp.where(hit, 1.0, selm)
        masked = jnp.where(hit, ninf, masked)
    wsum = ws[0]
    for k in range(1, TOP_K):
        wsum = wsum + ws[k]
    return (jnp.concatenate(idxs, axis=0).astype(jnp.int32),
            jnp.concatenate([w / wsum * ROUTED_SCALE for w in ws], axis=0), selm)


def _router(xs, mod, wr_t, rb, t_ctx, seq):
    t, d = xs.shape
    full = lambda a: pl.BlockSpec(a.shape, lambda i: (0, 0))
    return pl.pallas_call(
        _router_kernel,
        grid=(t // TMB,),
        in_specs=[pl.BlockSpec((TMB, d), lambda i: (i, 0)),
                  pl.BlockSpec((None, 6, d), lambda i: (_mod_row(i, t_ctx, seq, TMB), 0, 0)),
                  full(wr_t), full(rb)],
        out_specs=[pl.BlockSpec((TMB * ROW_TILE, LANES), lambda i: (i, 0)),
                   pl.BlockSpec((TOP_K, TMB), lambda i: (0, i)),
                   pl.BlockSpec((TOP_K, TMB), lambda i: (0, i)),
                   pl.BlockSpec((N_EXPERTS, 1), lambda i: (0, 0))],
        out_shape=[jax.ShapeDtypeStruct((t * ROW_TILE, LANES), F32),
                   jax.ShapeDtypeStruct((TOP_K, t), jnp.int32),
                   jax.ShapeDtypeStruct((TOP_K, t), F32),
                   jax.ShapeDtypeStruct((N_EXPERTS, 1), F32)],
        scratch_shapes=[pltpu.VMEM((N_EXPERTS, 1), F32)],
        compiler_params=_cp("arbitrary"),
        name="router",
    )(xs, mod, wr_t, rb)


def _experts_kernel(te_ref, tv_ref, src0_ref, srcn_ref, dst_ref, h_ref, wg_ref, wu_ref, wd_ref,
                    y_ref, xbuf, ybuf, gsem, ssem):
    i = pl.program_id(0)
    valid = tv_ref[i] > 0
    prev_valid = tv_ref[jnp.maximum(i - 1, 0)] > 0
    rows = TME * ROW_TILE

    def gather_start(idx_ref, s):
        for r in range(TME):
            pltpu.make_async_copy(
                h_ref.at[pl.ds(pl.multiple_of(idx_ref[0, r], ROW_TILE), ROW_TILE), :],
                xbuf.at[s, pl.ds(r * ROW_TILE, ROW_TILE), :], gsem.at[s]).start()

    def scatter_start(s):
        for r in range(TME):
            pltpu.make_async_copy(
                ybuf.at[s, pl.ds(r * ROW_TILE, ROW_TILE), :],
                y_ref.at[pl.ds(pl.multiple_of(dst_ref[0, r], ROW_TILE), ROW_TILE), :],
                ssem.at[s]).start()

    def gather_wait(s):
        pltpu.make_async_copy(h_ref.at[pl.ds(0, rows), :], xbuf.at[s], gsem.at[s]).wait()

    def scatter_wait(s):
        pltpu.make_async_copy(ybuf.at[s], y_ref.at[pl.ds(0, rows), :], ssem.at[s]).wait()

    @pl.when(i == 0)
    def _():
        gather_start(src0_ref, 0)

    def step(s):
        @pl.when(valid)
        def _():
            @pl.when(i >= 2)
            def _():
                scatter_wait(s)

            gather_wait(s)
            gather_start(srcn_ref, 1 - s)
            x = _tiles_to_rows(xbuf, (s,), TME).astype(BF16)
            a = jnp.dot(x, wg_ref[...].astype(BF16), preferred_element_type=F32)
            b = jnp.dot(x, wu_ref[...].astype(BF16), preferred_element_type=F32)
            hid = (_silu(a) * b).astype(BF16)
            _rows_to_tiles(ybuf, (s,), jnp.dot(hid, wd_ref[...].astype(BF16), preferred_element_type=F32))
            scatter_start(s)

        @pl.when(jnp.logical_not(valid) & prev_valid & (i >= 1))
        def _():
            gather_wait(s)
            scatter_wait(1 - s)

            @pl.when(i >= 2)
            def _():
                scatter_wait(s)

            dump = pltpu.make_async_copy(ybuf.at[s], y_ref.at[pl.ds(y_ref.shape[0] - rows, rows), :],
                                         ssem.at[s])
            dump.start()
            dump.wait()

    for s in range(2):
        pl.when(i % 2 == s)(functools.partial(step, s))


def _experts(tile_expert, tile_valid, src, dst, h2, wg, wu, wd, layer, n_out):
    n_tiles = src.shape[0]
    d, f = wg.shape[2:]
    smem = lambda fn: pl.BlockSpec((None, 1, TME), fn, memory_space=pltpu.SMEM)
    grid_spec = pltpu.PrefetchScalarGridSpec(
        num_scalar_prefetch=2,
        grid=(n_tiles,),
        in_specs=[smem(lambda i, te, tv: (0, 0, 0)),
                  smem(lambda i, te, tv: (jnp.minimum(i + 1, n_tiles - 1), 0, 0)),
                  smem(lambda i, te, tv: (i, 0, 0)),
                  pl.BlockSpec(memory_space=pl.ANY),
                  pl.BlockSpec((None, None, d, f), lambda i, te, tv: (layer, te[i], 0, 0)),
                  pl.BlockSpec((None, None, d, f), lambda i, te, tv: (layer, te[i], 0, 0)),
                  pl.BlockSpec((None, None, f, d), lambda i, te, tv: (layer, te[i], 0, 0))],
        out_specs=pl.BlockSpec(memory_space=pl.ANY),
        scratch_shapes=[pltpu.VMEM((2, TME * ROW_TILE, LANES), F32),
                        pltpu.VMEM((2, TME * ROW_TILE, LANES), F32),
                        pltpu.SemaphoreType.DMA((2,)), pltpu.SemaphoreType.DMA((2,))],
    )
    return pl.pallas_call(
        _experts_kernel,
        grid_spec=grid_spec,
        out_shape=jax.ShapeDtypeStruct((n_out * ROW_TILE, LANES), F32),
        compiler_params=_cp("arbitrary"),
        name="experts",
    )(tile_expert, tile_valid, src, src, dst, h2, wg, wu, wd)


def _combine_kernel(*refs):
    yg_refs = refs[:TOP_K]
    wt_ref, h_ref, x_ref, mod_ref, wsg_ref, wsu_ref, wsd_ref, ln_ref, o_ref = refs[TOP_K:]
    wt = wt_ref[...]
    n = x_ref.shape[0]
    acc = wt[:, 0:1] * _tiles_to_rows(yg_refs[0], (), n)
    for k in range(1, TOP_K):
        acc = acc + wt[:, k:k + 1] * _tiles_to_rows(yg_refs[k], (), n)
    h = _tiles_to_rows(h_ref, (), n).astype(BF16)
    a = jnp.dot(h, wsg_ref[...], preferred_element_type=F32)
    b = jnp.dot(h, wsu_ref[...], preferred_element_type=F32)
    acc = acc + jnp.dot((_silu(a) * b).astype(BF16), wsd_ref[...], preferred_element_type=F32)
    m = mod_ref[...]
    ln = ln_ref[...]
    o_ref[...] = _ln(ALPHA * x_ref[...] + m[5:6] * acc) * ln[0:1] + ln[1:2]


def _combine(yg, wt, h2, xs, mod, wsg, wsu, wsd, ln, t_ctx, seq):
    t, d = xs.shape
    full = lambda a: pl.BlockSpec(a.shape, lambda i: (0, 0))
    n_blocks = t // TM
    slot_specs = [pl.BlockSpec((TM * ROW_TILE, LANES), functools.partial(lambda i, k: (k * n_blocks + i, 0), k=k))
                  for k in range(TOP_K)]
    return pl.pallas_call(
        _combine_kernel,
        grid=(n_blocks,),
        in_specs=slot_specs + [
                  pl.BlockSpec((TM, TOP_K), lambda i: (i, 0)),
                  pl.BlockSpec((TM * ROW_TILE, LANES), lambda i: (i, 0)),
                  pl.BlockSpec((TM, d), lambda i: (i, 0)),
                  pl.BlockSpec((None, 6, d), lambda i: (_mod_row(i, t_ctx, seq, TM), 0, 0)),
                  full(wsg), full(wsu), full(wsd), full(ln)],
        out_specs=pl.BlockSpec((TM, d), lambda i: (i, 0)),
        out_shape=jax.ShapeDtypeStruct((t, d), F32),
        compiler_params=_cp("arbitrary"),
        name="combine",
    )(*([yg] * TOP_K), wt, h2, xs, mod, wsg, wsu, wsd, ln)


def _moe(xs, mod, wr_t, rb, wg, wu, wd, layer, wsg, wsu, wsd, ln, t_ctx, seq):
    t, d = xs.shape
    h2, idx, wt, cnt = _router(xs, mod, wr_t, rb, t_ctx, seq)
    n_pairs = t * TOP_K
    n_rows = n_pairs + N_EXPERTS * TME
    n_tiles = n_rows // TME
    assert n_rows < (1 << KEY_SHIFT) and n_pairs % TME == 0
    counts = cnt[:, 0].astype(jnp.int32)
    n_padding = (-counts) % TME
    slot = jnp.arange(TME, dtype=jnp.int32)[None, :]
    pad_key = jnp.where(slot < n_padding[:, None], jnp.arange(N_EXPERTS, dtype=jnp.int32)[:, None],
                        N_EXPERTS).reshape(-1)
    keys = jnp.concatenate([idx.reshape(-1), pad_key]) * (1 << KEY_SHIFT) + jnp.arange(n_rows, dtype=jnp.int32)
    order = lax.sort(keys)
    pair = order & ((1 << KEY_SHIFT) - 1)
    src = ((pair % t) * ROW_TILE).reshape(n_tiles, 1, TME)
    dst = (jnp.where(pair < n_pairs, pair, n_pairs + pair % TME) * ROW_TILE).reshape(n_tiles, 1, TME)
    first = order[::TME] >> KEY_SHIFT
    tile_expert = jnp.minimum(first, N_EXPERTS - 1)
    tile_valid = (first < N_EXPERTS).astype(jnp.int32)
    ye = _experts(tile_expert, tile_valid, src, dst, h2, wg, wu, wd, layer, n_pairs + TME)
    return _combine(ye, wt.T, h2, xs, mod, wsg, wsu, wsd, ln, t_ctx, seq)


def kernel(x, c, ctx, c_ctx, w_ada, b_ada, w_in, ssm_a_re, ssm_a_im, ssm_log_dt, ssm_b_re, ssm_b_im,
           ssm_c_re, ssm_c_im, ssm_d, w_glu, b_glu, conv_w, conv_b, conv_ln_g, conv_ln_b, mix_norm_g,
           w_out, ln1_g, ln1_b, w_router, router_bias, we_gate, we_up, we_down, ws_gate, ws_up,
           ws_down, ln2_g, ln2_b):
    bsz, seq, d = x.shape
    clen = ctx.shape[1]
    depth = w_in.shape[0]
    t_ctx, t_lat = bsz * clen, bsz * seq
    assert d == D_MODEL and bsz + 1 <= MOD_ROWS
    assert seq % GRID_W == 0 and clen % GRID_W == 0 and t_ctx % seq == 0
    assert seq % TMB == 0 and t_ctx % TMB == 0 and seq % TM == 0 and t_ctx % TM == 0

    xs = jnp.concatenate([ctx.reshape(t_ctx, d), x.reshape(t_lat, d)], axis=0)
    cc = jnp.zeros((MOD_ROWS, d), F32).at[0].set(c_ctx).at[1:bsz + 1].set(c)
    mod_all = _ada(cc, w_ada, b_ada).reshape(depth, MOD_ROWS, 6, d)
    perm = _chunk_perm()
    s5 = _s5_mats(ssm_a_re, ssm_a_im, ssm_log_dt, ssm_b_re, ssm_b_im, ssm_c_re, ssm_c_im, ssm_d)

    for l in range(depth):
        mod = mod_all[l]
        u, z = _inproj(xs, mod, w_in[l].astype(BF16), t_ctx, seq)

        tm, rm, om, lq, dvec = (a[l] for a in s5)
        h0 = jnp.zeros((SSM_G, bsz, 4 * SSM_P), F32)
        yc, fin = _scan(_to_chunks(u[:t_ctx], bsz, clen, perm), tm, rm, om, lq, dvec, h0, bsz)
        yl, _ = _scan(_to_chunks(u[t_ctx:], bsz, seq, perm), tm, rm, om, lq, dvec, fin, bsz)
        cp = jnp.stack([conv_b[l], conv_ln_g[l], conv_ln_b[l], mix_norm_g[l, D_SSM:]])
        pv = jnp.stack([b_glu[l], mix_norm_g[l, :D_SSM]])
        xs = _outproj(_from_chunks(yc, bsz, clen, perm), _from_chunks(yl, bsz, seq, perm),
                      _conv_ctx(z, conv_w[l], cp, bsz, clen),
                      _conv_lat(z, conv_w[l], cp, bsz, seq, t_ctx // seq),
                      xs, mod, w_glu[l].astype(BF16), pv, w_out[l].astype(BF16),
                      jnp.stack([ln1_g[l], ln1_b[l]]), t_ctx, seq)

        xs = _moe(xs, mod, w_router[l].T, router_bias[l][:, None], we_gate, we_up, we_down, l,
                  ws_gate[l].astype(BF16), ws_up[l].astype(BF16), ws_down[l].astype(BF16),
                  jnp.stack([ln2_g[l], ln2_b[l]]), t_ctx, seq)

    return xs[t_ctx:].reshape(bsz, seq, d)
```

```python
import functools
import math

import jax
import jax.numpy as jnp
from jax import lax
from jax.experimental import pallas as pl
from jax.experimental.pallas import tpu as pltpu

F32 = jnp.float32
BF16 = jnp.bfloat16

D_MODEL = 1024
DEPTH = 4
GRID_W = 64
D_SSM = 512
D_CONV = 512
SSM_H = 16
SSM_G = 32
SSM_P = 64
CONV_K = 31
CONV_PAD = 15
CONV_WIN = GRID_W + 32
N_EXPERTS = 64
TOP_K = 8
N_GROUPS = 8
TOP_K_GROUPS = 4
GROUP_SIZE = N_EXPERTS // N_GROUPS
D_EXPERT = 256
ROUTED_SCALE = 2.5
ALPHA = (2 * DEPTH) ** 0.25
LN_EPS = 1e-5

LANES = 128
CHUNK = 16
CW = CHUNK * SSM_H
LANE_TILES = D_SSM // LANES
GROUPS_PER_TILE = LANES // SSM_H
MOD_ROWS = 32
TM = 256
TMB = 512
TME = 512
KEY_SHIFT = 19
VMEM_LIMIT = 48 * 1024 * 1024


def _cp(*sem):
    return pltpu.CompilerParams(dimension_semantics=sem, vmem_limit_bytes=VMEM_LIMIT)


def _ln(x):
    mu = jnp.mean(x, axis=-1, keepdims=True)
    xc = x - mu
    var = jnp.mean(xc * xc, axis=-1, keepdims=True)
    return xc * lax.rsqrt(var + LN_EPS)


def _silu(x):
    return x * jax.nn.sigmoid(x)


def _ada_kernel(c_ref, w_ref, b_ref, o_ref):
    a = _silu(c_ref[...]).astype(BF16)
    o_ref[...] = jnp.dot(a, w_ref[...].astype(BF16), preferred_element_type=F32) + b_ref[...]


def _ada(cc, w_ada, b_ada):
    depth, d, n = w_ada.shape
    tn = 1536
    return pl.pallas_call(
        _ada_kernel,
        grid=(depth, n // tn),
        in_specs=[pl.BlockSpec((MOD_ROWS, d), lambda l, j: (0, 0)),
                  pl.BlockSpec((None, d, tn), lambda l, j: (l, 0, j)),
                  pl.BlockSpec((None, 1, tn), lambda l, j: (l, 0, j))],
        out_specs=pl.BlockSpec((None, MOD_ROWS, tn), lambda l, j: (l, 0, j)),
        out_shape=jax.ShapeDtypeStruct((depth, MOD_ROWS, n), F32),
        compiler_params=_cp("arbitrary", "arbitrary"),
        name="ada",
    )(cc, w_ada, b_ada.reshape(depth, 1, n))


def _inproj_kernel(x_ref, mod_ref, w_ref, u_ref, z_ref):
    m = mod_ref[...]
    h = _ln(x_ref[...]) * (1.0 + m[1:2]) + m[0:1]
    p = jnp.dot(h.astype(BF16), w_ref[...], preferred_element_type=F32)
    u_ref[...] = p[:, :D_SSM].astype(BF16)
    v = p[:, D_SSM:D_SSM + D_CONV]
    g = p[:, D_SSM + D_CONV:]
    z_ref[...] = (v * jax.nn.sigmoid(g)).astype(BF16)


def _mod_row(i, t_ctx, seq, tile):
    start = i * tile
    return jnp.where(start < t_ctx, 0, 1 + (start - t_ctx) // seq)


def _inproj(xs, mod, w, t_ctx, seq):
    t, d = xs.shape
    return pl.pallas_call(
        _inproj_kernel,
        grid=(t // TMB,),
        in_specs=[pl.BlockSpec((TMB, d), lambda i: (i, 0)),
                  pl.BlockSpec((None, 6, d), lambda i: (_mod_row(i, t_ctx, seq, TMB), 0, 0)),
                  pl.BlockSpec(w.shape, lambda i: (0, 0))],
        out_specs=[pl.BlockSpec((TMB, D_SSM), lambda i: (i, 0)),
                   pl.BlockSpec((TMB, D_CONV), lambda i: (i, 0))],
        out_shape=[jax.ShapeDtypeStruct((t, D_SSM), BF16),
                   jax.ShapeDtypeStruct((t, D_CONV), BF16)],
        compiler_params=_cp("arbitrary"),
        name="inproj",
    )(xs, mod, w)


def _s5_mats(a_re, a_im, log_dt, b_re, b_im, c_re, c_im, d_skip):
    hp = lax.Precision.HIGHEST
    q = CHUNK
    nl = a_re.shape[0]
    dt = jnp.exp(log_dt)[..., None]
    ldr, ldi = a_re * dt, a_im * dt
    ks = jnp.arange(-(q - 1), q + 1, dtype=F32)[:, None]
    mag = jnp.exp(ks * ldr[..., None, :])
    ang = ks * ldi[..., None, :]
    pr, pi = mag * jnp.cos(ang), mag * jnp.sin(ang)

    def powers(d, lo, hi, rev=False):
        r, i = pr[:, d, :, lo + q - 1:hi + q - 1], pi[:, d, :, lo + q - 1:hi + q - 1]
        return (r[:, :, ::-1], i[:, :, ::-1]) if rev else (r, i)

    nr, ni = pr[..., q, :] - 1.0, pi[..., q, :]
    den = a_re * a_re + a_im * a_im
    qr, qi = (nr * a_re + ni * a_im) / den, (ni * a_re - nr * a_im) / den
    bbr = qr[..., None] * b_re - qi[..., None] * b_im
    bbi = qr[..., None] * b_im + qi[..., None] * b_re

    def c_times(d, pw):
        cr, ci = c_re[:, d][:, :, None], c_im[:, d][:, :, None]
        r, i = pw[0][:, :, :, None, :], pw[1][:, :, :, None, :]
        return cr * r - ci * i, cr * i + ci * r

    def b_times(d, pw):
        br, bi = bbr[:, d][:, :, :, None, :], bbi[:, d][:, :, :, None, :]
        r, i = pw[0].swapaxes(2, 3)[..., None], pw[1].swapaxes(2, 3)[..., None]
        return r * br - i * bi, r * bi + i * br

    def b_times_t(d, pw):
        br, bi = bbr[:, d].swapaxes(-1, -2)[:, :, None], bbi[:, d].swapaxes(-1, -2)[:, :, None]
        r, i = pw[0][:, :, :, None, :], pw[1][:, :, :, None, :]
        return r * br - i * bi, r * bi + i * br

    def lag_kernel(d, pw_t, pw_s):
        ur, ui = c_times(d, pw_t)
        vr, vi = b_times(d, pw_s)
        ur, ui = ur.reshape(nl, SSM_G, CW, SSM_P), ui.reshape(nl, SSM_G, CW, SSM_P)
        vr, vi = vr.reshape(nl, SSM_G, SSM_P, CW), vi.reshape(nl, SSM_G, SSM_P, CW)
        return (jnp.einsum('lgpa,lgbp->lgab', vr, ur, precision=hp)
                - jnp.einsum('lgpa,lgbp->lgab', vi, ui, precision=hp))

    kf = lag_kernel(0, powers(0, 0, q), powers(0, -(q - 1), 1, rev=True))
    kb = lag_kernel(1, powers(1, -(q - 1), 1, rev=True), powers(1, 0, q))
    s_i = (jnp.arange(CW) // SSM_H)[:, None]
    t_i = (jnp.arange(CW) // SSM_H)[None, :]
    tm = jnp.where(t_i >= s_i, kf, 0.0) + jnp.where(s_i >= t_i, kb, 0.0)

    f_re, f_im = b_times_t(0, powers(0, 0, q, rev=True))
    g_re, g_im = b_times_t(1, powers(1, 0, q))
    rm = jnp.concatenate([f_re, g_re, f_im, g_im], axis=-1).reshape(nl, SSM_G, CW, 4 * SSM_P)

    of_r, of_i = c_times(0, powers(0, 1, q + 1))
    ob_r, ob_i = c_times(1, powers(1, 1, q + 1, rev=True))
    fix_o = lambda a: a.transpose(0, 1, 4, 2, 3).reshape(nl, SSM_G, SSM_P, CW)
    om = jnp.concatenate([fix_o(of_r), fix_o(ob_r), fix_o(-of_i), fix_o(-ob_i)], axis=2)

    last = 2 * q - 1
    lq = jnp.stack([jnp.concatenate([pr[:, 0, :, last], pr[:, 1, :, last]], -1),
                    jnp.concatenate([pi[:, 0, :, last], pi[:, 1, :, last]], -1)], axis=2)
    dvec = jnp.tile(d_skip[:, :, None, :], (1, 1, q, 1)).reshape(nl, SSM_G, 1, CW)
    return tm.astype(BF16), rm.astype(BF16), om.astype(BF16), lq, dvec


def _scan_kernel(x_ref, t_ref, r_ref, o_ref, lq_ref, d_ref, h0_ref, y_ref, fin_ref,
                 rr_ref, st_ref, *, n_chunks, bsz):
    p2 = 2 * SSM_P
    x = x_ref[...]
    rr_ref[...] = jnp.dot(x, r_ref[...], preferred_element_type=F32)
    lq = lq_ref[...]
    lre, lim = lq[0:1], lq[1:2]
    is_f = lax.broadcasted_iota(jnp.int32, (bsz, p2), 1) < SSM_P
    h0 = h0_ref[...]

    def step(j, carry):
        sre, sim = carry
        rf = pl.multiple_of(j * bsz, bsz)
        rb = pl.multiple_of((n_chunks - 1 - j) * bsz, bsz)
        st_ref[pl.ds(rf, bsz), 0:SSM_P] = sre[:, 0:SSM_P]
        st_ref[pl.ds(rb, bsz), SSM_P:p2] = sre[:, SSM_P:p2]
        st_ref[pl.ds(rf, bsz), p2:p2 + SSM_P] = sim[:, 0:SSM_P]
        st_ref[pl.ds(rb, bsz), p2 + SSM_P:2 * p2] = sim[:, SSM_P:p2]
        r_re = jnp.where(is_f, rr_ref[pl.ds(rf, bsz), 0:p2], rr_ref[pl.ds(rb, bsz), 0:p2])
        r_im = jnp.where(is_f, rr_ref[pl.ds(rf, bsz), p2:2 * p2], rr_ref[pl.ds(rb, bsz), p2:2 * p2])
        return lre * sre - lim * sim + r_re, lre * sim + lim * sre + r_im

    sre, sim = lax.fori_loop(0, n_chunks, step, (h0[:, 0:p2], h0[:, p2:2 * p2]))
    fin_ref[...] = jnp.concatenate([sre, sim], axis=1)
    y = jnp.dot(x, t_ref[...], preferred_element_type=F32)
    y = y + jnp.dot(st_ref[...].astype(BF16), o_ref[...], preferred_element_type=F32)
    y = y + d_ref[...] * x.astype(F32)
    y_ref[...] = jax.nn.gelu(y, approximate=True).astype(BF16)


def _scan(xg, tm, rm, om, lq, dvec, h0, bsz):
    g, n, _ = xg.shape
    n_chunks = n // bsz
    mat = pl.BlockSpec((None, CW, CW), lambda i: (i, 0, 0))
    return pl.pallas_call(
        functools.partial(_scan_kernel, n_chunks=n_chunks, bsz=bsz),
        grid=(g,),
        in_specs=[pl.BlockSpec((None, n, CW), lambda i: (i, 0, 0)), mat, mat, mat,
                  pl.BlockSpec((None, 2, 2 * SSM_P), lambda i: (i, 0, 0)),
                  pl.BlockSpec((None, 1, CW), lambda i: (i, 0, 0)),
                  pl.BlockSpec((None, bsz, 4 * SSM_P), lambda i: (i, 0, 0))],
        out_specs=[pl.BlockSpec((None, n, CW), lambda i: (i, 0, 0)),
                   pl.BlockSpec((None, bsz, 4 * SSM_P), lambda i: (i, 0, 0))],
        out_shape=[jax.ShapeDtypeStruct((g, n, CW), BF16),
                   jax.ShapeDtypeStruct((g, bsz, 4 * SSM_P), F32)],
        scratch_shapes=[pltpu.VMEM((n, 4 * SSM_P), F32), pltpu.VMEM((n, 4 * SSM_P), F32)],
        compiler_params=_cp("arbitrary"),
        name="s5_scan",
    )(xg, tm, rm, om, lq, dvec, h0)


def _chunk_perm():
    n = CHUNK * LANES
    src = (jnp.arange(n, dtype=jnp.int32).reshape(CHUNK, GROUPS_PER_TILE, SSM_H)
           .transpose(1, 0, 2).reshape(-1))
    return (jnp.arange(n, dtype=jnp.int32)[:, None] == src[None, :]).astype(BF16)


def _to_chunks(u, bsz, length, perm):
    nc = length // CHUNK
    z = (u.reshape(bsz, nc, CHUNK, LANE_TILES, LANES).transpose(3, 1, 0, 2, 4)
         .reshape(LANE_TILES, nc * bsz, CHUNK * LANES))
    w = jnp.einsum('jnk,kc->jnc', z, perm, preferred_element_type=BF16)
    return (w.reshape(LANE_TILES, nc * bsz, GROUPS_PER_TILE, CW).transpose(0, 2, 1, 3)
            .reshape(SSM_G, nc * bsz, CW))


def _from_chunks(y, bsz, length, perm):
    nc = length // CHUNK
    w = (y.reshape(LANE_TILES, GROUPS_PER_TILE, nc * bsz, CW).transpose(0, 2, 1, 3)
         .reshape(LANE_TILES, nc * bsz, CHUNK * LANES))
    z = jnp.einsum('jnc,kc->jnk', w, perm, preferred_element_type=BF16)
    return (z.reshape(LANE_TILES, nc, bsz, CHUNK, LANES).transpose(2, 1, 3, 0, 4)
            .reshape(bsz * length, D_SSM))


def _conv_post(cv, p_ref):
    p = p_ref[...]
    cv = cv + p[0:1]
    y = _silu(_ln(cv) * p[1:2] + p[2:3])
    return y * lax.rsqrt(jnp.mean(y * y, axis=-1, keepdims=True) + LN_EPS) * p[3:4]


def _conv_seq_tile(buf_ref, base, w_ref, b0, c0):
    win = buf_ref[pl.ds(base, CONV_WIN), b0:b0 + LANES]
    acc = jnp.zeros((GRID_W, LANES), F32)
    for ph in range(8):
        wb = win if ph == 0 else pltpu.roll(win, CONV_WIN - ph, axis=0)
        for a in range(4):
            j = 8 * a + ph - 1
            if 0 <= j < CONV_K:
                acc = acc + w_ref[j:j + 1, c0:c0 + LANES] * wb[8 * a:8 * a + GRID_W]
    return acc


def _conv_seq(buf_ref, base, w_ref, c0, width):
    return jnp.concatenate([_conv_seq_tile(buf_ref, base, w_ref, b, c0 + b)
                            for b in range(0, width, LANES)], axis=1)


def _conv_lat_kernel(z_ref, w_ref, p_ref, o_ref, hb_ref, vb_ref, cv_ref, *, rows):
    half = D_CONV // 2
    hstride = GRID_W + 16
    hb_ref[...] = jnp.zeros(hb_ref.shape, F32)
    vb_ref[pl.ds(0, CONV_PAD * GRID_W), :] = jnp.zeros((CONV_PAD * GRID_W, half), F32)
    vb_ref[pl.ds((CONV_PAD + rows) * GRID_W, CONV_PAD * GRID_W), :] = jnp.zeros((CONV_PAD * GRID_W, half), F32)
    for r in range(rows):
        hb_ref[pl.ds(16 + r * hstride, GRID_W), :] = z_ref[pl.ds(r * GRID_W, GRID_W), 0:half].astype(F32)
    vb_ref[pl.ds(CONV_PAD * GRID_W, rows * GRID_W), :] = z_ref[:, half:D_CONV].astype(F32)

    def along_row(b):
        def body(r, carry):
            hbase = pl.multiple_of(r * hstride, 16)
            vbase = pl.multiple_of(r * GRID_W, GRID_W)
            cv_ref[pl.ds(vbase, GRID_W), b:b + LANES] = _conv_seq_tile(hb_ref, hbase, w_ref, b, b)
            return carry
        lax.fori_loop(0, rows, body, 0)

    def along_col(b):
        def body(r, carry):
            vbase = pl.multiple_of(r * GRID_W, GRID_W)
            acc = jnp.zeros((GRID_W, LANES), F32)
            for j in range(CONV_K):
                acc = acc + (w_ref[j:j + 1, half + b:half + b + LANES]
                             * vb_ref[pl.ds(vbase + j * GRID_W, GRID_W), b:b + LANES])
            cv_ref[pl.ds(vbase, GRID_W), half + b:half + b + LANES] = acc
            return carry
        lax.fori_loop(0, rows, body, 0)

    for b in range(0, half, LANES):
        along_row(b)
        along_col(b)

    def post(r, carry):
        vbase = pl.multiple_of(r * GRID_W, GRID_W)
        o_ref[pl.ds(vbase, GRID_W), :] = _conv_post(cv_ref[pl.ds(vbase, GRID_W), :], p_ref).astype(BF16)
        return carry

    lax.fori_loop(0, rows, post, 0, unroll=4)


def _conv_lat(z, w, p, bsz, seq, blk0):
    rows = seq // GRID_W
    half = D_CONV // 2
    return pl.pallas_call(
        functools.partial(_conv_lat_kernel, rows=rows),
        grid=(bsz,),
        in_specs=[pl.BlockSpec((seq, D_CONV), lambda b: (blk0 + b, 0)),
                  pl.BlockSpec(w.shape, lambda b: (0, 0)),
                  pl.BlockSpec(p.shape, lambda b: (0, 0))],
        out_specs=pl.BlockSpec((seq, D_CONV), lambda b: (b, 0)),
        out_shape=jax.ShapeDtypeStruct((bsz * seq, D_CONV), BF16),
        scratch_shapes=[pltpu.VMEM((rows * (GRID_W + 16) + 16, half), F32),
                        pltpu.VMEM(((rows + 2 * CONV_PAD) * GRID_W, half), F32),
                        pltpu.VMEM((seq, D_CONV), F32)],
        compiler_params=_cp("arbitrary"),
        name="conv_latent",
    )(z, w, p)


def _conv_ctx_kernel(z_ref, w_ref, p_ref, o_ref, cb_ref, *, clen):
    cb_ref[pl.ds(0, 16), :] = jnp.zeros((16, D_CONV), F32)
    cb_ref[pl.ds(16 + clen, 16), :] = jnp.zeros((16, D_CONV), F32)
    cb_ref[pl.ds(16, clen), :] = z_ref[...].astype(F32)
    for blk in range(clen // GRID_W):
        base = blk * GRID_W
        acc = _conv_seq(cb_ref, base, w_ref, 0, D_CONV)
        o_ref[pl.ds(base, GRID_W), :] = _conv_post(acc, p_ref).astype(BF16)


def _conv_ctx(z, w, p, bsz, clen):
    return pl.pallas_call(
        functools.partial(_conv_ctx_kernel, clen=clen),
        grid=(bsz,),
        in_specs=[pl.BlockSpec((clen, D_CONV), lambda b: (b, 0)),
                  pl.BlockSpec(w.shape, lambda b: (0, 0)),
                  pl.BlockSpec(p.shape, lambda b: (0, 0))],
        out_specs=pl.BlockSpec((clen, D_CONV), lambda b: (b, 0)),
        out_shape=jax.ShapeDtypeStruct((bsz * clen, D_CONV), BF16),
        scratch_shapes=[pltpu.VMEM((clen + 32, D_CONV), F32)],
        compiler_params=_cp("arbitrary"),
        name="conv_context",
    )(z, w, p)


def _outproj_kernel(ysc_ref, ysl_ref, ycc_ref, ycl_ref, x_ref, mod_ref, wglu_ref, pv_ref, wo_ref, ln_ref,
                    o_ref, *, n_ctx_tiles):
    is_ctx = pl.program_id(0) < n_ctx_tiles
    ys = jnp.where(is_ctx, ysc_ref[...], ysl_ref[...])
    yc = jnp.where(is_ctx, ycc_ref[...], ycl_ref[...])
    pv = pv_ref[...]
    gl = jnp.dot(ys, wglu_ref[...], preferred_element_type=F32) + pv[0:1]
    yg = ys.astype(F32) * jax.nn.sigmoid(gl)
    yn = yg * lax.rsqrt(jnp.mean(yg * yg, axis=-1, keepdims=True) + LN_EPS) * pv[1:2]
    y = jnp.dot(yn.astype(BF16), wo_ref[0:D_SSM, :], preferred_element_type=F32)
    y = y + jnp.dot(yc, wo_ref[D_SSM:, :], preferred_element_type=F32)
    m = mod_ref[...]
    ln = ln_ref[...]
    o_ref[...] = _ln(ALPHA * x_ref[...] + m[2:3] * y) * ln[0:1] + ln[1:2]


def _outproj(ys_ctx, ys_lat, yc_ctx, yc_lat, xs, mod, wglu, pv, wo, ln, t_ctx, seq):
    t, d = xs.shape
    full = lambda a: pl.BlockSpec(a.shape, lambda i: (0, 0))
    nct = t_ctx // TMB
    ctx_tile = lambda i: (jnp.clip(i, 0, max(nct - 1, 0)), 0)
    lat_tile = lambda i: (jnp.maximum(i - nct, 0), 0)
    return pl.pallas_call(
        functools.partial(_outproj_kernel, n_ctx_tiles=nct),
        grid=(t // TMB,),
        in_specs=[pl.BlockSpec((TMB, D_SSM), ctx_tile),
                  pl.BlockSpec((TMB, D_SSM), lat_tile),
                  pl.BlockSpec((TMB, D_CONV), ctx_tile),
                  pl.BlockSpec((TMB, D_CONV), lat_tile),
                  pl.BlockSpec((TMB, d), lambda i: (i, 0)),
                  pl.BlockSpec((None, 6, d), lambda i: (_mod_row(i, t_ctx, seq, TMB), 0, 0)),
                  full(wglu), full(pv), full(wo), full(ln)],
        out_specs=pl.BlockSpec((TMB, d), lambda i: (i, 0)),
        out_shape=jax.ShapeDtypeStruct((t, d), F32),
        compiler_params=_cp("arbitrary"),
        name="outproj",
    )(ys_ctx, ys_lat, yc_ctx, yc_lat, xs, mod, wglu, pv, wo, ln)


ROW_TILE = 8


def _rows_to_tiles(ref, lead, val):
    n = val.shape[0]
    for c in range(ROW_TILE):
        ref[lead + (pl.ds(c, n, stride=ROW_TILE), slice(None))] = val[:, c * LANES:(c + 1) * LANES]


def _tiles_to_rows(ref, lead, n):
    return jnp.concatenate([ref[lead + (pl.ds(c, n, stride=ROW_TILE), slice(None))]
                            for c in range(ROW_TILE)], axis=1)


def _router_kernel(x_ref, mod_ref, wr_ref, rb_ref, h_ref, idx_ref, wt_ref, cnt_ref, carry_ref):
    @pl.when(pl.program_id(0) == 0)
    def _():
        carry_ref[...] = jnp.zeros(carry_ref.shape, F32)

    m = mod_ref[...]
    h = _ln(x_ref[...]) * (1.0 + m[4:5]) + m[3:4]
    _rows_to_tiles(h_ref, (), h)
    logits = lax.dot_general(wr_ref[...], h, (((1,), (1,)), ((), ())),
                             precision=lax.Precision.HIGHEST, preferred_element_type=F32)
    carry = carry_ref[...]
    for c0 in range(0, logits.shape[1], LANES):
        idx, wts, selm = _select_experts(logits[:, c0:c0 + LANES], rb_ref[...])
        idx_ref[:, c0:c0 + LANES] = idx
        wt_ref[:, c0:c0 + LANES] = wts
        carry = carry + jnp.sum(selm, axis=1, keepdims=True)
    carry_ref[...] = carry
    cnt_ref[...] = carry


def _select_experts(logits, bias):
    s = jax.nn.sigmoid(logits)
    biased = s + bias
    ninf = -jnp.inf
    tm = s.shape[1]

    row8 = lax.broadcasted_iota(jnp.int32, (GROUP_SIZE, tm), 0).astype(F32)
    gs = []
    for q in range(N_GROUPS):
        v = biased[q * GROUP_SIZE:(q + 1) * GROUP_SIZE]
        m1 = jnp.max(v, axis=0, keepdims=True)
        i1 = jnp.min(jnp.where(v == m1, row8, float(GROUP_SIZE)), axis=0, keepdims=True)
        m2 = jnp.max(jnp.where(row8 == i1, ninf, v), axis=0, keepdims=True)
        gs.append(m1 + m2)
    gsc = jnp.concatenate(gs, axis=0)
    rowg = lax.broadcasted_iota(jnp.int32, (N_GROUPS, tm), 0).astype(F32)
    gsel = jnp.zeros((N_GROUPS, tm), F32)
    for _ in range(TOP_K_GROUPS):
        mx = jnp.max(gsc, axis=0, keepdims=True)
        ii = jnp.min(jnp.where(gsc == mx, rowg, float(N_GROUPS)), axis=0, keepdims=True)
        hit = rowg == ii
        gsel = jnp.where(hit, 1.0, gsel)
        gsc = jnp.where(hit, ninf, gsc)
    emask = jnp.concatenate([jnp.broadcast_to(gsel[q:q + 1], (GROUP_SIZE, tm))
                             for q in range(N_GROUPS)], axis=0)
    masked = jnp.where(emask > 0.0, biased, ninf)

    rowe = lax.broadcasted_iota(jnp.int32, (N_EXPERTS, tm), 0).astype(F32)
    selm = jnp.zeros((N_EXPERTS, tm), F32)
    idxs, ws = [], []
    for _ in range(TOP_K):
        mx = jnp.max(masked, axis=0, keepdims=True)
        ii = jnp.min(jnp.where(masked == mx, rowe, float(N_EXPERTS)), axis=0, keepdims=True)
        hit = rowe == ii
        idxs.append(ii)
        ws.append(jnp.sum(jnp.where(hit, s, 0.0), axis=0, keepdims=True))
        selm = jnp.where(hit, 1.0, selm)
        masked = jnp.where(hit, ninf, masked)
    wsum = ws[0]
    for k in range(1, TOP_K):
        wsum = wsum + ws[k]
    return (jnp.concatenate(idxs, axis=0).astype(jnp.int32),
            jnp.concatenate([w / wsum * ROUTED_SCALE for w in ws], axis=0), selm)


def _router(xs, mod, wr_t, rb, t_ctx, seq):
    t, d = xs.shape
    full = lambda a: pl.BlockSpec(a.shape, lambda i: (0, 0))
    return pl.pallas_call(
        _router_kernel,
        grid=(t // TMB,),
        in_specs=[pl.BlockSpec((TMB, d), lambda i: (i, 0)),
                  pl.BlockSpec((None, 6, d), lambda i: (_mod_row(i, t_ctx, seq, TMB), 0, 0)),
                  full(wr_t), full(rb)],
        out_specs=[pl.BlockSpec((TMB * ROW_TILE, LANES), lambda i: (i, 0)),
                   pl.BlockSpec((TOP_K, TMB), lambda i: (0, i)),
                   pl.BlockSpec((TOP_K, TMB), lambda i: (0, i)),
                   pl.BlockSpec((N_EXPERTS, 1), lambda i: (0, 0))],
        out_shape=[jax.ShapeDtypeStruct((t * ROW_TILE, LANES), F32),
                   jax.ShapeDtypeStruct((TOP_K, t), jnp.int32),
                   jax.ShapeDtypeStruct((TOP_K, t), F32),
                   jax.ShapeDtypeStruct((N_EXPERTS, 1), F32)],
        scratch_shapes=[pltpu.VMEM((N_EXPERTS, 1), F32)],
        compiler_params=_cp("arbitrary"),
        name="router",
    )(xs, mod, wr_t, rb)


def _experts_kernel(te_ref, tv_ref, src0_ref, srcn_ref, dst_ref, h_ref, wg_ref, wu_ref, wd_ref,
                    y_ref, xbuf, ybuf, gsem, ssem):
    i = pl.program_id(0)
    valid = tv_ref[i] > 0
    prev_valid = tv_ref[jnp.maximum(i - 1, 0)] > 0
    rows = TME * ROW_TILE

    def gather_start(idx_ref, s):
        for r in range(TME):
            pltpu.make_async_copy(
                h_ref.at[pl.ds(pl.multiple_of(idx_ref[0, r], ROW_TILE), ROW_TILE), :],
                xbuf.at[s, pl.ds(r * ROW_TILE, ROW_TILE), :], gsem.at[s]).start()

    def scatter_start(s):
        for r in range(TME):
            pltpu.make_async_copy(
                ybuf.at[s, pl.ds(r * ROW_TILE, ROW_TILE), :],
                y_ref.at[pl.ds(pl.multiple_of(dst_ref[0, r], ROW_TILE), ROW_TILE), :],
                ssem.at[s]).start()

    def gather_wait(s):
        pltpu.make_async_copy(h_ref.at[pl.ds(0, rows), :], xbuf.at[s], gsem.at[s]).wait()

    def scatter_wait(s):
        pltpu.make_async_copy(ybuf.at[s], y_ref.at[pl.ds(0, rows), :], ssem.at[s]).wait()

    @pl.when(i == 0)
    def _():
        gather_start(src0_ref, 0)

    def step(s):
        @pl.when(valid)
        def _():
            @pl.when(i >= 2)
            def _():
                scatter_wait(s)

            gather_wait(s)
            gather_start(srcn_ref, 1 - s)
            x = _tiles_to_rows(xbuf, (s,), TME).astype(BF16)
            a = jnp.dot(x, wg_ref[...].astype(BF16), preferred_element_type=F32)
            b = jnp.dot(x, wu_ref[...].astype(BF16), preferred_element_type=F32)
            hid = (_silu(a) * b).astype(BF16)
            _rows_to_tiles(ybuf, (s,), jnp.dot(hid, wd_ref[...].astype(BF16), preferred_element_type=F32))
            scatter_start(s)

        @pl.when(jnp.logical_not(valid) & prev_valid & (i >= 1))
        def _():
            gather_wait(s)
            scatter_wait(1 - s)

            @pl.when(i >= 2)
            def _():
                scatter_wait(s)

            dump = pltpu.make_async_copy(ybuf.at[s], y_ref.at[pl.ds(y_ref.shape[0] - rows, rows), :],
                                         ssem.at[s])
            dump.start()
            dump.wait()

    for s in range(2):
        pl.when(i % 2 == s)(functools.partial(step, s))


def _experts(tile_expert, tile_valid, src, dst, h2, wg, wu, wd, layer, n_out):
    n_tiles = src.shape[0]
    d, f = wg.shape[2:]
    smem = lambda fn: pl.BlockSpec((None, 1, TME), fn, memory_space=pltpu.SMEM)
    grid_spec = pltpu.PrefetchScalarGridSpec(
        num_scalar_prefetch=2,
        grid=(n_tiles,),
        in_specs=[smem(lambda i, te, tv: (0, 0, 0)),
                  smem(lambda i, te, tv: (jnp.minimum(i + 1, n_tiles - 1), 0, 0)),
                  smem(lambda i, te, tv: (i, 0, 0)),
                  pl.BlockSpec(memory_space=pl.ANY),
                  pl.BlockSpec((None, None, d, f), lambda i, te, tv: (layer, te[i], 0, 0)),
                  pl.BlockSpec((None, None, d, f), lambda i, te, tv: (layer, te[i], 0, 0)),
                  pl.BlockSpec((None, None, f, d), lambda i, te, tv: (layer, te[i], 0, 0))],
        out_specs=pl.BlockSpec(memory_space=pl.ANY),
        scratch_shapes=[pltpu.VMEM((2, TME * ROW_TILE, LANES), F32),
                        pltpu.VMEM((2, TME * ROW_TILE, LANES), F32),
                        pltpu.SemaphoreType.DMA((2,)), pltpu.SemaphoreType.DMA((2,))],
    )
    return pl.pallas_call(
        _experts_kernel,
        grid_spec=grid_spec,
        out_shape=jax.ShapeDtypeStruct((n_out * ROW_TILE, LANES), F32),
        compiler_params=_cp("arbitrary"),
        name="experts",
    )(tile_expert, tile_valid, src, src, dst, h2, wg, wu, wd)


def _combine_kernel(*refs):
    yg_refs = refs[:TOP_K]
    wt_ref, h_ref, x_ref, mod_ref, wsg_ref, wsu_ref, wsd_ref, ln_ref, o_ref = refs[TOP_K:]
    wt = wt_ref[...]
    n = x_ref.shape[0]
    acc = wt[:, 0:1] * _tiles_to_rows(yg_refs[0], (), n)
    for k in range(1, TOP_K):
        acc = acc + wt[:, k:k + 1] * _tiles_to_rows(yg_refs[k], (), n)
    h = _tiles_to_rows(h_ref, (), n).astype(BF16)
    a = jnp.dot(h, wsg_ref[...], preferred_element_type=F32)
    b = jnp.dot(h, wsu_ref[...], preferred_element_type=F32)
    acc = acc + jnp.dot((_silu(a) * b).astype(BF16), wsd_ref[...], preferred_element_type=F32)
    m = mod_ref[...]
    ln = ln_ref[...]
    o_ref[...] = _ln(ALPHA * x_ref[...] + m[5:6] * acc) * ln[0:1] + ln[1:2]


def _combine(yg, wt, h2, xs, mod, wsg, wsu, wsd, ln, t_ctx, seq):
    t, d = xs.shape
    full = lambda a: pl.BlockSpec(a.shape, lambda i: (0, 0))
    n_blocks = t // TM
    slot_specs = [pl.BlockSpec((TM * ROW_TILE, LANES), functools.partial(lambda i, k: (k * n_blocks + i, 0), k=k))
                  for k in range(TOP_K)]
    return pl.pallas_call(
        _combine_kernel,
        grid=(n_blocks,),
        in_specs=slot_specs + [
                  pl.BlockSpec((TM, TOP_K), lambda i: (i, 0)),
                  pl.BlockSpec((TM * ROW_TILE, LANES), lambda i: (i, 0)),
                  pl.BlockSpec((TM, d), lambda i: (i, 0)),
                  pl.BlockSpec((None, 6, d), lambda i: (_mod_row(i, t_ctx, seq, TM), 0, 0)),
                  full(wsg), full(wsu), full(wsd), full(ln)],
        out_specs=pl.BlockSpec((TM, d), lambda i: (i, 0)),
        out_shape=jax.ShapeDtypeStruct((t, d), F32),
        compiler_params=_cp("arbitrary"),
        name="combine",
    )(*([yg] * TOP_K), wt, h2, xs, mod, wsg, wsu, wsd, ln)


def _moe(xs, mod, wr_t, rb, wg, wu, wd, layer, wsg, wsu, wsd, ln, t_ctx, seq):
    t, d = xs.shape
    h2, idx, wt, cnt = _router(xs, mod, wr_t, rb, t_ctx, seq)
    n_pairs = t * TOP_K
    n_rows = n_pairs + N_EXPERTS * TME
    n_tiles = n_rows // TME
    assert n_rows < (1 << KEY_SHIFT) and n_pairs % TME == 0
    counts = cnt[:, 0].astype(jnp.int32)
    n_padding = (-counts) % TME
    slot = jnp.arange(TME, dtype=jnp.int32)[None, :]
    pad_key = jnp.where(slot < n_padding[:, None], jnp.arange(N_EXPERTS, dtype=jnp.int32)[:, None],
                        N_EXPERTS).reshape(-1)
    keys = jnp.concatenate([idx.reshape(-1), pad_key]) * (1 << KEY_SHIFT) + jnp.arange(n_rows, dtype=jnp.int32)
    order = lax.sort(keys)
    pair = order & ((1 << KEY_SHIFT) - 1)
    src = ((pair % t) * ROW_TILE).reshape(n_tiles, 1, TME)
    dst = (jnp.where(pair < n_pairs, pair, n_pairs + pair % TME) * ROW_TILE).reshape(n_tiles, 1, TME)
    first = order[::TME] >> KEY_SHIFT
    tile_expert = jnp.minimum(first, N_EXPERTS - 1)
    tile_valid = (first < N_EXPERTS).astype(jnp.int32)
    ye = _experts(tile_expert, tile_valid, src, dst, h2, wg, wu, wd, layer, n_pairs + TME)
    return _combine(ye, wt.T, h2, xs, mod, wsg, wsu, wsd, ln, t_ctx, seq)


def kernel(x, c, ctx, c_ctx, w_ada, b_ada, w_in, ssm_a_re, ssm_a_im, ssm_log_dt, ssm_b_re, ssm_b_im,
           ssm_c_re, ssm_c_im, ssm_d, w_glu, b_glu, conv_w, conv_b, conv_ln_g, conv_ln_b, mix_norm_g,
           w_out, ln1_g, ln1_b, w_router, router_bias, we_gate, we_up, we_down, ws_gate, ws_up,
           ws_down, ln2_g, ln2_b):
    bsz, seq, d = x.shape
    clen = ctx.shape[1]
    depth = w_in.shape[0]
    t_ctx, t_lat = bsz * clen, bsz * seq
    assert d == D_MODEL and bsz + 1 <= MOD_ROWS
    assert seq % GRID_W == 0 and clen % GRID_W == 0 and t_ctx % seq == 0
    assert seq % TMB == 0 and t_ctx % TMB == 0 and seq % TM == 0 and t_ctx % TM == 0

    xs = jnp.concatenate([ctx.reshape(t_ctx, d), x.reshape(t_lat, d)], axis=0)
    cc = jnp.zeros((MOD_ROWS, d), F32).at[0].set(c_ctx).at[1:bsz + 1].set(c)
    mod_all = _ada(cc, w_ada, b_ada).reshape(depth, MOD_ROWS, 6, d)
    perm = _chunk_perm()
    s5 = _s5_mats(ssm_a_re, ssm_a_im, ssm_log_dt, ssm_b_re, ssm_b_im, ssm_c_re, ssm_c_im, ssm_d)

    for l in range(depth):
        mod = mod_all[l]
        u, z = _inproj(xs, mod, w_in[l].astype(BF16), t_ctx, seq)

        tm, rm, om, lq, dvec = (a[l] for a in s5)
        h0 = jnp.zeros((SSM_G, bsz, 4 * SSM_P), F32)
        yc, fin = _scan(_to_chunks(u[:t_ctx], bsz, clen, perm), tm, rm, om, lq, dvec, h0, bsz)
        yl, _ = _scan(_to_chunks(u[t_ctx:], bsz, seq, perm), tm, rm, om, lq, dvec, fin, bsz)
        cp = jnp.stack([conv_b[l], conv_ln_g[l], conv_ln_b[l], mix_norm_g[l, D_SSM:]])
        pv = jnp.stack([b_glu[l], mix_norm_g[l, :D_SSM]])
        ys_lat = _from_chunks(yl, bsz, seq, perm)
        yc_lat = _conv_lat(z, conv_w[l], cp, bsz, seq, t_ctx // seq)
        if l == depth - 1:
            xs, ys_ctx, yc_ctx, n_ctx = xs[t_ctx:], ys_lat, yc_lat, 0
        else:
            ys_ctx, yc_ctx, n_ctx = _from_chunks(yc, bsz, clen, perm), _conv_ctx(z, conv_w[l], cp, bsz, clen), t_ctx
        xs = _outproj(ys_ctx, ys_lat, yc_ctx, yc_lat, xs, mod, w_glu[l].astype(BF16), pv,
                      w_out[l].astype(BF16), jnp.stack([ln1_g[l], ln1_b[l]]), n_ctx, seq)

        xs = _moe(xs, mod, w_router[l].T, router_bias[l][:, None], we_gate, we_up, we_down, l,
                  ws_gate[l].astype(BF16), ws_up[l].astype(BF16), ws_down[l].astype(BF16),
                  jnp.stack([ln2_g[l], ln2_b[l]]), n_ctx, seq)

    return xs.reshape(bsz, seq, d)
```

```python
import functools
import math

import jax
import jax.numpy as jnp
from jax import lax
from jax.experimental import pallas as pl
from jax.experimental.pallas import tpu as pltpu

F32 = jnp.float32
BF16 = jnp.bfloat16

D_MODEL = 1024
DEPTH = 4
GRID_W = 64
D_SSM = 512
D_CONV = 512
SSM_H = 16
SSM_G = 32
SSM_P = 64
CONV_K = 31
CONV_PAD = 15
CONV_WIN = GRID_W + 32
N_EXPERTS = 64
TOP_K = 8
N_GROUPS = 8
TOP_K_GROUPS = 4
GROUP_SIZE = N_EXPERTS // N_GROUPS
D_EXPERT = 256
ROUTED_SCALE = 2.5
ALPHA = (2 * DEPTH) ** 0.25
LN_EPS = 1e-5

LANES = 128
CHUNK = 16
CW = CHUNK * SSM_H
LANE_TILES = D_SSM // LANES
GROUPS_PER_TILE = LANES // SSM_H
MOD_ROWS = 32
TM = 256
TMB = 1024
TME = 512
KEY_SHIFT = 19
VMEM_LIMIT = 48 * 1024 * 1024


def _cp(*sem):
    return pltpu.CompilerParams(dimension_semantics=sem, vmem_limit_bytes=VMEM_LIMIT)


def _ln(x):
    mu = jnp.mean(x, axis=-1, keepdims=True)
    xc = x - mu
    var = jnp.mean(xc * xc, axis=-1, keepdims=True)
    return xc * lax.rsqrt(var + LN_EPS)


def _silu(x):
    return x * jax.nn.sigmoid(x)


def _ada_kernel(c_ref, w_ref, b_ref, o_ref):
    a = _silu(c_ref[...]).astype(BF16)
    o_ref[...] = jnp.dot(a, w_ref[...].astype(BF16), preferred_element_type=F32) + b_ref[...]


def _ada(cc, w_ada, b_ada):
    depth, d, n = w_ada.shape
    tn = 1536
    return pl.pallas_call(
        _ada_kernel,
        grid=(depth, n // tn),
        in_specs=[pl.BlockSpec((MOD_ROWS, d), lambda l, j: (0, 0)),
                  pl.BlockSpec((None, d, tn), lambda l, j: (l, 0, j)),
                  pl.BlockSpec((None, 1, tn), lambda l, j: (l, 0, j))],
        out_specs=pl.BlockSpec((None, MOD_ROWS, tn), lambda l, j: (l, 0, j)),
        out_shape=jax.ShapeDtypeStruct((depth, MOD_ROWS, n), F32),
        compiler_params=_cp("arbitrary", "arbitrary"),
        name="ada",
    )(cc, w_ada, b_ada.reshape(depth, 1, n))


def _inproj_kernel(x_ref, mod_ref, w_ref, u_ref, z_ref):
    m = mod_ref[...]
    h = _ln(x_ref[...]) * (1.0 + m[1:2]) + m[0:1]
    p = jnp.dot(h.astype(BF16), w_ref[...], preferred_element_type=F32)
    u_ref[...] = p[:, :D_SSM].astype(BF16)
    v = p[:, D_SSM:D_SSM + D_CONV]
    g = p[:, D_SSM + D_CONV:]
    z_ref[...] = (v * jax.nn.sigmoid(g)).astype(BF16)


def _mod_row(i, t_ctx, seq, tile):
    start = i * tile
    return jnp.where(start < t_ctx, 0, 1 + (start - t_ctx) // seq)


def _inproj(xs, mod, w, t_ctx, seq):
    t, d = xs.shape
    return pl.pallas_call(
        _inproj_kernel,
        grid=(t // TMB,),
        in_specs=[pl.BlockSpec((TMB, d), lambda i: (i, 0)),
                  pl.BlockSpec((None, 6, d), lambda i: (_mod_row(i, t_ctx, seq, TMB), 0, 0)),
                  pl.BlockSpec(w.shape, lambda i: (0, 0))],
        out_specs=[pl.BlockSpec((TMB, D_SSM), lambda i: (i, 0)),
                   pl.BlockSpec((TMB, D_CONV), lambda i: (i, 0))],
        out_shape=[jax.ShapeDtypeStruct((t, D_SSM), BF16),
                   jax.ShapeDtypeStruct((t, D_CONV), BF16)],
        compiler_params=_cp("arbitrary"),
        name="inproj",
    )(xs, mod, w)


def _s5_mats(a_re, a_im, log_dt, b_re, b_im, c_re, c_im, d_skip):
    hp = lax.Precision.HIGHEST
    q = CHUNK
    nl = a_re.shape[0]
    dt = jnp.exp(log_dt)[..., None]
    ldr, ldi = a_re * dt, a_im * dt
    ks = jnp.arange(-(q - 1), q + 1, dtype=F32)[:, None]
    mag = jnp.exp(ks * ldr[..., None, :])
    ang = ks * ldi[..., None, :]
    pr, pi = mag * jnp.cos(ang), mag * jnp.sin(ang)

    def powers(d, lo, hi, rev=False):
        r, i = pr[:, d, :, lo + q - 1:hi + q - 1], pi[:, d, :, lo + q - 1:hi + q - 1]
        return (r[:, :, ::-1], i[:, :, ::-1]) if rev else (r, i)

    nr, ni = pr[..., q, :] - 1.0, pi[..., q, :]
    den = a_re * a_re + a_im * a_im
    qr, qi = (nr * a_re + ni * a_im) / den, (ni * a_re - nr * a_im) / den
    bbr = qr[..., None] * b_re - qi[..., None] * b_im
    bbi = qr[..., None] * b_im + qi[..., None] * b_re

    def c_times(d, pw):
        cr, ci = c_re[:, d][:, :, None], c_im[:, d][:, :, None]
        r, i = pw[0][:, :, :, None, :], pw[1][:, :, :, None, :]
        return cr * r - ci * i, cr * i + ci * r

    def b_times(d, pw):
        br, bi = bbr[:, d][:, :, :, None, :], bbi[:, d][:, :, :, None, :]
        r, i = pw[0].swapaxes(2, 3)[..., None], pw[1].swapaxes(2, 3)[..., None]
        return r * br - i * bi, r * bi + i * br

    def b_times_t(d, pw):
        br, bi = bbr[:, d].swapaxes(-1, -2)[:, :, None], bbi[:, d].swapaxes(-1, -2)[:, :, None]
        r, i = pw[0][:, :, :, None, :], pw[1][:, :, :, None, :]
        return r * br - i * bi, r * bi + i * br

    def lag_kernel(d, pw_t, pw_s):
        ur, ui = c_times(d, pw_t)
        vr, vi = b_times(d, pw_s)
        ur, ui = ur.reshape(nl, SSM_G, CW, SSM_P), ui.reshape(nl, SSM_G, CW, SSM_P)
        vr, vi = vr.reshape(nl, SSM_G, SSM_P, CW), vi.reshape(nl, SSM_G, SSM_P, CW)
        return (jnp.einsum('lgpa,lgbp->lgab', vr, ur, precision=hp)
                - jnp.einsum('lgpa,lgbp->lgab', vi, ui, precision=hp))

    kf = lag_kernel(0, powers(0, 0, q), powers(0, -(q - 1), 1, rev=True))
    kb = lag_kernel(1, powers(1, -(q - 1), 1, rev=True), powers(1, 0, q))
    s_i = (jnp.arange(CW) // SSM_H)[:, None]
    t_i = (jnp.arange(CW) // SSM_H)[None, :]
    tm = jnp.where(t_i >= s_i, kf, 0.0) + jnp.where(s_i >= t_i, kb, 0.0)

    f_re, f_im = b_times_t(0, powers(0, 0, q, rev=True))
    g_re, g_im = b_times_t(1, powers(1, 0, q))
    rm = jnp.concatenate([f_re, g_re, f_im, g_im], axis=-1).reshape(nl, SSM_G, CW, 4 * SSM_P)

    of_r, of_i = c_times(0, powers(0, 1, q + 1))
    ob_r, ob_i = c_times(1, powers(1, 1, q + 1, rev=True))
    fix_o = lambda a: a.transpose(0, 1, 4, 2, 3).reshape(nl, SSM_G, SSM_P, CW)
    om = jnp.concatenate([fix_o(of_r), fix_o(ob_r), fix_o(-of_i), fix_o(-ob_i)], axis=2)

    last = 2 * q - 1
    lq = jnp.stack([jnp.concatenate([pr[:, 0, :, last], pr[:, 1, :, last]], -1),
                    jnp.concatenate([pi[:, 0, :, last], pi[:, 1, :, last]], -1)], axis=2)
    dvec = jnp.tile(d_skip[:, :, None, :], (1, 1, q, 1)).reshape(nl, SSM_G, 1, CW)
    return tm.astype(BF16), rm.astype(BF16), om.astype(BF16), lq, dvec


def _scan_kernel(x_ref, t_ref, r_ref, o_ref, lq_ref, d_ref, h0_ref, y_ref, fin_ref,
                 rr_ref, st_ref, *, n_chunks, bsz):
    p2 = 2 * SSM_P
    x = x_ref[...]
    rr_ref[...] = jnp.dot(x, r_ref[...], preferred_element_type=F32)
    lq = lq_ref[...]
    lre, lim = lq[0:1], lq[1:2]
    is_f = lax.broadcasted_iota(jnp.int32, (bsz, p2), 1) < SSM_P
    h0 = h0_ref[...]

    def step(j, carry):
        sre, sim = carry
        rf = pl.multiple_of(j * bsz, bsz)
        rb = pl.multiple_of((n_chunks - 1 - j) * bsz, bsz)
        st_ref[pl.ds(rf, bsz), 0:SSM_P] = sre[:, 0:SSM_P]
        st_ref[pl.ds(rb, bsz), SSM_P:p2] = sre[:, SSM_P:p2]
        st_ref[pl.ds(rf, bsz), p2:p2 + SSM_P] = sim[:, 0:SSM_P]
        st_ref[pl.ds(rb, bsz), p2 + SSM_P:2 * p2] = sim[:, SSM_P:p2]
        r_re = jnp.where(is_f, rr_ref[pl.ds(rf, bsz), 0:p2], rr_ref[pl.ds(rb, bsz), 0:p2])
        r_im = jnp.where(is_f, rr_ref[pl.ds(rf, bsz), p2:2 * p2], rr_ref[pl.ds(rb, bsz), p2:2 * p2])
        return lre * sre - lim * sim + r_re, lre * sim + lim * sre + r_im

    sre, sim = lax.fori_loop(0, n_chunks, step, (h0[:, 0:p2], h0[:, p2:2 * p2]))
    fin_ref[...] = jnp.concatenate([sre, sim], axis=1)
    y = jnp.dot(x, t_ref[...], preferred_element_type=F32)
    y = y + jnp.dot(st_ref[...].astype(BF16), o_ref[...], preferred_element_type=F32)
    y = y + d_ref[...] * x.astype(F32)
    y_ref[...] = jax.nn.gelu(y, approximate=True).astype(BF16)


def _scan(xg, tm, rm, om, lq, dvec, h0, bsz):
    g, n, _ = xg.shape
    n_chunks = n // bsz
    mat = pl.BlockSpec((None, CW, CW), lambda i: (i, 0, 0))
    return pl.pallas_call(
        functools.partial(_scan_kernel, n_chunks=n_chunks, bsz=bsz),
        grid=(g,),
        in_specs=[pl.BlockSpec((None, n, CW), lambda i: (i, 0, 0)), mat, mat, mat,
                  pl.BlockSpec((None, 2, 2 * SSM_P), lambda i: (i, 0, 0)),
                  pl.BlockSpec((None, 1, CW), lambda i: (i, 0, 0)),
                  pl.BlockSpec((None, bsz, 4 * SSM_P), lambda i: (i, 0, 0))],
        out_specs=[pl.BlockSpec((None, n, CW), lambda i: (i, 0, 0)),
                   pl.BlockSpec((None, bsz, 4 * SSM_P), lambda i: (i, 0, 0))],
        out_shape=[jax.ShapeDtypeStruct((g, n, CW), BF16),
                   jax.ShapeDtypeStruct((g, bsz, 4 * SSM_P), F32)],
        scratch_shapes=[pltpu.VMEM((n, 4 * SSM_P), F32), pltpu.VMEM((n, 4 * SSM_P), F32)],
        compiler_params=_cp("arbitrary"),
        name="s5_scan",
    )(xg, tm, rm, om, lq, dvec, h0)


def _chunk_perm():
    n = CHUNK * LANES
    src = (jnp.arange(n, dtype=jnp.int32).reshape(CHUNK, GROUPS_PER_TILE, SSM_H)
           .transpose(1, 0, 2).reshape(-1))
    return (jnp.arange(n, dtype=jnp.int32)[:, None] == src[None, :]).astype(BF16)


def _to_chunks(u, bsz, length, perm):
    nc = length // CHUNK
    z = (u.reshape(bsz, nc, CHUNK, LANE_TILES, LANES).transpose(3, 1, 0, 2, 4)
         .reshape(LANE_TILES, nc * bsz, CHUNK * LANES))
    w = jnp.einsum('jnk,kc->jnc', z, perm, preferred_element_type=BF16)
    return (w.reshape(LANE_TILES, nc * bsz, GROUPS_PER_TILE, CW).transpose(0, 2, 1, 3)
            .reshape(SSM_G, nc * bsz, CW))


def _from_chunks(y, bsz, length, perm):
    nc = length // CHUNK
    w = (y.reshape(LANE_TILES, GROUPS_PER_TILE, nc * bsz, CW).transpose(0, 2, 1, 3)
         .reshape(LANE_TILES, nc * bsz, CHUNK * LANES))
    z = jnp.einsum('jnc,kc->jnk', w, perm, preferred_element_type=BF16)
    return (z.reshape(LANE_TILES, nc, bsz, CHUNK, LANES).transpose(2, 1, 3, 0, 4)
            .reshape(bsz * length, D_SSM))


def _conv_post(cv, p_ref):
    p = p_ref[...]
    cv = cv + p[0:1]
    y = _silu(_ln(cv) * p[1:2] + p[2:3])
    return y * lax.rsqrt(jnp.mean(y * y, axis=-1, keepdims=True) + LN_EPS) * p[3:4]


def _conv_seq_tile(buf_ref, base, w_ref, b0, c0):
    win = buf_ref[pl.ds(base, CONV_WIN), b0:b0 + LANES]
    acc = jnp.zeros((GRID_W, LANES), F32)
    for ph in range(8):
        wb = win if ph == 0 else pltpu.roll(win, CONV_WIN - ph, axis=0)
        for a in range(4):
            j = 8 * a + ph - 1
            if 0 <= j < CONV_K:
                acc = acc + w_ref[j:j + 1, c0:c0 + LANES] * wb[8 * a:8 * a + GRID_W]
    return acc


def _conv_seq(buf_ref, base, w_ref, c0, width):
    return jnp.concatenate([_conv_seq_tile(buf_ref, base, w_ref, b, c0 + b)
                            for b in range(0, width, LANES)], axis=1)


def _conv_lat_kernel(z_ref, w_ref, p_ref, o_ref, hb_ref, vb_ref, cv_ref, *, rows):
    half = D_CONV // 2
    hstride = GRID_W + 16
    hb_ref[...] = jnp.zeros(hb_ref.shape, F32)
    vb_ref[pl.ds(0, CONV_PAD * GRID_W), :] = jnp.zeros((CONV_PAD * GRID_W, half), F32)
    vb_ref[pl.ds((CONV_PAD + rows) * GRID_W, CONV_PAD * GRID_W), :] = jnp.zeros((CONV_PAD * GRID_W, half), F32)
    for r in range(rows):
        hb_ref[pl.ds(16 + r * hstride, GRID_W), :] = z_ref[pl.ds(r * GRID_W, GRID_W), 0:half].astype(F32)
    vb_ref[pl.ds(CONV_PAD * GRID_W, rows * GRID_W), :] = z_ref[:, half:D_CONV].astype(F32)

    def along_row(b):
        def body(r, carry):
            hbase = pl.multiple_of(r * hstride, 16)
            vbase = pl.multiple_of(r * GRID_W, GRID_W)
            cv_ref[pl.ds(vbase, GRID_W), b:b + LANES] = _conv_seq_tile(hb_ref, hbase, w_ref, b, b)
            return carry
        lax.fori_loop(0, rows, body, 0)

    def along_col(b):
        def body(r, carry):
            vbase = pl.multiple_of(r * GRID_W, GRID_W)
            acc = jnp.zeros((GRID_W, LANES), F32)
            for j in range(CONV_K):
                acc = acc + (w_ref[j:j + 1, half + b:half + b + LANES]
                             * vb_ref[pl.ds(vbase + j * GRID_W, GRID_W), b:b + LANES])
            cv_ref[pl.ds(vbase, GRID_W), half + b:half + b + LANES] = acc
            return carry
        lax.fori_loop(0, rows, body, 0)

    for b in range(0, half, LANES):
        along_row(b)
        along_col(b)

    def post(r, carry):
        vbase = pl.multiple_of(r * GRID_W, GRID_W)
        o_ref[pl.ds(vbase, GRID_W), :] = _conv_post(cv_ref[pl.ds(vbase, GRID_W), :], p_ref).astype(BF16)
        return carry

    lax.fori_loop(0, rows, post, 0, unroll=4)


def _conv_lat(z, w, p, bsz, seq, blk0):
    rows = seq // GRID_W
    half = D_CONV // 2
    return pl.pallas_call(
        functools.partial(_conv_lat_kernel, rows=rows),
        grid=(bsz,),
        in_specs=[pl.BlockSpec((seq, D_CONV), lambda b: (blk0 + b, 0)),
                  pl.BlockSpec(w.shape, lambda b: (0, 0)),
                  pl.BlockSpec(p.shape, lambda b: (0, 0))],
        out_specs=pl.BlockSpec((seq, D_CONV), lambda b: (b, 0)),
        out_shape=jax.ShapeDtypeStruct((bsz * seq, D_CONV), BF16),
        scratch_shapes=[pltpu.VMEM((rows * (GRID_W + 16) + 16, half), F32),
                        pltpu.VMEM(((rows + 2 * CONV_PAD) * GRID_W, half), F32),
                        pltpu.VMEM((seq, D_CONV), F32)],
        compiler_params=_cp("arbitrary"),
        name="conv_latent",
    )(z, w, p)


def _conv_ctx_kernel(z_ref, w_ref, p_ref, o_ref, cb_ref, *, clen):
    cb_ref[pl.ds(0, 16), :] = jnp.zeros((16, D_CONV), F32)
    cb_ref[pl.ds(16 + clen, 16), :] = jnp.zeros((16, D_CONV), F32)
    cb_ref[pl.ds(16, clen), :] = z_ref[...].astype(F32)
    for blk in range(clen // GRID_W):
        base = blk * GRID_W
        acc = _conv_seq(cb_ref, base, w_ref, 0, D_CONV)
        o_ref[pl.ds(base, GRID_W), :] = _conv_post(acc, p_ref).astype(BF16)


def _conv_ctx(z, w, p, bsz, clen):
    return pl.pallas_call(
        functools.partial(_conv_ctx_kernel, clen=clen),
        grid=(bsz,),
        in_specs=[pl.BlockSpec((clen, D_CONV), lambda b: (b, 0)),
                  pl.BlockSpec(w.shape, lambda b: (0, 0)),
                  pl.BlockSpec(p.shape, lambda b: (0, 0))],
        out_specs=pl.BlockSpec((clen, D_CONV), lambda b: (b, 0)),
        out_shape=jax.ShapeDtypeStruct((bsz * clen, D_CONV), BF16),
        scratch_shapes=[pltpu.VMEM((clen + 32, D_CONV), F32)],
        compiler_params=_cp("arbitrary"),
        name="conv_context",
    )(z, w, p)


def _outproj_kernel(ysc_ref, ysl_ref, ycc_ref, ycl_ref, x_ref, mod_ref, wglu_ref, pv_ref, wo_ref, ln_ref,
                    o_ref, *, n_ctx_tiles):
    is_ctx = pl.program_id(0) < n_ctx_tiles
    ys = jnp.where(is_ctx, ysc_ref[...], ysl_ref[...])
    yc = jnp.where(is_ctx, ycc_ref[...], ycl_ref[...])
    pv = pv_ref[...]
    gl = jnp.dot(ys, wglu_ref[...], preferred_element_type=F32) + pv[0:1]
    yg = ys.astype(F32) * jax.nn.sigmoid(gl)
    yn = yg * lax.rsqrt(jnp.mean(yg * yg, axis=-1, keepdims=True) + LN_EPS) * pv[1:2]
    y = jnp.dot(yn.astype(BF16), wo_ref[0:D_SSM, :], preferred_element_type=F32)
    y = y + jnp.dot(yc, wo_ref[D_SSM:, :], preferred_element_type=F32)
    m = mod_ref[...]
    ln = ln_ref[...]
    o_ref[...] = _ln(ALPHA * x_ref[...] + m[2:3] * y) * ln[0:1] + ln[1:2]


def _outproj(ys_ctx, ys_lat, yc_ctx, yc_lat, xs, mod, wglu, pv, wo, ln, t_ctx, seq):
    t, d = xs.shape
    full = lambda a: pl.BlockSpec(a.shape, lambda i: (0, 0))
    nct = t_ctx // TMB
    ctx_tile = lambda i: (jnp.clip(i, 0, max(nct - 1, 0)), 0)
    lat_tile = lambda i: (jnp.maximum(i - nct, 0), 0)
    return pl.pallas_call(
        functools.partial(_outproj_kernel, n_ctx_tiles=nct),
        grid=(t // TMB,),
        in_specs=[pl.BlockSpec((TMB, D_SSM), ctx_tile),
                  pl.BlockSpec((TMB, D_SSM), lat_tile),
                  pl.BlockSpec((TMB, D_CONV), ctx_tile),
                  pl.BlockSpec((TMB, D_CONV), lat_tile),
                  pl.BlockSpec((TMB, d), lambda i: (i, 0)),
                  pl.BlockSpec((None, 6, d), lambda i: (_mod_row(i, t_ctx, seq, TMB), 0, 0)),
                  full(wglu), full(pv), full(wo), full(ln)],
        out_specs=pl.BlockSpec((TMB, d), lambda i: (i, 0)),
        out_shape=jax.ShapeDtypeStruct((t, d), F32),
        compiler_params=_cp("arbitrary"),
        name="outproj",
    )(ys_ctx, ys_lat, yc_ctx, yc_lat, xs, mod, wglu, pv, wo, ln)


ROW_TILE = 8


def _rows_to_tiles(ref, lead, val):
    n = val.shape[0]
    for c in range(ROW_TILE):
        ref[lead + (pl.ds(c, n, stride=ROW_TILE), slice(None))] = val[:, c * LANES:(c + 1) * LANES]


def _tiles_to_rows(ref, lead, n):
    return jnp.concatenate([ref[lead + (pl.ds(c, n, stride=ROW_TILE), slice(None))]
                            for c in range(ROW_TILE)], axis=1)


def _router_kernel(x_ref, mod_ref, wr_ref, rb_ref, h_ref, idx_ref, wt_ref, cnt_ref, carry_ref):
    @pl.when(pl.program_id(0) == 0)
    def _():
        carry_ref[...] = jnp.zeros(carry_ref.shape, F32)

    m = mod_ref[...]
    h = _ln(x_ref[...]) * (1.0 + m[4:5]) + m[3:4]
    _rows_to_tiles(h_ref, (), h)
    logits = lax.dot_general(wr_ref[...], h, (((1,), (1,)), ((), ())),
                             precision=lax.Precision.HIGHEST, preferred_element_type=F32)
    carry = carry_ref[...]
    for c0 in range(0, logits.shape[1], LANES):
        idx, wts, selm = _select_experts(logits[:, c0:c0 + LANES], rb_ref[...])
        idx_ref[:, c0:c0 + LANES] = idx
        wt_ref[:, c0:c0 + LANES] = wts
        carry = carry + jnp.sum(selm, axis=1, keepdims=True)
    carry_ref[...] = carry
    cnt_ref[...] = carry


def _select_experts(logits, bias):
    s = jax.nn.sigmoid(logits)
    biased = s + bias
    ninf = -jnp.inf
    tm = s.shape[1]

    row8 = lax.broadcasted_iota(jnp.int32, (GROUP_SIZE, tm), 0).astype(F32)
    gs = []
    for q in range(N_GROUPS):
        v = biased[q * GROUP_SIZE:(q + 1) * GROUP_SIZE]
        m1 = jnp.max(v, axis=0, keepdims=True)
        i1 = jnp.min(jnp.where(v == m1, row8, float(GROUP_SIZE)), axis=0, keepdims=True)
        m2 = jnp.max(jnp.where(row8 == i1, ninf, v), axis=0, keepdims=True)
        gs.append(m1 + m2)
    gsc = jnp.concatenate(gs, axis=0)
    rowg = lax.broadcasted_iota(jnp.int32, (N_GROUPS, tm), 0).astype(F32)
    gsel = jnp.zeros((N_GROUPS, tm), F32)
    for _ in range(TOP_K_GROUPS):
        mx = jnp.max(gsc, axis=0, keepdims=True)
        ii = jnp.min(jnp.where(gsc == mx, rowg, float(N_GROUPS)), axis=0, keepdims=True)
        hit = rowg == ii
        gsel = jnp.where(hit, 1.0, gsel)
        gsc = jnp.where(hit, ninf, gsc)
    emask = jnp.concatenate([jnp.broadcast_to(gsel[q:q + 1], (GROUP_SIZE, tm))
                             for q in range(N_GROUPS)], axis=0)
    masked = jnp.where(emask > 0.0, biased, ninf)

    rowe = lax.broadcasted_iota(jnp.int32, (N_EXPERTS, tm), 0).astype(F32)
    selm = jnp.zeros((N_EXPERTS, tm), F32)
    idxs, ws = [], []
    for _ in range(TOP_K):
        mx = jnp.max(masked, axis=0, keepdims=True)
        ii = jnp.min(jnp.where(masked == mx, rowe, float(N_EXPERTS)), axis=0, keepdims=True)
        hit = rowe == ii
        idxs.append(ii)
        ws.append(jnp.sum(jnp.where(hit, s, 0.0), axis=0, keepdims=True))
        selm = jnp.where(hit, 1.0, selm)
        masked = jnp.where(hit, ninf, masked)
    wsum = ws[0]
    for k in range(1, TOP_K):
        wsum = wsum + ws[k]
    return (jnp.concatenate(idxs, axis=0).astype(jnp.int32),
            jnp.concatenate([w / wsum * ROUTED_SCALE for w in ws], axis=0), selm)


def _router(xs, mod, wr_t, rb, t_ctx, seq):
    t, d = xs.shape
    full = lambda a: pl.BlockSpec(a.shape, lambda i: (0, 0))
    return pl.pallas_call(
        _router_kernel,
        grid=(t // TMB,),
        in_specs=[pl.BlockSpec((TMB, d), lambda i: (i, 0)),
                  pl.BlockSpec((None, 6, d), lambda i: (_mod_row(i, t_ctx, seq, TMB), 0, 0)),
                  full(wr_t), full(rb)],
        out_specs=[pl.BlockSpec((TMB * ROW_TILE, LANES), lambda i: (i, 0)),
                   pl.BlockSpec((TOP_K, TMB), lambda i: (0, i)),
                   pl.BlockSpec((TOP_K, TMB), lambda i: (0, i)),
                   pl.BlockSpec((N_EXPERTS, 1), lambda i: (0, 0))],
        out_shape=[jax.ShapeDtypeStruct((t * ROW_TILE, LANES), F32),
                   jax.ShapeDtypeStruct((TOP_K, t), jnp.int32),
                   jax.ShapeDtypeStruct((TOP_K, t), F32),
                   jax.ShapeDtypeStruct((N_EXPERTS, 1), F32)],
        scratch_shapes=[pltpu.VMEM((N_EXPERTS, 1), F32)],
        compiler_params=_cp("arbitrary"),
        name="router",
    )(xs, mod, wr_t, rb)


def _experts_kernel(te_ref, tv_ref, src0_ref, srcn_ref, dst_ref, h_ref, wg_ref, wu_ref, wd_ref,
                    y_ref, xbuf, ybuf, gsem, ssem):
    i = pl.program_id(0)
    valid = tv_ref[i] > 0
    prev_valid = tv_ref[jnp.maximum(i - 1, 0)] > 0
    rows = TME * ROW_TILE

    def gather_start(idx_ref, s):
        for r in range(TME):
            pltpu.make_async_copy(
                h_ref.at[pl.ds(pl.multiple_of(idx_ref[0, r], ROW_TILE), ROW_TILE), :],
                xbuf.at[s, pl.ds(r * ROW_TILE, ROW_TILE), :], gsem.at[s]).start()

    def scatter_start(s):
        for r in range(TME):
            pltpu.make_async_copy(
                ybuf.at[s, pl.ds(r * ROW_TILE, ROW_TILE), :],
                y_ref.at[pl.ds(pl.multiple_of(dst_ref[0, r], ROW_TILE), ROW_TILE), :],
                ssem.at[s]).start()

    def gather_wait(s):
        pltpu.make_async_copy(h_ref.at[pl.ds(0, rows), :], xbuf.at[s], gsem.at[s]).wait()

    def scatter_wait(s):
        pltpu.make_async_copy(ybuf.at[s], y_ref.at[pl.ds(0, rows), :], ssem.at[s]).wait()

    @pl.when(i == 0)
    def _():
        gather_start(src0_ref, 0)

    def step(s):
        @pl.when(valid)
        def _():
            @pl.when(i >= 2)
            def _():
                scatter_wait(s)

            gather_wait(s)
            gather_start(srcn_ref, 1 - s)
            x = _tiles_to_rows(xbuf, (s,), TME).astype(BF16)
            a = jnp.dot(x, wg_ref[...].astype(BF16), preferred_element_type=F32)
            b = jnp.dot(x, wu_ref[...].astype(BF16), preferred_element_type=F32)
            hid = (_silu(a) * b).astype(BF16)
            _rows_to_tiles(ybuf, (s,), jnp.dot(hid, wd_ref[...].astype(BF16), preferred_element_type=F32))
            scatter_start(s)

        @pl.when(jnp.logical_not(valid) & prev_valid & (i >= 1))
        def _():
            gather_wait(s)
            scatter_wait(1 - s)

            @pl.when(i >= 2)
            def _():
                scatter_wait(s)

            dump = pltpu.make_async_copy(ybuf.at[s], y_ref.at[pl.ds(y_ref.shape[0] - rows, rows), :],
                                         ssem.at[s])
            dump.start()
            dump.wait()

    for s in range(2):
        pl.when(i % 2 == s)(functools.partial(step, s))


def _experts(tile_expert, tile_valid, src, dst, h2, wg, wu, wd, layer, n_out):
    n_tiles = src.shape[0]
    d, f = wg.shape[2:]
    smem = lambda fn: pl.BlockSpec((None, 1, TME), fn, memory_space=pltpu.SMEM)
    grid_spec = pltpu.PrefetchScalarGridSpec(
        num_scalar_prefetch=2,
        grid=(n_tiles,),
        in_specs=[smem(lambda i, te, tv: (0, 0, 0)),
                  smem(lambda i, te, tv: (jnp.minimum(i + 1, n_tiles - 1), 0, 0)),
                  smem(lambda i, te, tv: (i, 0, 0)),
                  pl.BlockSpec(memory_space=pl.ANY),
                  pl.BlockSpec((None, None, d, f), lambda i, te, tv: (layer, te[i], 0, 0)),
                  pl.BlockSpec((None, None, d, f), lambda i, te, tv: (layer, te[i], 0, 0)),
                  pl.BlockSpec((None, None, f, d), lambda i, te, tv: (layer, te[i], 0, 0))],
        out_specs=pl.BlockSpec(memory_space=pl.ANY),
        scratch_shapes=[pltpu.VMEM((2, TME * ROW_TILE, LANES), F32),
                        pltpu.VMEM((2, TME * ROW_TILE, LANES), F32),
                        pltpu.SemaphoreType.DMA((2,)), pltpu.SemaphoreType.DMA((2,))],
    )
    return pl.pallas_call(
        _experts_kernel,
        grid_spec=grid_spec,
        out_shape=jax.ShapeDtypeStruct((n_out * ROW_TILE, LANES), F32),
        compiler_params=_cp("arbitrary"),
        name="experts",
    )(tile_expert, tile_valid, src, src, dst, h2, wg, wu, wd)


def _combine_kernel(*refs):
    yg_refs = refs[:TOP_K]
    wt_ref, h_ref, x_ref, mod_ref, wsg_ref, wsu_ref, wsd_ref, ln_ref, o_ref = refs[TOP_K:]
    wt = wt_ref[...]
    n = x_ref.shape[0]
    acc = wt[:, 0:1] * _tiles_to_rows(yg_refs[0], (), n)
    for k in range(1, TOP_K):
        acc = acc + wt[:, k:k + 1] * _tiles_to_rows(yg_refs[k], (), n)
    h = _tiles_to_rows(h_ref, (), n).astype(BF16)
    a = jnp.dot(h, wsg_ref[...], preferred_element_type=F32)
    b = jnp.dot(h, wsu_ref[...], preferred_element_type=F32)
    acc = acc + jnp.dot((_silu(a) * b).astype(BF16), wsd_ref[...], preferred_element_type=F32)
    m = mod_ref[...]
    ln = ln_ref[...]
    o_ref[...] = _ln(ALPHA * x_ref[...] + m[5:6] * acc) * ln[0:1] + ln[1:2]


def _combine(yg, wt, h2, xs, mod, wsg, wsu, wsd, ln, t_ctx, seq):
    t, d = xs.shape
    full = lambda a: pl.BlockSpec(a.shape, lambda i: (0, 0))
    n_blocks = t // TM
    slot_specs = [pl.BlockSpec((TM * ROW_TILE, LANES), functools.partial(lambda i, k: (k * n_blocks + i, 0), k=k))
                  for k in range(TOP_K)]
    return pl.pallas_call(
        _combine_kernel,
        grid=(n_blocks,),
        in_specs=slot_specs + [
                  pl.BlockSpec((TM, TOP_K), lambda i: (i, 0)),
                  pl.BlockSpec((TM * ROW_TILE, LANES), lambda i: (i, 0)),
                  pl.BlockSpec((TM, d), lambda i: (i, 0)),
                  pl.BlockSpec((None, 6, d), lambda i: (_mod_row(i, t_ctx, seq, TM), 0, 0)),
                  full(wsg), full(wsu), full(wsd), full(ln)],
        out_specs=pl.BlockSpec((TM, d), lambda i: (i, 0)),
        out_shape=jax.ShapeDtypeStruct((t, d), F32),
        compiler_params=_cp("arbitrary"),
        name="combine",
    )(*([yg] * TOP_K), wt, h2, xs, mod, wsg, wsu, wsd, ln)


def _moe(xs, mod, wr_t, rb, wg, wu, wd, layer, wsg, wsu, wsd, ln, t_ctx, seq):
    t, d = xs.shape
    h2, idx, wt, cnt = _router(xs, mod, wr_t, rb, t_ctx, seq)
    n_pairs = t * TOP_K
    n_rows = n_pairs + N_EXPERTS * TME
    n_tiles = n_rows // TME
    assert n_rows < (1 << KEY_SHIFT) and n_pairs % TME == 0
    counts = cnt[:, 0].astype(jnp.int32)
    n_padding = (-counts) % TME
    slot = jnp.arange(TME, dtype=jnp.int32)[None, :]
    pad_key = jnp.where(slot < n_padding[:, None], jnp.arange(N_EXPERTS, dtype=jnp.int32)[:, None],
                        N_EXPERTS).reshape(-1)
    keys = jnp.concatenate([idx.reshape(-1), pad_key]) * (1 << KEY_SHIFT) + jnp.arange(n_rows, dtype=jnp.int32)
    order = lax.sort(keys)
    pair = order & ((1 << KEY_SHIFT) - 1)
    src = ((pair % t) * ROW_TILE).reshape(n_tiles, 1, TME)
    dst = (jnp.where(pair < n_pairs, pair, n_pairs + pair % TME) * ROW_TILE).reshape(n_tiles, 1, TME)
    first = order[::TME] >> KEY_SHIFT
    tile_expert = jnp.minimum(first, N_EXPERTS - 1)
    tile_valid = (first < N_EXPERTS).astype(jnp.int32)
    ye = _experts(tile_expert, tile_valid, src, dst, h2, wg, wu, wd, layer, n_pairs + TME)
    return _combine(ye, wt.T, h2, xs, mod, wsg, wsu, wsd, ln, t_ctx, seq)


def kernel(x, c, ctx, c_ctx, w_ada, b_ada, w_in, ssm_a_re, ssm_a_im, ssm_log_dt, ssm_b_re, ssm_b_im,
           ssm_c_re, ssm_c_im, ssm_d, w_glu, b_glu, conv_w, conv_b, conv_ln_g, conv_ln_b, mix_norm_g,
           w_out, ln1_g, ln1_b, w_router, router_bias, we_gate, we_up, we_down, ws_gate, ws_up,
           ws_down, ln2_g, ln2_b):
    bsz, seq, d = x.shape
    clen = ctx.shape[1]
    depth = w_in.shape[0]
    t_ctx, t_lat = bsz * clen, bsz * seq
    assert d == D_MODEL and bsz + 1 <= MOD_ROWS
    assert seq % GRID_W == 0 and clen % GRID_W == 0 and t_ctx % seq == 0
    assert seq % TMB == 0 and t_ctx % TMB == 0 and seq % TM == 0 and t_ctx % TM == 0

    xs = jnp.concatenate([ctx.reshape(t_ctx, d), x.reshape(t_lat, d)], axis=0)
    cc = jnp.zeros((MOD_ROWS, d), F32).at[0].set(c_ctx).at[1:bsz + 1].set(c)
    mod_all = _ada(cc, w_ada, b_ada).reshape(depth, MOD_ROWS, 6, d)
    perm = _chunk_perm()
    s5 = _s5_mats(ssm_a_re, ssm_a_im, ssm_log_dt, ssm_b_re, ssm_b_im, ssm_c_re, ssm_c_im, ssm_d)

    for l in range(depth):
        mod = mod_all[l]
        u, z = _inproj(xs, mod, w_in[l].astype(BF16), t_ctx, seq)

        tm, rm, om, lq, dvec = (a[l] for a in s5)
        h0 = jnp.zeros((SSM_G, bsz, 4 * SSM_P), F32)
        yc, fin = _scan(_to_chunks(u[:t_ctx], bsz, clen, perm), tm, rm, om, lq, dvec, h0, bsz)
        yl, _ = _scan(_to_chunks(u[t_ctx:], bsz, seq, perm), tm, rm, om, lq, dvec, fin, bsz)
        cp = jnp.stack([conv_b[l], conv_ln_g[l], conv_ln_b[l], mix_norm_g[l, D_SSM:]])
        pv = jnp.stack([b_glu[l], mix_norm_g[l, :D_SSM]])
        ys_lat = _from_chunks(yl, bsz, seq, perm)
        yc_lat = _conv_lat(z, conv_w[l], cp, bsz, seq, t_ctx // seq)
        if l == depth - 1:
            xs, ys_ctx, yc_ctx, n_ctx = xs[t_ctx:], ys_lat, yc_lat, 0
        else:
            ys_ctx, yc_ctx, n_ctx = _from_chunks(yc, bsz, clen, perm), _conv_ctx(z, conv_w[l], cp, bsz, clen), t_ctx
        xs = _outproj(ys_ctx, ys_lat, yc_ctx, yc_lat, xs, mod, w_glu[l].astype(BF16), pv,
                      w_out[l].astype(BF16), jnp.stack([ln1_g[l], ln1_b[l]]), n_ctx, seq)

        xs = _moe(xs, mod, w_router[l].T, router_bias[l][:, None], we_gate, we_up, we_down, l,
                  ws_gate[l].astype(BF16), ws_up[l].astype(BF16), ws_down[l].astype(BF16),
                  jnp.stack([ln2_g[l], ln2_b[l]]), n_ctx, seq)

    return xs.reshape(bsz, seq, d)
```

```python
import functools
import math

import jax
import jax.numpy as jnp
from jax import lax
from jax.experimental import pallas as pl
from jax.experimental.pallas import tpu as pltpu

F32 = jnp.float32
BF16 = jnp.bfloat16

D_MODEL = 1024
DEPTH = 4
GRID_W = 64
D_SSM = 512
D_CONV = 512
SSM_H = 16
SSM_G = 32
SSM_P = 64
CONV_K = 31
CONV_PAD = 15
CONV_WIN = GRID_W + 32
N_EXPERTS = 64
TOP_K = 8
N_GROUPS = 8
TOP_K_GROUPS = 4
GROUP_SIZE = N_EXPERTS // N_GROUPS
D_EXPERT = 256
ROUTED_SCALE = 2.5
ALPHA = (2 * DEPTH) ** 0.25
LN_EPS = 1e-5

LANES = 128
CHUNK = 16
CW = CHUNK * SSM_H
LANE_TILES = D_SSM // LANES
GROUPS_PER_TILE = LANES // SSM_H
MOD_ROWS = 32
TM = 256
TMB = 1024
TME = 512
KEY_SHIFT = 19
VMEM_LIMIT = 48 * 1024 * 1024


def _cp(*sem):
    return pltpu.CompilerParams(dimension_semantics=sem, vmem_limit_bytes=VMEM_LIMIT)


def _ln(x):
    mu = jnp.mean(x, axis=-1, keepdims=True)
    xc = x - mu
    var = jnp.mean(xc * xc, axis=-1, keepdims=True)
    return xc * lax.rsqrt(var + LN_EPS)


def _silu(x):
    return x * jax.nn.sigmoid(x)


def _ada_kernel(c_ref, w_ref, b_ref, o_ref):
    a = _silu(c_ref[...]).astype(BF16)
    o_ref[...] = jnp.dot(a, w_ref[...].astype(BF16), preferred_element_type=F32) + b_ref[...]


def _ada(cc, w_ada, b_ada):
    depth, d, n = w_ada.shape
    tn = 1536
    return pl.pallas_call(
        _ada_kernel,
        grid=(depth, n // tn),
        in_specs=[pl.BlockSpec((MOD_ROWS, d), lambda l, j: (0, 0)),
                  pl.BlockSpec((None, d, tn), lambda l, j: (l, 0, j)),
                  pl.BlockSpec((None, 1, tn), lambda l, j: (l, 0, j))],
        out_specs=pl.BlockSpec((None, MOD_ROWS, tn), lambda l, j: (l, 0, j)),
        out_shape=jax.ShapeDtypeStruct((depth, MOD_ROWS, n), F32),
        compiler_params=_cp("arbitrary", "arbitrary"),
        name="ada",
    )(cc, w_ada, b_ada.reshape(depth, 1, n))


def _inproj_kernel(x_ref, mod_ref, w_ref, u_ref, z_ref):
    m = mod_ref[...]
    h = _ln(x_ref[...]) * (1.0 + m[1:2]) + m[0:1]
    p = jnp.dot(h.astype(BF16), w_ref[...], preferred_element_type=F32)
    u_ref[...] = p[:, :D_SSM].astype(BF16)
    v = p[:, D_SSM:D_SSM + D_CONV]
    g = p[:, D_SSM + D_CONV:]
    z_ref[...] = (v * jax.nn.sigmoid(g)).astype(BF16)


def _mod_row(i, t_ctx, seq, tile):
    start = i * tile
    return jnp.where(start < t_ctx, 0, 1 + (start - t_ctx) // seq)


def _inproj(xs, mod, w, t_ctx, seq):
    t, d = xs.shape
    return pl.pallas_call(
        _inproj_kernel,
        grid=(t // TMB,),
        in_specs=[pl.BlockSpec((TMB, d), lambda i: (i, 0)),
                  pl.BlockSpec((None, 6, d), lambda i: (_mod_row(i, t_ctx, seq, TMB), 0, 0)),
                  pl.BlockSpec(w.shape, lambda i: (0, 0))],
        out_specs=[pl.BlockSpec((TMB, D_SSM), lambda i: (i, 0)),
                   pl.BlockSpec((TMB, D_CONV), lambda i: (i, 0))],
        out_shape=[jax.ShapeDtypeStruct((t, D_SSM), BF16),
                   jax.ShapeDtypeStruct((t, D_CONV), BF16)],
        compiler_params=_cp("arbitrary"),
        name="inproj",
    )(xs, mod, w)


def _s5_mats(a_re, a_im, log_dt, b_re, b_im, c_re, c_im, d_skip):
    hp = lax.Precision.HIGHEST
    q = CHUNK
    nl = a_re.shape[0]
    dt = jnp.exp(log_dt)[..., None]
    ldr, ldi = a_re * dt, a_im * dt
    ks = jnp.arange(-(q - 1), q + 1, dtype=F32)[:, None]
    mag = jnp.exp(ks * ldr[..., None, :])
    ang = ks * ldi[..., None, :]
    pr, pi = mag * jnp.cos(ang), mag * jnp.sin(ang)

    def powers(d, lo, hi, rev=False):
        r, i = pr[:, d, :, lo + q - 1:hi + q - 1], pi[:, d, :, lo + q - 1:hi + q - 1]
        return (r[:, :, ::-1], i[:, :, ::-1]) if rev else (r, i)

    nr, ni = pr[..., q, :] - 1.0, pi[..., q, :]
    den = a_re * a_re + a_im * a_im
    qr, qi = (nr * a_re + ni * a_im) / den, (ni * a_re - nr * a_im) / den
    bbr = qr[..., None] * b_re - qi[..., None] * b_im
    bbi = qr[..., None] * b_im + qi[..., None] * b_re

    def c_times(d, pw):
        cr, ci = c_re[:, d][:, :, None], c_im[:, d][:, :, None]
        r, i = pw[0][:, :, :, None, :], pw[1][:, :, :, None, :]
        return cr * r - ci * i, cr * i + ci * r

    def b_times(d, pw):
        br, bi = bbr[:, d][:, :, :, None, :], bbi[:, d][:, :, :, None, :]
        r, i = pw[0].swapaxes(2, 3)[..., None], pw[1].swapaxes(2, 3)[..., None]
        return r * br - i * bi, r * bi + i * br

    def b_times_t(d, pw):
        br, bi = bbr[:, d].swapaxes(-1, -2)[:, :, None], bbi[:, d].swapaxes(-1, -2)[:, :, None]
        r, i = pw[0][:, :, :, None, :], pw[1][:, :, :, None, :]
        return r * br - i * bi, r * bi + i * br

    def lag_kernel(d, pw_t, pw_s):
        ur, ui = c_times(d, pw_t)
        vr, vi = b_times(d, pw_s)
        ur, ui = ur.reshape(nl, SSM_G, CW, SSM_P), ui.reshape(nl, SSM_G, CW, SSM_P)
        vr, vi = vr.reshape(nl, SSM_G, SSM_P, CW), vi.reshape(nl, SSM_G, SSM_P, CW)
        return (jnp.einsum('lgpa,lgbp->lgab', vr, ur, precision=hp)
                - jnp.einsum('lgpa,lgbp->lgab', vi, ui, precision=hp))

    kf = lag_kernel(0, powers(0, 0, q), powers(0, -(q - 1), 1, rev=True))
    kb = lag_kernel(1, powers(1, -(q - 1), 1, rev=True), powers(1, 0, q))
    s_i = (jnp.arange(CW) // SSM_H)[:, None]
    t_i = (jnp.arange(CW) // SSM_H)[None, :]
    tm = jnp.where(t_i >= s_i, kf, 0.0) + jnp.where(s_i >= t_i, kb, 0.0)

    f_re, f_im = b_times_t(0, powers(0, 0, q, rev=True))
    g_re, g_im = b_times_t(1, powers(1, 0, q))
    rm = jnp.concatenate([f_re, g_re, f_im, g_im], axis=-1).reshape(nl, SSM_G, CW, 4 * SSM_P)

    of_r, of_i = c_times(0, powers(0, 1, q + 1))
    ob_r, ob_i = c_times(1, powers(1, 1, q + 1, rev=True))
    fix_o = lambda a: a.transpose(0, 1, 4, 2, 3).reshape(nl, SSM_G, SSM_P, CW)
    om = jnp.concatenate([fix_o(of_r), fix_o(ob_r), fix_o(-of_i), fix_o(-ob_i)], axis=2)

    last = 2 * q - 1
    lq = jnp.stack([jnp.concatenate([pr[:, 0, :, last], pr[:, 1, :, last]], -1),
                    jnp.concatenate([pi[:, 0, :, last], pi[:, 1, :, last]], -1)], axis=2)
    dvec = jnp.tile(d_skip[:, :, None, :], (1, 1, q, 1)).reshape(nl, SSM_G, 1, CW)
    return tm.astype(BF16), rm.astype(BF16), om.astype(BF16), lq, dvec


def _scan_kernel(x_ref, t_ref, r_ref, o_ref, lq_ref, d_ref, h0_ref, y_ref, fin_ref,
                 rr_ref, st_ref, *, n_chunks, bsz):
    p2 = 2 * SSM_P
    x = x_ref[...]
    rr_ref[...] = jnp.dot(x, r_ref[...], preferred_element_type=F32)
    lq = lq_ref[...]
    lre, lim = lq[0:1], lq[1:2]
    is_f = lax.broadcasted_iota(jnp.int32, (bsz, p2), 1) < SSM_P
    h0 = h0_ref[...]

    def step(j, carry):
        sre, sim = carry
        rf = pl.multiple_of(j * bsz, bsz)
        rb = pl.multiple_of((n_chunks - 1 - j) * bsz, bsz)
        st_ref[pl.ds(rf, bsz), 0:SSM_P] = sre[:, 0:SSM_P]
        st_ref[pl.ds(rb, bsz), SSM_P:p2] = sre[:, SSM_P:p2]
        st_ref[pl.ds(rf, bsz), p2:p2 + SSM_P] = sim[:, 0:SSM_P]
        st_ref[pl.ds(rb, bsz), p2 + SSM_P:2 * p2] = sim[:, SSM_P:p2]
        r_re = jnp.where(is_f, rr_ref[pl.ds(rf, bsz), 0:p2], rr_ref[pl.ds(rb, bsz), 0:p2])
        r_im = jnp.where(is_f, rr_ref[pl.ds(rf, bsz), p2:2 * p2], rr_ref[pl.ds(rb, bsz), p2:2 * p2])
        return lre * sre - lim * sim + r_re, lre * sim + lim * sre + r_im

    sre, sim = lax.fori_loop(0, n_chunks, step, (h0[:, 0:p2], h0[:, p2:2 * p2]))
    fin_ref[...] = jnp.concatenate([sre, sim], axis=1)
    y = jnp.dot(x, t_ref[...], preferred_element_type=F32)
    y = y + jnp.dot(st_ref[...].astype(BF16), o_ref[...], preferred_element_type=F32)
    y = y + d_ref[...] * x.astype(F32)
    y_ref[...] = jax.nn.gelu(y, approximate=True).astype(BF16)


def _scan(xg, tm, rm, om, lq, dvec, h0, bsz):
    g, n, _ = xg.shape
    n_chunks = n // bsz
    mat = pl.BlockSpec((None, CW, CW), lambda i: (i, 0, 0))
    return pl.pallas_call(
        functools.partial(_scan_kernel, n_chunks=n_chunks, bsz=bsz),
        grid=(g,),
        in_specs=[pl.BlockSpec((None, n, CW), lambda i: (i, 0, 0)), mat, mat, mat,
                  pl.BlockSpec((None, 2, 2 * SSM_P), lambda i: (i, 0, 0)),
                  pl.BlockSpec((None, 1, CW), lambda i: (i, 0, 0)),
                  pl.BlockSpec((None, bsz, 4 * SSM_P), lambda i: (i, 0, 0))],
        out_specs=[pl.BlockSpec((None, n, CW), lambda i: (i, 0, 0)),
                   pl.BlockSpec((None, bsz, 4 * SSM_P), lambda i: (i, 0, 0))],
        out_shape=[jax.ShapeDtypeStruct((g, n, CW), BF16),
                   jax.ShapeDtypeStruct((g, bsz, 4 * SSM_P), F32)],
        scratch_shapes=[pltpu.VMEM((n, 4 * SSM_P), F32), pltpu.VMEM((n, 4 * SSM_P), F32)],
        compiler_params=_cp("arbitrary"),
        name="s5_scan",
    )(xg, tm, rm, om, lq, dvec, h0)


def _chunk_perm():
    n = CHUNK * LANES
    src = (jnp.arange(n, dtype=jnp.int32).reshape(CHUNK, GROUPS_PER_TILE, SSM_H)
           .transpose(1, 0, 2).reshape(-1))
    return (jnp.arange(n, dtype=jnp.int32)[:, None] == src[None, :]).astype(BF16)


def _to_chunks(u, bsz, length, perm):
    nc = length // CHUNK
    z = (u.reshape(bsz, nc, CHUNK, LANE_TILES, LANES).transpose(3, 1, 0, 2, 4)
         .reshape(LANE_TILES, nc * bsz, CHUNK * LANES))
    w = jnp.einsum('jnk,kc->jnc', z, perm, preferred_element_type=BF16)
    return (w.reshape(LANE_TILES, nc * bsz, GROUPS_PER_TILE, CW).transpose(0, 2, 1, 3)
            .reshape(SSM_G, nc * bsz, CW))


def _from_chunks(y, bsz, length, perm):
    nc = length // CHUNK
    w = (y.reshape(LANE_TILES, GROUPS_PER_TILE, nc * bsz, CW).transpose(0, 2, 1, 3)
         .reshape(LANE_TILES, nc * bsz, CHUNK * LANES))
    z = jnp.einsum('jnc,kc->jnk', w, perm, preferred_element_type=BF16)
    return (z.reshape(LANE_TILES, nc, bsz, CHUNK, LANES).transpose(2, 1, 3, 0, 4)
            .reshape(bsz * length, D_SSM))


def _conv_post(cv, p_ref):
    p = p_ref[...]
    cv = cv + p[0:1]
    y = _silu(_ln(cv) * p[1:2] + p[2:3])
    return y * lax.rsqrt(jnp.mean(y * y, axis=-1, keepdims=True) + LN_EPS) * p[3:4]


def _conv_seq_tile(buf_ref, base, w_ref, b0, c0):
    win = buf_ref[pl.ds(base, CONV_WIN), b0:b0 + LANES]
    acc = jnp.zeros((GRID_W, LANES), F32)
    for ph in range(8):
        wb = win if ph == 0 else pltpu.roll(win, CONV_WIN - ph, axis=0)
        for a in range(4):
            j = 8 * a + ph - 1
            if 0 <= j < CONV_K:
                acc = acc + w_ref[j:j + 1, c0:c0 + LANES] * wb[8 * a:8 * a + GRID_W]
    return acc


def _conv_seq(buf_ref, base, w_ref, c0, width):
    return jnp.concatenate([_conv_seq_tile(buf_ref, base, w_ref, b, c0 + b)
                            for b in range(0, width, LANES)], axis=1)


def _conv_lat_kernel(z_ref, w_ref, p_ref, o_ref, hb_ref, vb_ref, cv_ref, *, rows):
    half = D_CONV // 2
    hstride = GRID_W + 16
    hb_ref[...] = jnp.zeros(hb_ref.shape, F32)
    vb_ref[pl.ds(0, CONV_PAD * GRID_W), :] = jnp.zeros((CONV_PAD * GRID_W, half), F32)
    vb_ref[pl.ds((CONV_PAD + rows) * GRID_W, CONV_PAD * GRID_W), :] = jnp.zeros((CONV_PAD * GRID_W, half), F32)
    for r in range(rows):
        hb_ref[pl.ds(16 + r * hstride, GRID_W), :] = z_ref[pl.ds(r * GRID_W, GRID_W), 0:half].astype(F32)
    vb_ref[pl.ds(CONV_PAD * GRID_W, rows * GRID_W), :] = z_ref[:, half:D_CONV].astype(F32)

    def along_row(b):
        def body(r, carry):
            hbase = pl.multiple_of(r * hstride, 16)
            vbase = pl.multiple_of(r * GRID_W, GRID_W)
            cv_ref[pl.ds(vbase, GRID_W), b:b + LANES] = _conv_seq_tile(hb_ref, hbase, w_ref, b, b)
            return carry
        lax.fori_loop(0, rows, body, 0)

    def along_col(b):
        def body(r, carry):
            vbase = pl.multiple_of(r * GRID_W, GRID_W)
            acc = jnp.zeros((GRID_W, LANES), F32)
            for j in range(CONV_K):
                acc = acc + (w_ref[j:j + 1, half + b:half + b + LANES]
                             * vb_ref[pl.ds(vbase + j * GRID_W, GRID_W), b:b + LANES])
            cv_ref[pl.ds(vbase, GRID_W), half + b:half + b + LANES] = acc
            return carry
        lax.fori_loop(0, rows, body, 0)

    for b in range(0, half, LANES):
        along_row(b)
        along_col(b)

    def post(r, carry):
        vbase = pl.multiple_of(r * GRID_W, GRID_W)
        o_ref[pl.ds(vbase, GRID_W), :] = _conv_post(cv_ref[pl.ds(vbase, GRID_W), :], p_ref).astype(BF16)
        return carry

    lax.fori_loop(0, rows, post, 0, unroll=4)


def _conv_lat(z, w, p, bsz, seq, blk0):
    rows = seq // GRID_W
    half = D_CONV // 2
    return pl.pallas_call(
        functools.partial(_conv_lat_kernel, rows=rows),
        grid=(bsz,),
        in_specs=[pl.BlockSpec((seq, D_CONV), lambda b: (blk0 + b, 0)),
                  pl.BlockSpec(w.shape, lambda b: (0, 0)),
                  pl.BlockSpec(p.shape, lambda b: (0, 0))],
        out_specs=pl.BlockSpec((seq, D_CONV), lambda b: (b, 0)),
        out_shape=jax.ShapeDtypeStruct((bsz * seq, D_CONV), BF16),
        scratch_shapes=[pltpu.VMEM((rows * (GRID_W + 16) + 16, half), F32),
                        pltpu.VMEM(((rows + 2 * CONV_PAD) * GRID_W, half), F32),
                        pltpu.VMEM((seq, D_CONV), F32)],
        compiler_params=_cp("arbitrary"),
        name="conv_latent",
    )(z, w, p)


def _conv_ctx_kernel(z_ref, w_ref, p_ref, o_ref, cb_ref, *, clen):
    cb_ref[pl.ds(0, 16), :] = jnp.zeros((16, D_CONV), F32)
    cb_ref[pl.ds(16 + clen, 16), :] = jnp.zeros((16, D_CONV), F32)
    cb_ref[pl.ds(16, clen), :] = z_ref[...].astype(F32)
    for blk in range(clen // GRID_W):
        base = blk * GRID_W
        acc = _conv_seq(cb_ref, base, w_ref, 0, D_CONV)
        o_ref[pl.ds(base, GRID_W), :] = _conv_post(acc, p_ref).astype(BF16)


def _conv_ctx(z, w, p, bsz, clen):
    return pl.pallas_call(
        functools.partial(_conv_ctx_kernel, clen=clen),
        grid=(bsz,),
        in_specs=[pl.BlockSpec((clen, D_CONV), lambda b: (b, 0)),
                  pl.BlockSpec(w.shape, lambda b: (0, 0)),
                  pl.BlockSpec(p.shape, lambda b: (0, 0))],
        out_specs=pl.BlockSpec((clen, D_CONV), lambda b: (b, 0)),
        out_shape=jax.ShapeDtypeStruct((bsz * clen, D_CONV), BF16),
        scratch_shapes=[pltpu.VMEM((clen + 32, D_CONV), F32)],
        compiler_params=_cp("arbitrary"),
        name="conv_context",
    )(z, w, p)


def _outproj_kernel(ysc_ref, ysl_ref, ycc_ref, ycl_ref, x_ref, mod_ref, wglu_ref, pv_ref, wo_ref, ln_ref,
                    o_ref, *, n_ctx_tiles):
    is_ctx = pl.program_id(0) < n_ctx_tiles
    ys = jnp.where(is_ctx, ysc_ref[...], ysl_ref[...])
    yc = jnp.where(is_ctx, ycc_ref[...], ycl_ref[...])
    pv = pv_ref[...]
    gl = jnp.dot(ys, wglu_ref[...], preferred_element_type=F32) + pv[0:1]
    yg = ys.astype(F32) * jax.nn.sigmoid(gl)
    yn = yg * lax.rsqrt(jnp.mean(yg * yg, axis=-1, keepdims=True) + LN_EPS) * pv[1:2]
    y = jnp.dot(yn.astype(BF16), wo_ref[0:D_SSM, :], preferred_element_type=F32)
    y = y + jnp.dot(yc, wo_ref[D_SSM:, :], preferred_element_type=F32)
    m = mod_ref[...]
    ln = ln_ref[...]
    o_ref[...] = _ln(ALPHA * x_ref[...] + m[2:3] * y) * ln[0:1] + ln[1:2]


def _outproj(ys_ctx, ys_lat, yc_ctx, yc_lat, xs, mod, wglu, pv, wo, ln, t_ctx, seq):
    t, d = xs.shape
    full = lambda a: pl.BlockSpec(a.shape, lambda i: (0, 0))
    nct = t_ctx // TMB
    ctx_tile = lambda i: (jnp.clip(i, 0, max(nct - 1, 0)), 0)
    lat_tile = lambda i: (jnp.maximum(i - nct, 0), 0)
    return pl.pallas_call(
        functools.partial(_outproj_kernel, n_ctx_tiles=nct),
        grid=(t // TMB,),
        in_specs=[pl.BlockSpec((TMB, D_SSM), ctx_tile),
                  pl.BlockSpec((TMB, D_SSM), lat_tile),
                  pl.BlockSpec((TMB, D_CONV), ctx_tile),
                  pl.BlockSpec((TMB, D_CONV), lat_tile),
                  pl.BlockSpec((TMB, d), lambda i: (i, 0)),
                  pl.BlockSpec((None, 6, d), lambda i: (_mod_row(i, t_ctx, seq, TMB), 0, 0)),
                  full(wglu), full(pv), full(wo), full(ln)],
        out_specs=pl.BlockSpec((TMB, d), lambda i: (i, 0)),
        out_shape=jax.ShapeDtypeStruct((t, d), F32),
        compiler_params=_cp("arbitrary"),
        name="outproj",
    )(ys_ctx, ys_lat, yc_ctx, yc_lat, xs, mod, wglu, pv, wo, ln)


ROW_TILE = 8


def _rows_to_tiles(ref, lead, val):
    n = val.shape[0]
    for c in range(ROW_TILE):
        ref[lead + (pl.ds(c, n, stride=ROW_TILE), slice(None))] = val[:, c * LANES:(c + 1) * LANES]


def _tiles_to_rows(ref, lead, n):
    return jnp.concatenate([ref[lead + (pl.ds(c, n, stride=ROW_TILE), slice(None))]
                            for c in range(ROW_TILE)], axis=1)


def _router_kernel(x_ref, mod_ref, wr_ref, rb_ref, h_ref, idx_ref, wt_ref, cnt_ref, carry_ref):
    @pl.when(pl.program_id(0) == 0)
    def _():
        carry_ref[...] = jnp.zeros(carry_ref.shape, F32)

    m = mod_ref[...]
    h = _ln(x_ref[...]) * (1.0 + m[4:5]) + m[3:4]
    _rows_to_tiles(h_ref, (), h)
    logits = lax.dot_general(wr_ref[...], h, (((1,), (1,)), ((), ())),
                             precision=lax.Precision.HIGHEST, preferred_element_type=F32)
    carry = carry_ref[...]
    for c0 in range(0, logits.shape[1], LANES):
        idx, wts, selm = _select_experts(logits[:, c0:c0 + LANES], rb_ref[...])
        idx_ref[:, c0:c0 + LANES] = idx
        wt_ref[:, c0:c0 + LANES] = wts
        carry = carry + jnp.sum(selm, axis=1, keepdims=True)
    carry_ref[...] = carry
    cnt_ref[...] = carry


def _select_experts(logits, bias):
    s = jax.nn.sigmoid(logits)
    biased = s + bias
    ninf = -jnp.inf
    tm = s.shape[1]

    row8 = lax.broadcasted_iota(jnp.int32, (GROUP_SIZE, tm), 0).astype(F32)
    gs = []
    for q in range(N_GROUPS):
        v = biased[q * GROUP_SIZE:(q + 1) * GROUP_SIZE]
        m1 = jnp.max(v, axis=0, keepdims=True)
        i1 = jnp.min(jnp.where(v == m1, row8, float(GROUP_SIZE)), axis=0, keepdims=True)
        m2 = jnp.max(jnp.where(row8 == i1, ninf, v), axis=0, keepdims=True)
        gs.append(m1 + m2)
    gsc = jnp.concatenate(gs, axis=0)
    rowg = lax.broadcasted_iota(jnp.int32, (N_GROUPS, tm), 0).astype(F32)
    gsel = jnp.zeros((N_GROUPS, tm), F32)
    for _ in range(TOP_K_GROUPS):
        mx = jnp.max(gsc, axis=0, keepdims=True)
        ii = jnp.min(jnp.where(gsc == mx, rowg, float(N_GROUPS)), axis=0, keepdims=True)
        hit = rowg == ii
        gsel = jnp.where(hit, 1.0, gsel)
        gsc = jnp.where(hit, ninf, gsc)
    emask = jnp.concatenate([jnp.broadcast_to(gsel[q:q + 1], (GROUP_SIZE, tm))
                             for q in range(N_GROUPS)], axis=0)
    masked = jnp.where(emask > 0.0, biased, ninf)

    rowe = lax.broadcasted_iota(jnp.int32, (N_EXPERTS, tm), 0).astype(F32)
    selm = jnp.zeros((N_EXPERTS, tm), F32)
    idxs, ws = [], []
    for _ in range(TOP_K):
        mx = jnp.max(masked, axis=0, keepdims=True)
        ii = jnp.min(jnp.where(masked == mx, rowe, float(N_EXPERTS)), axis=0, keepdims=True)
        hit = rowe == ii
        idxs.append(ii)
        ws.append(jnp.sum(jnp.where(hit, s, 0.0), axis=0, keepdims=True))
        selm = jnp.where(hit, 1.0, selm)
        masked = jnp.where(hit, ninf, masked)
    wsum = ws[0]
    for k in range(1, TOP_K):
        wsum = wsum + ws[k]
    return (jnp.concatenate(idxs, axis=0).astype(jnp.int32),
            jnp.concatenate([w / wsum * ROUTED_SCALE for w in ws], axis=0), selm)


def _router(xs, mod, wr_t, rb, t_ctx, seq):
    t, d = xs.shape
    full = lambda a: pl.BlockSpec(a.shape, lambda i: (0, 0))
    return pl.pallas_call(
        _router_kernel,
        grid=(t // TMB,),
        in_specs=[pl.BlockSpec((TMB, d), lambda i: (i, 0)),
                  pl.BlockSpec((None, 6, d), lambda i: (_mod_row(i, t_ctx, seq, TMB), 0, 0)),
                  full(wr_t), full(rb)],
        out_specs=[pl.BlockSpec((TMB * ROW_TILE, LANES), lambda i: (i, 0)),
                   pl.BlockSpec((TOP_K, TMB), lambda i: (0, i)),
                   pl.BlockSpec((TOP_K, TMB), lambda i: (0, i)),
                   pl.BlockSpec((N_EXPERTS, 1), lambda i: (0, 0))],
        out_shape=[jax.ShapeDtypeStruct((t * ROW_TILE, LANES), F32),
                   jax.ShapeDtypeStruct((TOP_K, t), jnp.int32),
                   jax.ShapeDtypeStruct((TOP_K, t), F32),
                   jax.ShapeDtypeStruct((N_EXPERTS, 1), F32)],
        scratch_shapes=[pltpu.VMEM((N_EXPERTS, 1), F32)],
        compiler_params=_cp("arbitrary"),
        name="router",
    )(xs, mod, wr_t, rb)


def _experts_kernel(te_ref, tv_ref, src0_ref, srcn_ref, dst_ref, h_ref, wg_ref, wu_ref, wd_ref,
                    y_ref, xbuf, ybuf, gsem, ssem):
    i = pl.program_id(0)
    valid = tv_ref[i] > 0
    prev_valid = tv_ref[jnp.maximum(i - 1, 0)] > 0
    rows = TME * ROW_TILE

    def gather_start(idx_ref, s):
        for r in range(TME):
            pltpu.make_async_copy(
                h_ref.at[pl.ds(pl.multiple_of(idx_ref[0, r], ROW_TILE), ROW_TILE), :],
                xbuf.at[s, pl.ds(r * ROW_TILE, ROW_TILE), :], gsem.at[s]).start()

    def scatter_start(s):
        for r in range(TME):
            pltpu.make_async_copy(
                ybuf.at[s, pl.ds(r * ROW_TILE, ROW_TILE), :],
                y_ref.at[pl.ds(pl.multiple_of(dst_ref[0, r], ROW_TILE), ROW_TILE), :],
                ssem.at[s]).start(priority=1)

    def gather_wait(s):
        pltpu.make_async_copy(h_ref.at[pl.ds(0, rows), :], xbuf.at[s], gsem.at[s]).wait()

    def scatter_wait(s):
        pltpu.make_async_copy(ybuf.at[s], y_ref.at[pl.ds(0, rows), :], ssem.at[s]).wait()

    @pl.when(i == 0)
    def _():
        gather_start(src0_ref, 0)

    def step(s):
        @pl.when(valid)
        def _():
            @pl.when(i >= 2)
            def _():
                scatter_wait(s)

            gather_wait(s)
            gather_start(srcn_ref, 1 - s)
            x = _tiles_to_rows(xbuf, (s,), TME).astype(BF16)
            a = jnp.dot(x, wg_ref[...].astype(BF16), preferred_element_type=F32)
            b = jnp.dot(x, wu_ref[...].astype(BF16), preferred_element_type=F32)
            hid = (_silu(a) * b).astype(BF16)
            _rows_to_tiles(ybuf, (s,), jnp.dot(hid, wd_ref[...].astype(BF16), preferred_element_type=F32))
            scatter_start(s)

        @pl.when(jnp.logical_not(valid) & prev_valid & (i >= 1))
        def _():
            gather_wait(s)
            scatter_wait(1 - s)

            @pl.when(i >= 2)
            def _():
                scatter_wait(s)

            dump = pltpu.make_async_copy(ybuf.at[s], y_ref.at[pl.ds(y_ref.shape[0] - rows, rows), :],
                                         ssem.at[s])
            dump.start()
            dump.wait()

    for s in range(2):
        pl.when(i % 2 == s)(functools.partial(step, s))


def _experts(tile_expert, tile_valid, src, dst, h2, wg, wu, wd, layer, n_out):
    n_tiles = src.shape[0]
    d, f = wg.shape[2:]
    smem = lambda fn: pl.BlockSpec((None, 1, TME), fn, memory_space=pltpu.SMEM)
    grid_spec = pltpu.PrefetchScalarGridSpec(
        num_scalar_prefetch=2,
        grid=(n_tiles,),
        in_specs=[smem(lambda i, te, tv: (0, 0, 0)),
                  smem(lambda i, te, tv: (jnp.minimum(i + 1, n_tiles - 1), 0, 0)),
                  smem(lambda i, te, tv: (i, 0, 0)),
                  pl.BlockSpec(memory_space=pl.ANY),
                  pl.BlockSpec((None, None, d, f), lambda i, te, tv: (layer, te[i], 0, 0)),
                  pl.BlockSpec((None, None, d, f), lambda i, te, tv: (layer, te[i], 0, 0)),
                  pl.BlockSpec((None, None, f, d), lambda i, te, tv: (layer, te[i], 0, 0))],
        out_specs=pl.BlockSpec(memory_space=pl.ANY),
        scratch_shapes=[pltpu.VMEM((2, TME * ROW_TILE, LANES), F32),
                        pltpu.VMEM((2, TME * ROW_TILE, LANES), F32),
                        pltpu.SemaphoreType.DMA((2,)), pltpu.SemaphoreType.DMA((2,))],
    )
    return pl.pallas_call(
        _experts_kernel,
        grid_spec=grid_spec,
        out_shape=jax.ShapeDtypeStruct((n_out * ROW_TILE, LANES), F32),
        compiler_params=_cp("arbitrary"),
        name="experts",
    )(tile_expert, tile_valid, src, src, dst, h2, wg, wu, wd)


def _combine_kernel(*refs):
    yg_refs = refs[:TOP_K]
    wt_ref, h_ref, x_ref, mod_ref, wsg_ref, wsu_ref, wsd_ref, ln_ref, o_ref = refs[TOP_K:]
    wt = wt_ref[...]
    n = x_ref.shape[0]
    acc = wt[:, 0:1] * _tiles_to_rows(yg_refs[0], (), n)
    for k in range(1, TOP_K):
        acc = acc + wt[:, k:k + 1] * _tiles_to_rows(yg_refs[k], (), n)
    h = _tiles_to_rows(h_ref, (), n).astype(BF16)
    a = jnp.dot(h, wsg_ref[...], preferred_element_type=F32)
    b = jnp.dot(h, wsu_ref[...], preferred_element_type=F32)
    acc = acc + jnp.dot((_silu(a) * b).astype(BF16), wsd_ref[...], preferred_element_type=F32)
    m = mod_ref[...]
    ln = ln_ref[...]
    o_ref[...] = _ln(ALPHA * x_ref[...] + m[5:6] * acc) * ln[0:1] + ln[1:2]


def _combine(yg, wt, h2, xs, mod, wsg, wsu, wsd, ln, t_ctx, seq):
    t, d = xs.shape
    full = lambda a: pl.BlockSpec(a.shape, lambda i: (0, 0))
    n_blocks = t // TM
    slot_specs = [pl.BlockSpec((TM * ROW_TILE, LANES), functools.partial(lambda i, k: (k * n_blocks + i, 0), k=k))
                  for k in range(TOP_K)]
    return pl.pallas_call(
        _combine_kernel,
        grid=(n_blocks,),
        in_specs=slot_specs + [
                  pl.BlockSpec((TM, TOP_K), lambda i: (i, 0)),
                  pl.BlockSpec((TM * ROW_TILE, LANES), lambda i: (i, 0)),
                  pl.BlockSpec((TM, d), lambda i: (i, 0)),
                  pl.BlockSpec((None, 6, d), lambda i: (_mod_row(i, t_ctx, seq, TM), 0, 0)),
                  full(wsg), full(wsu), full(wsd), full(ln)],
        out_specs=pl.BlockSpec((TM, d), lambda i: (i, 0)),
        out_shape=jax.ShapeDtypeStruct((t, d), F32),
        compiler_params=_cp("arbitrary"),
        name="combine",
    )(*([yg] * TOP_K), wt, h2, xs, mod, wsg, wsu, wsd, ln)


def _moe(xs, mod, wr_t, rb, wg, wu, wd, layer, wsg, wsu, wsd, ln, t_ctx, seq):
    t, d = xs.shape
    h2, idx, wt, cnt = _router(xs, mod, wr_t, rb, t_ctx, seq)
    n_pairs = t * TOP_K
    n_rows = n_pairs + N_EXPERTS * TME
    n_tiles = n_rows // TME
    assert n_rows < (1 << KEY_SHIFT) and n_pairs % TME == 0
    counts = cnt[:, 0].astype(jnp.int32)
    n_padding = (-counts) % TME
    slot = jnp.arange(TME, dtype=jnp.int32)[None, :]
    pad_key = jnp.where(slot < n_padding[:, None], jnp.arange(N_EXPERTS, dtype=jnp.int32)[:, None],
                        N_EXPERTS).reshape(-1)
    keys = jnp.concatenate([idx.reshape(-1), pad_key]) * (1 << KEY_SHIFT) + jnp.arange(n_rows, dtype=jnp.int32)
    order = lax.sort(keys)
    pair = order & ((1 << KEY_SHIFT) - 1)
    src = ((pair % t) * ROW_TILE).reshape(n_tiles, 1, TME)
    dst = (jnp.where(pair < n_pairs, pair, n_pairs + pair % TME) * ROW_TILE).reshape(n_tiles, 1, TME)
    first = order[::TME] >> KEY_SHIFT
    tile_expert = jnp.minimum(first, N_EXPERTS - 1)
    tile_valid = (first < N_EXPERTS).astype(jnp.int32)
    ye = _experts(tile_expert, tile_valid, src, dst, h2, wg, wu, wd, layer, n_pairs + TME)
    return _combine(ye, wt.T, h2, xs, mod, wsg, wsu, wsd, ln, t_ctx, seq)


def kernel(x, c, ctx, c_ctx, w_ada, b_ada, w_in, ssm_a_re, ssm_a_im, ssm_log_dt, ssm_b_re, ssm_b_im,
           ssm_c_re, ssm_c_im, ssm_d, w_glu, b_glu, conv_w, conv_b, conv_ln_g, conv_ln_b, mix_norm_g,
           w_out, ln1_g, ln1_b, w_router, router_bias, we_gate, we_up, we_down, ws_gate, ws_up,
           ws_down, ln2_g, ln2_b):
    bsz, seq, d = x.shape
    clen = ctx.shape[1]
    depth = w_in.shape[0]
    t_ctx, t_lat = bsz * clen, bsz * seq
    assert d == D_MODEL and bsz + 1 <= MOD_ROWS
    assert seq % GRID_W == 0 and clen % GRID_W == 0 and t_ctx % seq == 0
    assert seq % TMB == 0 and t_ctx % TMB == 0 and seq % TM == 0 and t_ctx % TM == 0

    xs = jnp.concatenate([ctx.reshape(t_ctx, d), x.reshape(t_lat, d)], axis=0)
    cc = jnp.zeros((MOD_ROWS, d), F32).at[0].set(c_ctx).at[1:bsz + 1].set(c)
    mod_all = _ada(cc, w_ada, b_ada).reshape(depth, MOD_ROWS, 6, d)
    perm = _chunk_perm()
    s5 = _s5_mats(ssm_a_re, ssm_a_im, ssm_log_dt, ssm_b_re, ssm_b_im, ssm_c_re, ssm_c_im, ssm_d)

    for l in range(depth):
        mod = mod_all[l]
        u, z = _inproj(xs, mod, w_in[l].astype(BF16), t_ctx, seq)

        tm, rm, om, lq, dvec = (a[l] for a in s5)
        h0 = jnp.zeros((SSM_G, bsz, 4 * SSM_P), F32)
        yc, fin = _scan(_to_chunks(u[:t_ctx], bsz, clen, perm), tm, rm, om, lq, dvec, h0, bsz)
        yl, _ = _scan(_to_chunks(u[t_ctx:], bsz, seq, perm), tm, rm, om, lq, dvec, fin, bsz)
        cp = jnp.stack([conv_b[l], conv_ln_g[l], conv_ln_b[l], mix_norm_g[l, D_SSM:]])
        pv = jnp.stack([b_glu[l], mix_norm_g[l, :D_SSM]])
        ys_lat = _from_chunks(yl, bsz, seq, perm)
        yc_lat = _conv_lat(z, conv_w[l], cp, bsz, seq, t_ctx // seq)
        if l == depth - 1:
            xs, ys_ctx, yc_ctx, n_ctx = xs[t_ctx:], ys_lat, yc_lat, 0
        else:
            ys_ctx, yc_ctx, n_ctx = _from_chunks(yc, bsz, clen, perm), _conv_ctx(z, conv_w[l], cp, bsz, clen), t_ctx
        xs = _outproj(ys_ctx, ys_lat, yc_ctx, yc_lat, xs, mod, w_glu[l].astype(BF16), pv,
                      w_out[l].astype(BF16), jnp.stack([ln1_g[l], ln1_b[l]]), n_ctx, seq)

        xs = _moe(xs, mod, w_router[l].T, router_bias[l][:, None], we_gate, we_up, we_down, l,
                  ws_gate[l].astype(BF16), ws_up[l].astype(BF16), ws_down[l].astype(BF16),
                  jnp.stack([ln2_g[l], ln2_b[l]]), n_ctx, seq)

    return xs.reshape(bsz, seq, d)
```
